```python
import math
import jax, jax.numpy as jnp
from jax import lax
import numpy as np


D_MODEL = 2048
BATCH = 2
SEQ = 8192
DEPTH = 1

CHUNK = 64
QBLOCK = 128
N_MEM = 256
EPS = 1e-6

GLA_HEADS = 4
GLA_DK = D_MODEL // 2 // GLA_HEADS
GLA_DV = D_MODEL // GLA_HEADS
GLA_GATE_RANK = 16
GLA_TAU = 16.0

DSA_HEADS = 16
DSA_DH = D_MODEL // DSA_HEADS
DSA_DV = D_MODEL // DSA_HEADS
DSA_LATENT = D_MODEL // 8
IDX_HEADS = 16
IDX_DIM = 128
INDEX_TOPK = 256

REL_BUCKETS = 32
REL_MAX_DIST = 128

XATTN_HEADS = 4
XATTN_DH = D_MODEL // XATTN_HEADS

D_FF = ((8 * D_MODEL // 3 + 255) // 256) * 256

IN_SPLITS = (
    GLA_HEADS * GLA_DK,
    GLA_HEADS * GLA_DK,
    GLA_HEADS * GLA_DV,
    GLA_HEADS * GLA_DV,
    GLA_GATE_RANK,
    DSA_HEADS * DSA_DH,
    DSA_LATENT,
    IDX_HEADS * IDX_DIM,
    IDX_DIM,
    IDX_HEADS,
    D_MODEL,
    D_MODEL,
)
IN_COLS = sum(IN_SPLITS)

kernel_name = 'hybrid_gla_dsa_stream_block'


def rms_norm(x, gain):
    xf = x.astype(jnp.float32)
    y = xf * lax.rsqrt(jnp.mean(xf * xf, axis=-1, keepdims=True) + EPS)
    return (y * gain.astype(jnp.float32)).astype(x.dtype)


def layer_norm(x, w, b):
    xf = x.astype(jnp.float32)
    mu = jnp.mean(xf, axis=-1, keepdims=True)
    xc = xf - mu
    var = jnp.mean(xc * xc, axis=-1, keepdims=True)
    return (xc * lax.rsqrt(var + EPS) * w.astype(jnp.float32) + b.astype(jnp.float32)).astype(x.dtype)


def split_columns(y, sizes):
    parts = []
    off = 0
    for s in sizes:
        parts.append(y[..., off:off + s])
        off += s
    return parts


def t5_bucket(rel):
    half = REL_BUCKETS // 2
    max_exact = half // 2
    ret = jnp.where(rel > 0, half, 0)
    n = jnp.abs(rel)
    nf = jnp.maximum(n, 1).astype(jnp.float32)
    large = max_exact + (jnp.log(nf / max_exact) / math.log(REL_MAX_DIST / max_exact)
                         * (half - max_exact)).astype(jnp.int32)
    large = jnp.minimum(large, half - 1)
    return ret + jnp.where(n < max_exact, n, large)


def gla_mixer(q, k, v, r, a_low, w_a2, b_a, out_norm):
    B, S, _ = q.shape
    nc = S // CHUNK
    f32 = jnp.float32
    log_a = jax.nn.log_sigmoid((a_low @ w_a2 + b_a).astype(f32)) / GLA_TAU

    def to_chunks(t, d):
        return t.astype(f32).reshape(B, nc, CHUNK, GLA_HEADS, d).transpose(1, 0, 3, 2, 4)

    qc = to_chunks(q, GLA_DK) * (GLA_DK ** -0.5)
    kc = to_chunks(k, GLA_DK)
    vc = to_chunks(v, GLA_DV)
    bc = jnp.cumsum(to_chunks(log_a, GLA_DK), axis=3)
    lower = jnp.tril(jnp.ones((CHUNK, CHUNK), dtype=bool))

    def step(state, inp):
        qi, ki, vi, bi = inp
        eb = jnp.exp(bi)
        ebi = jnp.exp(-bi)
        q_fwd = qi * eb
        a_lo = jnp.einsum('bhtd,bhsd->bhts', q_fwd, ki * ebi)
        a_up = jnp.einsum('bhtd,bhsd->bhts', qi * ebi, ki * eb)
        scores = jnp.where(lower, a_lo, a_up)
        o = (jnp.einsum('bhts,bhsv->bhtv', scores, vi)
             + jnp.einsum('bhtd,bhdv->bhtv', q_fwd, state))
        b_last = bi[:, :, -1, :]
        state = (jnp.exp(b_last)[..., None] * state
                 + jnp.einsum('bhsd,bhsv->bhdv', ki * jnp.exp(b_last[:, :, None, :] - bi), vi))
        return state, o

    state0 = jnp.zeros((B, GLA_HEADS, GLA_DK, GLA_DV), f32)
    _, o = lax.scan(step, state0, (qc, kc, vc, bc))
    o = o.transpose(1, 0, 3, 2, 4).reshape(B, S, GLA_HEADS, GLA_DV)
    o = rms_norm(o, out_norm).reshape(B, S, GLA_HEADS * GLA_DV)
    o = o * jax.nn.silu(r.astype(f32))
    return o.astype(q.dtype)


def dsa_mixer(q, c_lat, q_idx, k_idx, w_idx, w_uk, w_uv, rel_bias):
    B, S, _ = q.shape
    dtype = q.dtype
    f32 = jnp.float32
    top_k = min(INDEX_TOPK, S // 4)
    q = q.reshape(B, S, DSA_HEADS, DSA_DH)
    q_idx = q_idx.reshape(B, S, IDX_HEADS, IDX_DIM)
    w_idx = w_idx * (IDX_HEADS ** -0.5 * IDX_DIM ** -0.5)
    key_chunk = jnp.arange(S) // CHUNK
    gather = jax.vmap(lambda table, idx: table[idx])

    def block(start):
        qb = lax.dynamic_slice_in_dim(q, start, QBLOCK, axis=1)
        qib = lax.dynamic_slice_in_dim(q_idx, start, QBLOCK, axis=1)
        wb = lax.dynamic_slice_in_dim(w_idx, start, QBLOCK, axis=1)
        t_pos = start + jnp.arange(QBLOCK)
        t_chunk = t_pos // CHUNK
        dots = jnp.einsum('bqhd,bsd->bqhs', qib, k_idx)
        score = jnp.einsum('bqh,bqhs->bqs', wb, jax.nn.relu(dots)).astype(f32)
        score = jnp.where(key_chunk[None, None, :] <= t_chunk[None, :, None], score, -jnp.inf)
        _, idx = lax.top_k(score, top_k)
        valid = (idx // CHUNK) <= t_chunk[None, :, None]
        c_sel = gather(c_lat, idx)
        q_lat = jnp.einsum('bqhd,hdr->bqhr', qb, w_uk)
        logits = jnp.einsum('bqhr,bqkr->bqhk', q_lat, c_sel).astype(f32) * (DSA_DH ** -0.5)
        bias = rel_bias[t5_bucket(idx - t_pos[None, :, None])]
        logits = logits + jnp.moveaxis(bias, -1, 2).astype(f32)
        logits = jnp.where(valid[:, :, None, :], logits, -jnp.inf)
        p = jax.nn.softmax(logits, axis=-1).astype(dtype)
        o_lat = jnp.einsum('bqhk,bqkr->bqhr', p, c_sel)
        o = jnp.einsum('bqhr,hrv->bqhv', o_lat, w_uv)
        return o.reshape(B, QBLOCK, DSA_HEADS * DSA_DV)

    out = lax.map(block, jnp.arange(S // QBLOCK, dtype=jnp.int32) * QBLOCK)
    return out.transpose(1, 0, 2, 3).reshape(B, S, DSA_HEADS * DSA_DV)


def memory_cross_attention(u, mem_n, w_q, w_kv, w_o):
    B, S, D = u.shape
    M = mem_n.shape[1]
    q = (u @ w_q).reshape(B, S, XATTN_HEADS, XATTN_DH)
    k, v = jnp.split(mem_n @ w_kv, 2, axis=-1)
    k = k.reshape(B, M, XATTN_HEADS, XATTN_DH)
    v = v.reshape(B, M, XATTN_HEADS, XATTN_DH)
    logits = jnp.einsum('bshd,bmhd->bhsm', q, k).astype(jnp.float32) * (XATTN_DH ** -0.5)
    p = jax.nn.softmax(logits, axis=-1).astype(u.dtype)
    o = jnp.einsum('bhsm,bmhd->bshd', p, v).reshape(B, S, D)
    return o @ w_o


def swiglu(u, w_gate, w_up, w_down):
    return (jax.nn.silu(u @ w_gate) * (u @ w_up)) @ w_down


def setup_inputs(seed: int = 0) -> dict:
    key = jax.random.key(seed)
    ks = jax.random.split(key, 32)
    L = DEPTH
    D = D_MODEL

    def nrm(k, shape, scale):
        return jax.random.normal(k, shape, jnp.float32) * scale

    def gain(k, shape):
        return 1.0 + 0.02 * jax.random.normal(k, shape, jnp.float32)

    return {
        'x': nrm(ks[0], (BATCH, SEQ, D), 1.0),
        'mem': nrm(ks[1], (BATCH, N_MEM, D), 1.0),
        'w_in': nrm(ks[2], (L, D, IN_COLS), D ** -0.5),
        'gla_w_a2': nrm(ks[3], (L, GLA_GATE_RANK, GLA_HEADS * GLA_DK), GLA_GATE_RANK ** -0.5),
        'gla_b_a': nrm(ks[4], (L, GLA_HEADS * GLA_DK), 0.1),
        'gla_out_norm': gain(ks[5], (L, GLA_DV)),
        'dsa_w_uk': nrm(ks[6], (L, DSA_HEADS, DSA_DH, DSA_LATENT), DSA_LATENT ** -0.5),
        'dsa_w_uv': nrm(ks[7], (L, DSA_HEADS, DSA_LATENT, DSA_DV), DSA_LATENT ** -0.5),
        'dsa_latent_norm': gain(ks[8], (L, DSA_LATENT)),
        'idx_k_norm_w': gain(ks[9], (L, IDX_DIM)),
        'idx_k_norm_b': nrm(ks[10], (L, IDX_DIM), 0.02),
        'rel_bias': nrm(ks[11], (REL_BUCKETS, DSA_HEADS), 0.2),
        'w_gla_branch': nrm(ks[12], (L, GLA_HEADS * GLA_DV, D), (GLA_HEADS * GLA_DV) ** -0.5),
        'w_dsa_branch': nrm(ks[13], (L, DSA_HEADS * DSA_DV, D), (DSA_HEADS * DSA_DV) ** -0.5),
        'w_mix_out': nrm(ks[14], (L, D, D), D ** -0.5),
        'mix_pre_norm': gain(ks[15], (L, D)),
        'mix_post_norm': gain(ks[16], (L, D)),
        'xa_pre_norm': gain(ks[17], (L, D)),
        'xa_post_norm': gain(ks[18], (L, D)),
        'xa_mem_norm': gain(ks[19], (L, D)),
        'w_xa_q': nrm(ks[20], (L, D, D), D ** -0.5),
        'w_xa_kv': nrm(ks[21], (L, D, 2 * D), D ** -0.5),
        'w_xa_o': nrm(ks[22], (L, D, D), D ** -0.5),
        'ffn_pre_norm': gain(ks[23], (L, D)),
        'ffn_post_norm': gain(ks[24], (L, D)),
        'w_ffn_gate': nrm(ks[25], (L, D, D_FF), D ** -0.5),
        'w_ffn_up': nrm(ks[26], (L, D, D_FF), D ** -0.5),
        'w_ffn_down': nrm(ks[27], (L, D_FF, D), D_FF ** -0.5),
    }


def reference(x, mem, w_in, gla_w_a2, gla_b_a, gla_out_norm, dsa_w_uk, dsa_w_uv, dsa_latent_norm,
              idx_k_norm_w, idx_k_norm_b, rel_bias, w_gla_branch, w_dsa_branch, w_mix_out,
              mix_pre_norm, mix_post_norm, xa_pre_norm, xa_post_norm, xa_mem_norm,
              w_xa_q, w_xa_kv, w_xa_o, ffn_pre_norm, ffn_post_norm,
              w_ffn_gate, w_ffn_up, w_ffn_down):
    h = x
    for layer in range(DEPTH):
        u = rms_norm(h, mix_pre_norm[layer])
        proj = u @ w_in[layer]
        (g_q, g_k, g_v, g_r, g_a, d_q, d_c, i_q, i_k, i_w, gate_gla, gate_dsa) = split_columns(proj, IN_SPLITS)
        y_gla = gla_mixer(g_q, g_k, g_v, g_r, g_a, gla_w_a2[layer], gla_b_a[layer], gla_out_norm[layer])
        c_lat = rms_norm(d_c, dsa_latent_norm[layer])
        k_idx = layer_norm(i_k, idx_k_norm_w[layer], idx_k_norm_b[layer])
        y_dsa = dsa_mixer(d_q, c_lat, i_q, k_idx, i_w, dsa_w_uk[layer], dsa_w_uv[layer], rel_bias)
        merged = (jax.nn.sigmoid(gate_gla) * (y_gla @ w_gla_branch[layer])
                  + jax.nn.sigmoid(gate_dsa) * (y_dsa @ w_dsa_branch[layer]))
        h = h + rms_norm(merged @ w_mix_out[layer], mix_post_norm[layer])
        u = rms_norm(h, xa_pre_norm[layer])
        mem_n = rms_norm(mem, xa_mem_norm[layer])
        y = memory_cross_attention(u, mem_n, w_xa_q[layer], w_xa_kv[layer], w_xa_o[layer])
        h = h + rms_norm(y, xa_post_norm[layer])
        u = rms_norm(h, ffn_pre_norm[layer])
        y = swiglu(u, w_ffn_gate[layer], w_ffn_up[layer], w_ffn_down[layer])
        h = h + rms_norm(y, ffn_post_norm[layer])
    return h
```

```python
import functools
import math

import jax
import jax.numpy as jnp
import numpy as np
from jax import lax
from jax.experimental import pallas as pl
from jax.experimental.pallas import tpu as pltpu

F32 = jnp.float32
BF16 = jnp.bfloat16
I32 = jnp.int32

D_MODEL = 2048
CHUNK = 64
EPS = 1e-6

GLA_HEADS = 4
GLA_DK = 256
GLA_DV = 512
GLA_RANK = 16
GLA_TAU = 16.0

DSA_HEADS = 16
DSA_DH = 128
DSA_DV = 128
DSA_LATENT = 256
IDX_HEADS = 16
IDX_DIM = 128
INDEX_TOPK = 256

REL_BUCKETS = 32
REL_MAX_DIST = 128

XA_HEADS = 4
XA_DH = 512

_SPLITS = (1024, 1024, 2048, 2048, 16, 2048, 256, 2048, 128, 16, 2048, 2048)
_OFFS = tuple(int(v) for v in np.cumsum((0,) + _SPLITS))
(_O_GQ, _O_GK, _O_GV, _O_GR, _O_GA, _O_DQ, _O_DC, _O_IQ, _O_IK, _O_IW, _O_GG, _O_GD, _O_END) = _OFFS

MAIN_COLS = 14336
SMALL_COLS = 512

QBLK = 128
KCH = 512
PAD_FRONT = 128
INT_MIN = -2 ** 31
NEG = -1e30

VMEM_LIMIT = 56 * 1024 * 1024


def _cparams(sem):
    return pltpu.CompilerParams(dimension_semantics=sem, vmem_limit_bytes=VMEM_LIMIT)


def _rms(x, gain):
    ms = jnp.mean(x * x, axis=-1, keepdims=True)
    return x * lax.rsqrt(ms + EPS) * gain


def _dot(a, b):
    return jnp.dot(a, b, preferred_element_type=F32)


def _dot_nt(a, b):
    return lax.dot_general(a, b, (((1,), (1,)), ((), ())), preferred_element_type=F32)


def _norm_matmul_kernel(x_ref, g_ref, w_ref, o_ref, u_ref):
    @pl.when(pl.program_id(1) == 0)
    def _():
        u_ref[...] = _rms(x_ref[...], g_ref[...]).astype(BF16)

    o_ref[...] = _dot(u_ref[...], w_ref[...]).astype(o_ref.dtype)


def _norm_matmul(x, gain, w, out_dtype, tm, tn):
    t, d = x.shape
    n = w.shape[1]
    return pl.pallas_call(
        _norm_matmul_kernel,
        grid=(t // tm, n // tn),
        in_specs=[pl.BlockSpec((tm, d), lambda i, j: (i, 0)),
                  pl.BlockSpec((1, d), lambda i, j: (0, 0)),
                  pl.BlockSpec((d, tn), lambda i, j: (0, j))],
        out_specs=pl.BlockSpec((tm, tn), lambda i, j: (i, j)),
        out_shape=jax.ShapeDtypeStruct((t, n), out_dtype),
        scratch_shapes=[pltpu.VMEM((tm, d), BF16)],
        compiler_params=_cparams(("parallel", "arbitrary")),
        name="norm_matmul",
    )(x, gain, w)


def _small_proj(x, gain, w_small, lat_g, ik_w, ik_b, batch, seq):
    tm = PAD_FRONT
    nblk = seq // tm
    npad = (seq + PAD_FRONT + KCH - 1) // KCH * KCH + KCH
    ntail = (npad - PAD_FRONT - seq) // tm
    grid = (batch, 1 + nblk + ntail)

    def x_map(b, i):
        return (b * nblk + jnp.clip(i - 1, 0, nblk - 1), 0)

    def kernel(x_ref, g_ref, w_ref, lat_g_ref, ik_w_ref, ik_b_ref, c_ref, k_ref, misc_ref):
        i = pl.program_id(1)
        is_pad = jnp.logical_or(i == 0, i > nblk)

        @pl.when(is_pad)
        def _():
            c_ref[...] = jnp.zeros_like(c_ref)
            k_ref[...] = jnp.zeros_like(k_ref)
            misc_ref[...] = jnp.zeros_like(misc_ref)

        @pl.when(jnp.logical_not(is_pad))
        def _():
            u = _rms(x_ref[...], g_ref[...]).astype(BF16)
            p = _dot(u, w_ref[...])
            c_ref[0] = _rms(p[:, 0:256], lat_g_ref[...]).astype(BF16)
            ik = p[:, 256:384]
            mu = jnp.mean(ik, axis=-1, keepdims=True)
            xc = ik - mu
            var = jnp.mean(xc * xc, axis=-1, keepdims=True)
            k_ref[0] = (xc * lax.rsqrt(var + EPS) * ik_w_ref[...] + ik_b_ref[...]).astype(BF16)
            misc_ref[0] = p[:, 384:512]

    d = x.shape[1]
    const = lambda b, i: (0, 0)
    return pl.pallas_call(
        kernel,
        grid=grid,
        in_specs=[pl.BlockSpec((tm, d), x_map),
                  pl.BlockSpec((1, d), const),
                  pl.BlockSpec((d, SMALL_COLS), const),
                  pl.BlockSpec((1, DSA_LATENT), const),
                  pl.BlockSpec((1, IDX_DIM), const),
                  pl.BlockSpec((1, IDX_DIM), const)],
        out_specs=[pl.BlockSpec((1, tm, DSA_LATENT), lambda b, i: (b, i, 0)),
                   pl.BlockSpec((1, tm, IDX_DIM), lambda b, i: (b, i, 0)),
                   pl.BlockSpec((1, tm, 128), lambda b, i: (b, i, 0))],
        out_shape=[jax.ShapeDtypeStruct((batch, npad, DSA_LATENT), BF16),
                   jax.ShapeDtypeStruct((batch, npad, IDX_DIM), BF16),
                   jax.ShapeDtypeStruct((batch, npad, 128), F32)],
        compiler_params=_cparams(("parallel", "arbitrary")),
        name="small_proj",
    )(x, gain, w_small, lat_g, ik_w, ik_b)


def _log_sigmoid(z):
    return jnp.minimum(z, 0.0) - jnp.log1p(jnp.exp(-jnp.abs(z)))


def _split3(x):
    h = x.astype(BF16)
    r = x - h.astype(F32)
    m = r.astype(BF16)
    l = (r - m.astype(F32)).astype(BF16)
    return h, m, l


def _gla_kernel(q_ref, k_ref, v_ref, r_ref, misc_ref, wa2_ref, ba_ref, on_ref, o_ref, state_ref):
    c = pl.program_id(1)

    @pl.when(c == 0)
    def _():
        state_ref[...] = jnp.zeros_like(state_ref)

    row = lax.broadcasted_iota(I32, (CHUNK, CHUNK), 0)
    col = lax.broadcasted_iota(I32, (CHUNK, CHUNK), 1)
    lower = col <= row
    tril = jnp.where(lower, 1.0, 0.0).astype(BF16)
    a_low = misc_ref[0][:, 0:GLA_RANK].astype(BF16)

    for h in range(GLA_HEADS):
        ks = slice(h * GLA_DK, (h + 1) * GLA_DK)
        vs = slice(h * GLA_DV, (h + 1) * GLA_DV)
        z = _dot(a_low, wa2_ref[:, ks]) + ba_ref[:, ks]
        la = _log_sigmoid(z) * (1.0 / GLA_TAU)
        l_h, l_m, l_l = _split3(la)
        b = _dot(tril, l_h) + _dot(tril, l_m) + _dot(tril, l_l)
        eb = jnp.exp(b)
        ebi = jnp.exp(-b)
        b_last = b[CHUNK - 1:CHUNK, :]
        q = q_ref[:, ks].astype(F32) * (GLA_DK ** -0.5)
        k = k_ref[:, ks].astype(F32)
        v = v_ref[:, vs]
        q_fwd = (q * eb).astype(BF16)
        a_lo = _dot_nt(q_fwd, (k * ebi).astype(BF16))
        a_up = _dot_nt((q * ebi).astype(BF16), (k * eb).astype(BF16))
        scores = jnp.where(lower, a_lo, a_up).astype(BF16)
        st = state_ref[h]
        o = _dot(scores, v) + _dot_nt(q_fwd, st.astype(BF16))
        k_dec = (k * jnp.exp(b_last - b)).astype(BF16)
        v_t = v.astype(F32).T.astype(BF16)
        state_ref[h] = st * jnp.exp(b_last) + _dot(v_t, k_dec)
        o = _rms(o, on_ref[...])
        r = r_ref[:, vs].astype(F32)
        o_ref[:, vs] = (o * (r * jax.nn.sigmoid(r))).astype(o_ref.dtype)


def _gla(proj, misc, w_a2, b_a, out_norm, batch, seq):
    nc = seq // CHUNK
    t = batch * seq
    tok = lambda b, c: b * nc + c
    return pl.pallas_call(
        _gla_kernel,
        grid=(batch, nc),
        in_specs=[pl.BlockSpec((CHUNK, 1024), lambda b, c: (tok(b, c), 0)),
                  pl.BlockSpec((CHUNK, 1024), lambda b, c: (tok(b, c), 1)),
                  pl.BlockSpec((CHUNK, 2048), lambda b, c: (tok(b, c), 1)),
                  pl.BlockSpec((CHUNK, 2048), lambda b, c: (tok(b, c), 2)),
                  pl.BlockSpec((1, CHUNK, 128), lambda b, c: (b, c + PAD_FRONT // CHUNK, 0)),
                  pl.BlockSpec((GLA_RANK, GLA_HEADS * GLA_DK), lambda b, c: (0, 0)),
                  pl.BlockSpec((1, GLA_HEADS * GLA_DK), lambda b, c: (0, 0)),
                  pl.BlockSpec((1, GLA_DV), lambda b, c: (0, 0))],
        out_specs=pl.BlockSpec((CHUNK, GLA_HEADS * GLA_DV), lambda b, c: (tok(b, c), 0)),
        out_shape=jax.ShapeDtypeStruct((t, GLA_HEADS * GLA_DV), BF16),
        scratch_shapes=[pltpu.VMEM((GLA_HEADS, GLA_DV, GLA_DK), F32)],
        compiler_params=_cparams(("parallel", "arbitrary")),
        name="gla",
    )(proj, proj, proj, proj, misc, w_a2, b_a, out_norm)


def _t5_bucket(rel):
    half = REL_BUCKETS // 2
    max_exact = half // 2
    ret = jnp.where(rel > 0, half, 0)
    n = jnp.abs(rel)
    nf = jnp.maximum(n, 1).astype(jnp.float32)
    large = max_exact + (jnp.log(nf / max_exact) / math.log(REL_MAX_DIST / max_exact)
                         * (half - max_exact)).astype(jnp.int32)
    large = jnp.minimum(large, half - 1)
    return ret + jnp.where(n < max_exact, n, large)


def _bias_band_kernel(bucket_ref, rb_ref, o_ref):
    far = REL_BUCKETS // 2 - 1
    bucket = bucket_ref[...]
    for h in range(DSA_HEADS):
        acc = jnp.zeros(bucket.shape, F32)
        for b in range(REL_BUCKETS):
            acc = jnp.where(bucket == b, rb_ref[b, h], acc)
        o_ref[h] = acc - rb_ref[far, h]


def _bias_band(rel_bias):
    t = jnp.arange(QBLK, dtype=jnp.int32)[:, None]
    j = jnp.arange(2 * QBLK, dtype=jnp.int32)[None, :]
    bucket = _t5_bucket(j - QBLK - t).astype(jnp.int32)
    return pl.pallas_call(
        _bias_band_kernel,
        in_specs=[pl.BlockSpec(memory_space=pltpu.VMEM), pl.BlockSpec(memory_space=pltpu.SMEM)],
        out_specs=pl.BlockSpec(memory_space=pltpu.VMEM),
        out_shape=jax.ShapeDtypeStruct((DSA_HEADS, QBLK, 2 * QBLK), F32),
        name="bias_band",
    )(bucket, rel_bias)


def _sortable(x):
    i = pltpu.bitcast(x, I32)
    return jnp.where(i < 0, i ^ jnp.int32(0x7FFFFFFF), i)


def _dsa_kernel(top_k, dq_ref, iq_ref, misc_ref, kidx_ref, clat_ref, wuk_ref, wuv_ref, band_ref,
                o_ref, keys_ref, wb_ref, qlat_ref, madd_ref, m_ref, l_ref, acc_ref):
    qb = pl.program_id(1)
    start = qb * QBLK

    w_scale = IDX_HEADS ** -0.5 * IDX_DIM ** -0.5
    wq = misc_ref[0][:, GLA_RANK:GLA_RANK + IDX_HEADS] * w_scale
    for h in range(IDX_HEADS):
        wb_ref[h] = jnp.broadcast_to(wq[:, h:h + 1], (QBLK, 128))

    row = lax.broadcasted_iota(I32, (QBLK, KCH), 0)
    lane = lax.broadcasted_iota(I32, (QBLK, KCH), 1)
    p_lim = start + (row // CHUNK + 1) * CHUNK + PAD_FRONT

    n_chunks = (start + 2 * QBLK + KCH - 1) // KCH

    def idx_body(c, carry):
        off = pl.multiple_of(c * KCH, KCH)
        kc = kidx_ref[0, pl.ds(off, KCH), :]
        acc = jnp.zeros((QBLK, KCH), F32)
        for h in range(IDX_HEADS):
            d = _dot_nt(iq_ref[:, h * IDX_DIM:(h + 1) * IDX_DIM], kc)
            wbh = wb_ref[h]
            acc = acc + jnp.concatenate([wbh] * (KCH // 128), axis=1) * jnp.maximum(d, 0.0)
        p = lane + off
        valid = jnp.logical_and(p >= PAD_FRONT, p < p_lim)
        keys_ref[:, pl.ds(off, KCH)] = jnp.where(valid, _sortable(acc), INT_MIN)
        return carry

    lax.fori_loop(0, n_chunks, idx_body, 0)

    def count_ge(cand):
        def body(c, cnt):
            off = pl.multiple_of(c * KCH, KCH)
            kk = keys_ref[:, pl.ds(off, KCH)]
            for s in range(KCH // 128):
                cnt = cnt + jnp.where(kk[:, s * 128:(s + 1) * 128] >= cand, 1, 0)
            return cnt
        cnt = lax.fori_loop(0, n_chunks, body, jnp.zeros((QBLK, 128), I32))
        return jnp.broadcast_to(jnp.sum(cnt, axis=1, keepdims=True), (QBLK, 128))

    zero = jnp.zeros((QBLK, 128), I32)
    prefix = jnp.where(count_ge(zero) >= top_k, zero, jnp.full((QBLK, 128), INT_MIN, I32))

    def bit_body(i, prefix):
        cand = prefix + (jnp.int32(1) << (30 - i))
        return jnp.where(count_ge(cand) >= top_k, cand, prefix)

    thr = lax.fori_loop(0, 31, bit_body, prefix)
    thr = jnp.maximum(thr, INT_MIN + 1)

    for h in range(DSA_HEADS):
        ql = _dot(dq_ref[:, h * DSA_DH:(h + 1) * DSA_DH], wuk_ref[h]) * (DSA_DH ** -0.5)
        qlat_ref[h] = ql.astype(BF16)

    m_ref[...] = jnp.full(m_ref.shape, NEG, F32)
    l_ref[...] = jnp.zeros(l_ref.shape, F32)
    acc_ref[...] = jnp.zeros(acc_ref.shape, F32)

    def attend(off, width, p_hi, band_lo):
        kk = keys_ref[:, pl.ds(off, width)]
        sel = kk >= jnp.concatenate([thr] * (width // 128), axis=1)
        if p_hi is not None:
            pp = lax.broadcasted_iota(I32, (QBLK, width), 1) + off
            sel = jnp.logical_and(sel, pp < p_hi)
        madd_ref[:, 0:width] = jnp.where(sel, 0.0, NEG)
        cc = clat_ref[0, pl.ds(off, width), :]

        def head_body(h, carry):
            s = _dot_nt(qlat_ref[h], cc) + madd_ref[:, 0:width]
            if band_lo is not None:
                s = s + band_ref[h, :, band_lo:band_lo + width]
            m_old = m_ref[h]
            m_new = jnp.maximum(m_old, jnp.broadcast_to(jnp.max(s, axis=1, keepdims=True), (QBLK, 128)))
            alpha = jnp.exp(m_old - m_new)
            p = jnp.exp(s - jnp.concatenate([m_new] * (width // 128), axis=1))
            l_ref[h] = alpha * l_ref[h] + jnp.broadcast_to(jnp.sum(p, axis=1, keepdims=True), (QBLK, 128))
            m_ref[h] = m_new
            acc_ref[h] = (jnp.concatenate([alpha] * (DSA_LATENT // 128), axis=1) * acc_ref[h]
                          + _dot(p.astype(BF16), cc))
            return carry

        lax.fori_loop(0, DSA_HEADS, head_body, 0)

    n_full = start // KCH

    def far_body(c, carry):
        attend(pl.multiple_of(c * KCH, KCH), KCH, None, None)
        return carry

    lax.fori_loop(0, n_full, far_body, 0)

    @pl.when(start % KCH != 0)
    def _():
        attend(pl.multiple_of(n_full * KCH, KCH), KCH, start, None)

    attend(pl.multiple_of(start, QBLK), 2 * QBLK, None, 0)

    for h in range(DSA_HEADS):
        inv = 1.0 / l_ref[h]
        o_lat = acc_ref[h] * jnp.concatenate([inv] * (DSA_LATENT // 128), axis=1)
        o_ref[:, h * DSA_DV:(h + 1) * DSA_DV] = _dot(o_lat.astype(BF16), wuv_ref[h]).astype(o_ref.dtype)


def _dsa(proj, misc, kidx, clat, w_uk, w_uv, band, batch, seq):
    nqb = seq // QBLK
    t = batch * seq
    npad = clat.shape[1]
    top_k = min(INDEX_TOPK, seq // 4)
    tokb = lambda b, i: b * nqb + i
    const3 = lambda b, i: (0, 0, 0)
    return pl.pallas_call(
        functools.partial(_dsa_kernel, top_k),
        grid=(batch, nqb),
        in_specs=[pl.BlockSpec((QBLK, 2048), lambda b, i: (tokb(b, i), 3)),
                  pl.BlockSpec((QBLK, 2048), lambda b, i: (tokb(b, i), 4)),
                  pl.BlockSpec((1, QBLK, 128), lambda b, i: (b, i + PAD_FRONT // QBLK, 0)),
                  pl.BlockSpec((1, npad, IDX_DIM), lambda b, i: (b, 0, 0)),
                  pl.BlockSpec((1, npad, DSA_LATENT), lambda b, i: (b, 0, 0)),
                  pl.BlockSpec((DSA_HEADS, DSA_DH, DSA_LATENT), const3),
                  pl.BlockSpec((DSA_HEADS, DSA_LATENT, DSA_DV), const3),
                  pl.BlockSpec((DSA_HEADS, QBLK, 2 * QBLK), const3)],
        out_specs=pl.BlockSpec((QBLK, DSA_HEADS * DSA_DV), lambda b, i: (tokb(b, i), 0)),
        out_shape=jax.ShapeDtypeStruct((t, DSA_HEADS * DSA_DV), BF16),
        scratch_shapes=[pltpu.VMEM((QBLK, npad), I32),
                        pltpu.VMEM((IDX_HEADS, QBLK, 128), F32),
                        pltpu.VMEM((DSA_HEADS, QBLK, DSA_LATENT), BF16),
                        pltpu.VMEM((QBLK, KCH), F32),
                        pltpu.VMEM((DSA_HEADS, QBLK, 128), F32),
                        pltpu.VMEM((DSA_HEADS, QBLK, 128), F32),
                        pltpu.VMEM((DSA_HEADS, QBLK, DSA_LATENT), F32)],
        compiler_params=_cparams(("parallel", "arbitrary")),
        name="dsa",
    )(proj, proj, misc, kidx, clat, w_uk, w_uv, band)


def _merge_kernel(yg_ref, yd_ref, gg_ref, gd_ref, wg_ref, wd_ref, o_ref):
    a = _dot(yg_ref[...], wg_ref[...])
    b = _dot(yd_ref[...], wd_ref[...])
    gg = jax.nn.sigmoid(gg_ref[...].astype(F32))
    gd = jax.nn.sigmoid(gd_ref[...].astype(F32))
    o_ref[...] = (gg * a + gd * b).astype(o_ref.dtype)


def _merge(y_gla, y_dsa, proj, w_g, w_d, tm, tn):
    t, d = y_gla.shape
    n = w_g.shape[1]
    ncb = n // tn
    return pl.pallas_call(
        _merge_kernel,
        grid=(t // tm, ncb),
        in_specs=[pl.BlockSpec((tm, d), lambda i, j: (i, 0)),
                  pl.BlockSpec((tm, d), lambda i, j: (i, 0)),
                  pl.BlockSpec((tm, tn), lambda i, j: (i, 5 * ncb + j)),
                  pl.BlockSpec((tm, tn), lambda i, j: (i, 6 * ncb + j)),
                  pl.BlockSpec((d, tn), lambda i, j: (0, j)),
                  pl.BlockSpec((d, tn), lambda i, j: (0, j))],
        out_specs=pl.BlockSpec((tm, tn), lambda i, j: (i, j)),
        out_shape=jax.ShapeDtypeStruct((t, n), BF16),
        compiler_params=_cparams(("parallel", "arbitrary")),
        name="merge",
    )(y_gla, y_dsa, proj, proj, w_g, w_d)


def _proj_norm_res_kernel(y_ref, w_ref, g_ref, h_ref, o_ref):
    y = _dot(y_ref[...], w_ref[...])
    o_ref[...] = h_ref[...] + _rms(y, g_ref[...])


def _proj_norm_res(y, w, gain, h, tm):
    t, d = y.shape
    n = w.shape[1]
    return pl.pallas_call(
        _proj_norm_res_kernel,
        grid=(t // tm,),
        in_specs=[pl.BlockSpec((tm, d), lambda i: (i, 0)),
                  pl.BlockSpec((d, n), lambda i: (0, 0)),
                  pl.BlockSpec((1, n), lambda i: (0, 0)),
                  pl.BlockSpec((tm, n), lambda i: (i, 0))],
        out_specs=pl.BlockSpec((tm, n), lambda i: (i, 0)),
        out_shape=jax.ShapeDtypeStruct((t, n), F32),
        compiler_params=_cparams(("parallel",)),
        name="proj_norm_res",
    )(y, w, gain, h)


def _xattn_kernel(q_ref, k_ref, v_ref, o_ref):
    for h in range(XA_HEADS):
        hs = slice(h * XA_DH, (h + 1) * XA_DH)
        s = _dot_nt(q_ref[:, hs], k_ref[0][:, hs]) * (XA_DH ** -0.5)
        m = jnp.max(s, axis=-1, keepdims=True)
        p = jnp.exp(s - m)
        p = p / jnp.sum(p, axis=-1, keepdims=True)
        o_ref[:, hs] = _dot(p.astype(BF16), v_ref[0][:, hs]).astype(o_ref.dtype)


def _xattn(q, kv, batch, seq, tm):
    t, d = q.shape
    n_mem = kv.shape[1]
    nb = seq // tm
    return pl.pallas_call(
        _xattn_kernel,
        grid=(batch, nb),
        in_specs=[pl.BlockSpec((tm, d), lambda b, i: (b * nb + i, 0)),
                  pl.BlockSpec((1, n_mem, d), lambda b, i: (b, 0, 0)),
                  pl.BlockSpec((1, n_mem, d), lambda b, i: (b, 0, 1))],
        out_specs=pl.BlockSpec((tm, d), lambda b, i: (b * nb + i, 0)),
        out_shape=jax.ShapeDtypeStruct((t, d), BF16),
        compiler_params=_cparams(("parallel", "parallel")),
        name="xattn",
    )(q, kv, kv)


def _ffn_kernel(h_ref, g_ref, wg_ref, wu_ref, wd_ref, pg_ref, o_ref, u_ref, acc_ref):
    f = pl.program_id(1)

    @pl.when(f == 0)
    def _():
        u_ref[...] = _rms(h_ref[...], g_ref[...]).astype(BF16)
        acc_ref[...] = jnp.zeros_like(acc_ref)

    u = u_ref[...]
    a = _dot(u, wg_ref[...])
    b = _dot(u, wu_ref[...])
    act = (a * jax.nn.sigmoid(a) * b).astype(BF16)
    acc_ref[...] += _dot(act, wd_ref[...])

    @pl.when(f == pl.num_programs(1) - 1)
    def _():
        o_ref[...] = h_ref[...] + _rms(acc_ref[...], pg_ref[...])


def _ffn(h, pre_gain, w_gate, w_up, w_down, post_gain, tm, tf):
    t, d = h.shape
    ff = w_gate.shape[1]
    return pl.pallas_call(
        _ffn_kernel,
        grid=(t // tm, ff // tf),
        in_specs=[pl.BlockSpec((tm, d), lambda i, f: (i, 0)),
                  pl.BlockSpec((1, d), lambda i, f: (0, 0)),
                  pl.BlockSpec((d, tf), lambda i, f: (0, f)),
                  pl.BlockSpec((d, tf), lambda i, f: (0, f)),
                  pl.BlockSpec((tf, d), lambda i, f: (f, 0)),
                  pl.BlockSpec((1, d), lambda i, f: (0, 0))],
        out_specs=pl.BlockSpec((tm, d), lambda i, f: (i, 0)),
        out_shape=jax.ShapeDtypeStruct((t, d), F32),
        scratch_shapes=[pltpu.VMEM((tm, d), BF16), pltpu.VMEM((tm, d), F32)],
        compiler_params=_cparams(("parallel", "arbitrary")),
        name="ffn",
    )(h, pre_gain, w_gate, w_up, w_down, post_gain)


def _row(v):
    return v.reshape(1, -1).astype(F32)


def _layer(h, mem, w_in, gla_w_a2, gla_b_a, gla_out_norm, dsa_w_uk, dsa_w_uv, dsa_latent_norm,
           idx_k_norm_w, idx_k_norm_b, band, w_gla_branch, w_dsa_branch, w_mix_out,
           mix_pre_norm, mix_post_norm, xa_pre_norm, xa_post_norm, xa_mem_norm,
           w_xa_q, w_xa_kv, w_xa_o, ffn_pre_norm, ffn_post_norm, w_ffn_gate, w_ffn_up, w_ffn_down,
           batch, seq):
    d = D_MODEL
    t = batch * seq
    cols = lambda a, b: w_in[:, a:b]
    w_main = jnp.concatenate(
        [cols(_O_GQ, _O_GA), cols(_O_DQ, _O_DC), cols(_O_IQ, _O_IK), cols(_O_GG, _O_END)], axis=1).astype(BF16)
    w_small = jnp.concatenate(
        [cols(_O_DC, _O_IQ), cols(_O_IK, _O_IW), cols(_O_GA, _O_DQ), cols(_O_IW, _O_GG),
         jnp.zeros((d, SMALL_COLS - 416), w_in.dtype)], axis=1).astype(BF16)

    pre = _row(mix_pre_norm)
    proj = _norm_matmul(h, pre, w_main, BF16, 1024, 512)
    clat, kidx, misc = _small_proj(h, pre, w_small, _row(dsa_latent_norm), _row(idx_k_norm_w),
                                   _row(idx_k_norm_b), batch, seq)

    y_gla = _gla(proj, misc, gla_w_a2.astype(BF16), _row(gla_b_a), _row(gla_out_norm), batch, seq)
    y_dsa = _dsa(proj, misc, kidx, clat, dsa_w_uk.astype(BF16), dsa_w_uv.astype(BF16), band, batch, seq)

    merged = _merge(y_gla, y_dsa, proj, w_gla_branch.astype(BF16), w_dsa_branch.astype(BF16), 1024, 512)
    h = _proj_norm_res(merged, w_mix_out.astype(BF16), _row(mix_post_norm), h, 512)

    n_mem = mem.shape[1]
    kv = _norm_matmul(mem.reshape(batch * n_mem, d), _row(xa_mem_norm), w_xa_kv.astype(BF16), BF16,
                      batch * n_mem, 512).reshape(batch, n_mem, 2 * d)
    q = _norm_matmul(h, _row(xa_pre_norm), w_xa_q.astype(BF16), BF16, 1024, 512)
    o = _xattn(q, kv, batch, seq, 512)
    h = _proj_norm_res(o, w_xa_o.astype(BF16), _row(xa_post_norm), h, 512)

    h = _ffn(h, _row(ffn_pre_norm), w_ffn_gate.astype(BF16), w_ffn_up.astype(BF16),
             w_ffn_down.astype(BF16), _row(ffn_post_norm), 512, 512)
    return h


def kernel(x, mem, w_in, gla_w_a2, gla_b_a, gla_out_norm, dsa_w_uk, dsa_w_uv, dsa_latent_norm,
           idx_k_norm_w, idx_k_norm_b, rel_bias, w_gla_branch, w_dsa_branch, w_mix_out,
           mix_pre_norm, mix_post_norm, xa_pre_norm, xa_post_norm, xa_mem_norm,
           w_xa_q, w_xa_kv, w_xa_o, ffn_pre_norm, ffn_post_norm, w_ffn_gate, w_ffn_up, w_ffn_down):
    batch, seq, d = x.shape
    depth = w_in.shape[0]
    band = _bias_band(rel_bias.astype(F32))
    h = x.reshape(batch * seq, d)
    for l in range(depth):
        h = _layer(h, mem, w_in[l], gla_w_a2[l], gla_b_a[l], gla_out_norm[l], dsa_w_uk[l], dsa_w_uv[l],
                   dsa_latent_norm[l], idx_k_norm_w[l], idx_k_norm_b[l], band, w_gla_branch[l],
                   w_dsa_branch[l], w_mix_out[l], mix_pre_norm[l], mix_post_norm[l], xa_pre_norm[l],
                   xa_post_norm[l], xa_mem_norm[l], w_xa_q[l], w_xa_kv[l], w_xa_o[l], ffn_pre_norm[l],
                   ffn_post_norm[l], w_ffn_gate[l], w_ffn_up[l], w_ffn_down[l], batch, seq)
    return h.reshape(batch, seq, d)
```

```python
import functools
import math

import jax
import jax.numpy as jnp
import numpy as np
from jax import lax
from jax.experimental import pallas as pl
from jax.experimental.pallas import tpu as pltpu

F32 = jnp.float32
BF16 = jnp.bfloat16
I32 = jnp.int32

D_MODEL = 2048
CHUNK = 64
EPS = 1e-6

GLA_HEADS = 4
GLA_DK = 256
GLA_DV = 512
GLA_RANK = 16
GLA_TAU = 16.0

DSA_HEADS = 16
DSA_DH = 128
DSA_DV = 128
DSA_LATENT = 256
IDX_HEADS = 16
IDX_DIM = 128
INDEX_TOPK = 256

REL_BUCKETS = 32
REL_MAX_DIST = 128

XA_HEADS = 4
XA_DH = 512

_SPLITS = (1024, 1024, 2048, 2048, 16, 2048, 256, 2048, 128, 16, 2048, 2048)
_OFFS = tuple(int(v) for v in np.cumsum((0,) + _SPLITS))
(_O_GQ, _O_GK, _O_GV, _O_GR, _O_GA, _O_DQ, _O_DC, _O_IQ, _O_IK, _O_IW, _O_GG, _O_GD, _O_END) = _OFFS

MAIN_COLS = 14336
SMALL_COLS = 512

QBLK = 128
KCH = 512
PAD_FRONT = KCH
INT_MIN = -2 ** 31
NEG = -1e30
LOG2E = 1.4426950408889634

VMEM_LIMIT = 56 * 1024 * 1024


def _cparams(sem):
    return pltpu.CompilerParams(dimension_semantics=sem, vmem_limit_bytes=VMEM_LIMIT)


def _rms(x, gain):
    ms = jnp.mean(x * x, axis=-1, keepdims=True)
    return x * lax.rsqrt(ms + EPS) * gain


def _dot(a, b):
    return jnp.dot(a, b, preferred_element_type=F32)


def _dot_nt(a, b):
    return lax.dot_general(a, b, (((1,), (1,)), ((), ())), preferred_element_type=F32)


def _norm_matmul_kernel(x_ref, g_ref, w_ref, o_ref, u_ref):
    @pl.when(pl.program_id(1) == 0)
    def _():
        u_ref[...] = _rms(x_ref[...], g_ref[...]).astype(BF16)

    o_ref[...] = _dot(u_ref[...], w_ref[...]).astype(o_ref.dtype)


def _norm_matmul(x, gain, w, out_dtype, tm, tn):
    t, d = x.shape
    n = w.shape[1]
    return pl.pallas_call(
        _norm_matmul_kernel,
        grid=(t // tm, n // tn),
        in_specs=[pl.BlockSpec((tm, d), lambda i, j: (i, 0)),
                  pl.BlockSpec((1, d), lambda i, j: (0, 0)),
                  pl.BlockSpec((d, tn), lambda i, j: (0, j))],
        out_specs=pl.BlockSpec((tm, tn), lambda i, j: (i, j)),
        out_shape=jax.ShapeDtypeStruct((t, n), out_dtype),
        scratch_shapes=[pltpu.VMEM((tm, d), BF16)],
        compiler_params=_cparams(("parallel", "arbitrary")),
        name="norm_matmul",
    )(x, gain, w)


def _small_proj(x, gain, w_small, lat_g, ik_w, ik_b, batch, seq):
    tm = PAD_FRONT
    assert seq % tm == 0
    nblk = seq // tm
    npad = PAD_FRONT + seq
    grid = (batch, 1 + nblk)

    def x_map(b, i):
        return (b * nblk + jnp.maximum(i - 1, 0), 0)

    def kernel(x_ref, g_ref, w_ref, lat_g_ref, ik_w_ref, ik_b_ref, c_ref, k_ref, misc_ref):
        is_pad = pl.program_id(1) == 0

        @pl.when(is_pad)
        def _():
            c_ref[...] = jnp.zeros_like(c_ref)
            k_ref[...] = jnp.zeros_like(k_ref)
            misc_ref[...] = jnp.zeros_like(misc_ref)

        @pl.when(jnp.logical_not(is_pad))
        def _():
            u = _rms(x_ref[...], g_ref[...]).astype(BF16)
            p = _dot(u, w_ref[...])
            c_ref[0] = _rms(p[:, 0:256], lat_g_ref[...]).astype(BF16)
            ik = p[:, 256:384]
            mu = jnp.mean(ik, axis=-1, keepdims=True)
            xc = ik - mu
            var = jnp.mean(xc * xc, axis=-1, keepdims=True)
            k_ref[0] = (xc * lax.rsqrt(var + EPS) * ik_w_ref[...] + ik_b_ref[...]).astype(BF16)
            misc_ref[0] = p[:, 384:512]

    d = x.shape[1]
    const = lambda b, i: (0, 0)
    return pl.pallas_call(
        kernel,
        grid=grid,
        in_specs=[pl.BlockSpec((tm, d), x_map),
                  pl.BlockSpec((1, d), const),
                  pl.BlockSpec((d, SMALL_COLS), const),
                  pl.BlockSpec((1, DSA_LATENT), const),
                  pl.BlockSpec((1, IDX_DIM), const),
                  pl.BlockSpec((1, IDX_DIM), const)],
        out_specs=[pl.BlockSpec((1, tm, DSA_LATENT), lambda b, i: (b, i, 0)),
                   pl.BlockSpec((1, tm, IDX_DIM), lambda b, i: (b, i, 0)),
                   pl.BlockSpec((1, tm, 128), lambda b, i: (b, i, 0))],
        out_shape=[jax.ShapeDtypeStruct((batch, npad, DSA_LATENT), BF16),
                   jax.ShapeDtypeStruct((batch, npad, IDX_DIM), BF16),
                   jax.ShapeDtypeStruct((batch, npad, 128), F32)],
        compiler_params=_cparams(("parallel", "arbitrary")),
        name="small_proj",
    )(x, gain, w_small, lat_g, ik_w, ik_b)


def _log_sigmoid(z):
    return jnp.minimum(z, 0.0) - jnp.log1p(jnp.exp(-jnp.abs(z)))


def _split3(x):
    h = x.astype(BF16)
    r = x - h.astype(F32)
    m = r.astype(BF16)
    l = (r - m.astype(F32)).astype(BF16)
    return h, m, l


def _gla_kernel(q_ref, k_ref, v_ref, r_ref, misc_ref, wa2_ref, ba_ref, on_ref, o_ref, state_ref):
    c = pl.program_id(1)

    @pl.when(c == 0)
    def _():
        state_ref[...] = jnp.zeros_like(state_ref)

    row = lax.broadcasted_iota(I32, (CHUNK, CHUNK), 0)
    col = lax.broadcasted_iota(I32, (CHUNK, CHUNK), 1)
    lower = col <= row
    tril = jnp.where(lower, 1.0, 0.0).astype(BF16)
    a_low = misc_ref[0][:, 0:GLA_RANK].astype(BF16)

    for h in range(GLA_HEADS):
        ks = slice(h * GLA_DK, (h + 1) * GLA_DK)
        vs = slice(h * GLA_DV, (h + 1) * GLA_DV)
        z = _dot(a_low, wa2_ref[:, ks]) + ba_ref[:, ks]
        la = _log_sigmoid(z) * (1.0 / GLA_TAU)
        l_h, l_m, l_l = _split3(la)
        b = _dot(tril, l_h) + _dot(tril, l_m) + _dot(tril, l_l)
        eb = jnp.exp(b)
        ebi = jnp.exp(-b)
        b_last = b[CHUNK - 1:CHUNK, :]
        q = q_ref[:, ks].astype(F32) * (GLA_DK ** -0.5)
        k = k_ref[:, ks].astype(F32)
        v = v_ref[:, vs]
        q_fwd = (q * eb).astype(BF16)
        a_lo = _dot_nt(q_fwd, (k * ebi).astype(BF16))
        a_up = _dot_nt((q * ebi).astype(BF16), (k * eb).astype(BF16))
        scores = jnp.where(lower, a_lo, a_up).astype(BF16)
        st = state_ref[h]
        o = _dot(scores, v) + _dot_nt(q_fwd, st.astype(BF16))
        k_dec = (k * jnp.exp(b_last - b)).astype(BF16)
        v_t = v.astype(F32).T.astype(BF16)
        state_ref[h] = st * jnp.exp(b_last) + _dot(v_t, k_dec)
        o = _rms(o, on_ref[...])
        r = r_ref[:, vs].astype(F32)
        o_ref[:, vs] = (o * (r * jax.nn.sigmoid(r))).astype(o_ref.dtype)


def _gla(proj, misc, w_a2, b_a, out_norm, batch, seq):
    nc = seq // CHUNK
    t = batch * seq
    tok = lambda b, c: b * nc + c
    return pl.pallas_call(
        _gla_kernel,
        grid=(batch, nc),
        in_specs=[pl.BlockSpec((CHUNK, 1024), lambda b, c: (tok(b, c), 0)),
                  pl.BlockSpec((CHUNK, 1024), lambda b, c: (tok(b, c), 1)),
                  pl.BlockSpec((CHUNK, 2048), lambda b, c: (tok(b, c), 1)),
                  pl.BlockSpec((CHUNK, 2048), lambda b, c: (tok(b, c), 2)),
                  pl.BlockSpec((1, CHUNK, 128), lambda b, c: (b, c + PAD_FRONT // CHUNK, 0)),
                  pl.BlockSpec((GLA_RANK, GLA_HEADS * GLA_DK), lambda b, c: (0, 0)),
                  pl.BlockSpec((1, GLA_HEADS * GLA_DK), lambda b, c: (0, 0)),
                  pl.BlockSpec((1, GLA_DV), lambda b, c: (0, 0))],
        out_specs=pl.BlockSpec((CHUNK, GLA_HEADS * GLA_DV), lambda b, c: (tok(b, c), 0)),
        out_shape=jax.ShapeDtypeStruct((t, GLA_HEADS * GLA_DV), BF16),
        scratch_shapes=[pltpu.VMEM((GLA_HEADS, GLA_DV, GLA_DK), F32)],
        compiler_params=_cparams(("parallel", "arbitrary")),
        name="gla",
    )(proj, proj, proj, proj, misc, w_a2, b_a, out_norm)


def _t5_bucket(rel):
    half = REL_BUCKETS // 2
    max_exact = half // 2
    ret = jnp.where(rel > 0, half, 0)
    n = jnp.abs(rel)
    nf = jnp.maximum(n, 1).astype(jnp.float32)
    large = max_exact + (jnp.log(nf / max_exact) / math.log(REL_MAX_DIST / max_exact)
                         * (half - max_exact)).astype(jnp.int32)
    large = jnp.minimum(large, half - 1)
    return ret + jnp.where(n < max_exact, n, large)


def _bias_band_kernel(bucket_ref, rb_ref, o_ref):
    far = REL_BUCKETS // 2 - 1
    bucket = bucket_ref[...]
    for h in range(DSA_HEADS):
        acc = jnp.zeros(bucket.shape, F32)
        for b in range(REL_BUCKETS):
            acc = jnp.where(bucket == b, rb_ref[b, h], acc)
        o_ref[h] = (acc - rb_ref[far, h]) * LOG2E


def _bias_band(rel_bias):
    t = jnp.arange(QBLK, dtype=jnp.int32)[:, None]
    j = jnp.arange(2 * QBLK, dtype=jnp.int32)[None, :]
    bucket = _t5_bucket(j - QBLK - t).astype(jnp.int32)
    return pl.pallas_call(
        _bias_band_kernel,
        in_specs=[pl.BlockSpec(memory_space=pltpu.VMEM), pl.BlockSpec(memory_space=pltpu.SMEM)],
        out_specs=pl.BlockSpec(memory_space=pltpu.VMEM),
        out_shape=jax.ShapeDtypeStruct((DSA_HEADS, QBLK, 2 * QBLK), F32),
        name="bias_band",
    )(bucket, rel_bias)


def _sortable(x):
    i = pltpu.bitcast(x, I32)
    return jnp.where(i < 0, i ^ jnp.int32(0x7FFFFFFF), i)


def _dsa_kernel(top_k, dq_ref, iq_ref, misc_ref, kidx_ref, clat_ref, wuk_ref, wuv_ref, band_ref,
                o_ref, keys_ref, iqs_ref, wb_ref, qlat_ref, madd_ref, s_ref, s2_ref, p_ref, alpha_ref,
                m_ref, l_ref, acc_ref):
    qb = pl.program_id(1)
    start = qb * QBLK
    hrows = lambda h: slice(h * QBLK, (h + 1) * QBLK)

    w_scale = IDX_HEADS ** -0.5 * IDX_DIM ** -0.5
    wq = misc_ref[0][:, GLA_RANK:GLA_RANK + IDX_HEADS] * w_scale
    for h in range(IDX_HEADS):
        wb_ref[hrows(h), :] = jnp.broadcast_to(wq[:, h:h + 1], (QBLK, 128))
        iqs_ref[hrows(h), :] = iq_ref[:, h * IDX_DIM:(h + 1) * IDX_DIM]

    row = lax.broadcasted_iota(I32, (QBLK, KCH), 0)
    lane = lax.broadcasted_iota(I32, (QBLK, KCH), 1)
    p_lim = start + (row // CHUNK + 1) * CHUNK + PAD_FRONT

    n_chunks = (start + PAD_FRONT + QBLK + KCH - 1) // KCH
    keys_ref[:, 0:KCH] = jnp.full((QBLK, KCH), INT_MIN, I32)

    def idx_body(c, carry):
        off = pl.multiple_of(c * KCH, KCH)
        kc = kidx_ref[0, pl.ds(off, KCH), :]
        s_ref[...] = _dot_nt(iqs_ref[...], kc)
        acc = jnp.zeros((QBLK, KCH), F32)
        for h in range(IDX_HEADS):
            wbh = wb_ref[hrows(h), :]
            acc = acc + jnp.concatenate([wbh] * (KCH // 128), axis=1) * jnp.maximum(s_ref[hrows(h), :], 0.0)
        keys_ref[:, pl.ds(off, KCH)] = jnp.where(lane + off < p_lim, _sortable(acc), INT_MIN)
        return carry

    lax.fori_loop(1, n_chunks, idx_body, 0)

    def count_ge(cand):
        def body(c, cnt):
            off = pl.multiple_of(c * KCH, KCH)
            kk = keys_ref[:, pl.ds(off, KCH)]
            for s in range(KCH // 128):
                cnt = cnt + jnp.where(kk[:, s * 128:(s + 1) * 128] >= cand, 1, 0)
            return cnt
        cnt = lax.fori_loop(1, n_chunks, body, jnp.zeros((QBLK, 128), I32))
        return jnp.broadcast_to(jnp.sum(cnt, axis=1, keepdims=True), (QBLK, 128))

    zero = jnp.zeros((QBLK, 128), I32)
    prefix = jnp.where(count_ge(zero) >= top_k, zero, jnp.full((QBLK, 128), INT_MIN, I32))

    def bit_body(i, prefix):
        cand = prefix + (jnp.int32(1) << (30 - i))
        return jnp.where(count_ge(cand) >= top_k, cand, prefix)

    thr = lax.fori_loop(0, 31, bit_body, prefix)
    thr = jnp.maximum(thr, INT_MIN + 1)

    for h in range(DSA_HEADS):
        ql = _dot(dq_ref[:, h * DSA_DH:(h + 1) * DSA_DH], wuk_ref[h]) * (LOG2E * DSA_DH ** -0.5)
        qlat_ref[hrows(h), :] = ql.astype(BF16)

    m_ref[...] = jnp.full(m_ref.shape, NEG, F32)
    l_ref[...] = jnp.zeros(l_ref.shape, F32)
    acc_ref[...] = jnp.zeros(acc_ref.shape, F32)

    end = start + PAD_FRONT + QBLK
    n_att = (start + QBLK + KCH - 1) // KCH

    def chunk_off(j):
        return pl.multiple_of(end - KCH * (j + 1), 128)

    def scores(j, dst_ref):
        cc = clat_ref[0, pl.ds(chunk_off(j), KCH), :]
        dst_ref[...] = _dot_nt(qlat_ref[...], cc)

    def softmax_update(j, src_ref, near):
        off = chunk_off(j)
        kk = keys_ref[:, pl.ds(off, KCH)]
        madd_ref[...] = jnp.where(kk >= jnp.concatenate([thr] * (KCH // 128), axis=1), 0.0, NEG)
        cc = clat_ref[0, pl.ds(off, KCH), :]
        for h in range(DSA_HEADS):
            s = src_ref[hrows(h), :] + madd_ref[...]
            if near:
                s = jnp.concatenate([s[:, :KCH - 2 * QBLK], s[:, KCH - 2 * QBLK:] + band_ref[h]], axis=1)
            m_old = m_ref[hrows(h), :]
            m_new = jnp.maximum(m_old, jnp.broadcast_to(jnp.max(s, axis=1, keepdims=True), (QBLK, 128)))
            alpha = jnp.exp2(m_old - m_new)
            p = jnp.exp2(s - jnp.concatenate([m_new] * (KCH // 128), axis=1))
            l_ref[hrows(h), :] = (alpha * l_ref[hrows(h), :]
                                  + jnp.broadcast_to(jnp.sum(p, axis=1, keepdims=True), (QBLK, 128)))
            m_ref[hrows(h), :] = m_new
            alpha_ref[hrows(h), :] = alpha
            p_ref[hrows(h), :] = p.astype(BF16)
        al = alpha_ref[...]
        acc_ref[...] = (jnp.concatenate([al] * (DSA_LATENT // 128), axis=1) * acc_ref[...]
                        + _dot(p_ref[...], cc))

    def step(j, cur_ref, nxt_ref, near):
        scores(jnp.minimum(j + 1, n_att - 1), nxt_ref)
        softmax_update(j, cur_ref, near)

    scores(0, s_ref)
    step(0, s_ref, s2_ref, True)

    def att_body(j, carry):
        @pl.when(j % 2 == 1)
        def _():
            step(j, s2_ref, s_ref, False)

        @pl.when(j % 2 == 0)
        def _():
            step(j, s_ref, s2_ref, False)

        return carry

    lax.fori_loop(1, n_att, att_body, 0)

    for h in range(DSA_HEADS):
        inv = 1.0 / l_ref[hrows(h), :]
        o_lat = acc_ref[hrows(h), :] * jnp.concatenate([inv] * (DSA_LATENT // 128), axis=1)
        o_ref[:, h * DSA_DV:(h + 1) * DSA_DV] = _dot(o_lat.astype(BF16), wuv_ref[h]).astype(o_ref.dtype)


def _dsa(proj, misc, kidx, clat, w_uk, w_uv, band, batch, seq):
    nqb = seq // QBLK
    t = batch * seq
    npad = clat.shape[1]
    top_k = min(INDEX_TOPK, seq // 4)
    hq = DSA_HEADS * QBLK
    tokb = lambda b, i: b * nqb + i
    const3 = lambda b, i: (0, 0, 0)
    return pl.pallas_call(
        functools.partial(_dsa_kernel, top_k),
        grid=(batch, nqb),
        in_specs=[pl.BlockSpec((QBLK, 2048), lambda b, i: (tokb(b, i), 3)),
                  pl.BlockSpec((QBLK, 2048), lambda b, i: (tokb(b, i), 4)),
                  pl.BlockSpec((1, QBLK, 128), lambda b, i: (b, i + PAD_FRONT // QBLK, 0)),
                  pl.BlockSpec((1, npad, IDX_DIM), lambda b, i: (b, 0, 0)),
                  pl.BlockSpec((1, npad, DSA_LATENT), lambda b, i: (b, 0, 0)),
                  pl.BlockSpec((DSA_HEADS, DSA_DH, DSA_LATENT), const3),
                  pl.BlockSpec((DSA_HEADS, DSA_LATENT, DSA_DV), const3),
                  pl.BlockSpec((DSA_HEADS, QBLK, 2 * QBLK), const3)],
        out_specs=pl.BlockSpec((QBLK, DSA_HEADS * DSA_DV), lambda b, i: (tokb(b, i), 0)),
        out_shape=jax.ShapeDtypeStruct((t, DSA_HEADS * DSA_DV), BF16),
        scratch_shapes=[pltpu.VMEM((QBLK, npad), I32),
                        pltpu.VMEM((hq, IDX_DIM), BF16),
                        pltpu.VMEM((hq, 128), F32),
                        pltpu.VMEM((hq, DSA_LATENT), BF16),
                        pltpu.VMEM((QBLK, KCH), F32),
                        pltpu.VMEM((hq, KCH), F32),
                        pltpu.VMEM((hq, KCH), F32),
                        pltpu.VMEM((hq, KCH), BF16),
                        pltpu.VMEM((hq, 128), F32),
                        pltpu.VMEM((hq, 128), F32),
                        pltpu.VMEM((hq, 128), F32),
                        pltpu.VMEM((hq, DSA_LATENT), F32)],
        compiler_params=_cparams(("parallel", "arbitrary")),
        name="dsa",
    )(proj, proj, misc, kidx, clat, w_uk, w_uv, band)


def _merge_kernel(yg_ref, yd_ref, gg_ref, gd_ref, wg_ref, wd_ref, o_ref):
    a = _dot(yg_ref[...], wg_ref[...])
    b = _dot(yd_ref[...], wd_ref[...])
    gg = jax.nn.sigmoid(gg_ref[...].astype(F32))
    gd = jax.nn.sigmoid(gd_ref[...].astype(F32))
    o_ref[...] = (gg * a + gd * b).astype(o_ref.dtype)


def _merge(y_gla, y_dsa, proj, w_g, w_d, tm, tn):
    t, d = y_gla.shape
    n = w_g.shape[1]
    ncb = n // tn
    return pl.pallas_call(
        _merge_kernel,
        grid=(t // tm, ncb),
        in_specs=[pl.BlockSpec((tm, d), lambda i, j: (i, 0)),
                  pl.BlockSpec((tm, d), lambda i, j: (i, 0)),
                  pl.BlockSpec((tm, tn), lambda i, j: (i, 5 * ncb + j)),
                  pl.BlockSpec((tm, tn), lambda i, j: (i, 6 * ncb + j)),
                  pl.BlockSpec((d, tn), lambda i, j: (0, j)),
                  pl.BlockSpec((d, tn), lambda i, j: (0, j))],
        out_specs=pl.BlockSpec((tm, tn), lambda i, j: (i, j)),
        out_shape=jax.ShapeDtypeStruct((t, n), BF16),
        compiler_params=_cparams(("parallel", "arbitrary")),
        name="merge",
    )(y_gla, y_dsa, proj, proj, w_g, w_d)


def _proj_norm_res_kernel(y_ref, w_ref, g_ref, h_ref, o_ref):
    y = _dot(y_ref[...], w_ref[...])
    o_ref[...] = h_ref[...] + _rms(y, g_ref[...])


def _proj_norm_res(y, w, gain, h, tm):
    t, d = y.shape
    n = w.shape[1]
    return pl.pallas_call(
        _proj_norm_res_kernel,
        grid=(t // tm,),
        in_specs=[pl.BlockSpec((tm, d), lambda i: (i, 0)),
                  pl.BlockSpec((d, n), lambda i: (0, 0)),
                  pl.BlockSpec((1, n), lambda i: (0, 0)),
                  pl.BlockSpec((tm, n), lambda i: (i, 0))],
        out_specs=pl.BlockSpec((tm, n), lambda i: (i, 0)),
        out_shape=jax.ShapeDtypeStruct((t, n), F32),
        compiler_params=_cparams(("parallel",)),
        name="proj_norm_res",
    )(y, w, gain, h)


def _xattn_kernel(q_ref, k_ref, v_ref, o_ref):
    for h in range(XA_HEADS):
        hs = slice(h * XA_DH, (h + 1) * XA_DH)
        s = _dot_nt(q_ref[:, hs], k_ref[0][:, hs]) * (XA_DH ** -0.5)
        m = jnp.max(s, axis=-1, keepdims=True)
        p = jnp.exp(s - m)
        p = p / jnp.sum(p, axis=-1, keepdims=True)
        o_ref[:, hs] = _dot(p.astype(BF16), v_ref[0][:, hs]).astype(o_ref.dtype)


def _xattn(q, kv, batch, seq, tm):
    t, d = q.shape
    n_mem = kv.shape[1]
    nb = seq // tm
    return pl.pallas_call(
        _xattn_kernel,
        grid=(batch, nb),
        in_specs=[pl.BlockSpec((tm, d), lambda b, i: (b * nb + i, 0)),
                  pl.BlockSpec((1, n_mem, d), lambda b, i: (b, 0, 0)),
                  pl.BlockSpec((1, n_mem, d), lambda b, i: (b, 0, 1))],
        out_specs=pl.BlockSpec((tm, d), lambda b, i: (b * nb + i, 0)),
        out_shape=jax.ShapeDtypeStruct((t, d), BF16),
        compiler_params=_cparams(("parallel", "parallel")),
        name="xattn",
    )(q, kv, kv)


def _ffn_kernel(h_ref, g_ref, wg_ref, wu_ref, wd_ref, pg_ref, o_ref, u_ref, acc_ref):
    f = pl.program_id(1)

    @pl.when(f == 0)
    def _():
        u_ref[...] = _rms(h_ref[...], g_ref[...]).astype(BF16)
        acc_ref[...] = jnp.zeros_like(acc_ref)

    u = u_ref[...]
    a = _dot(u, wg_ref[...])
    b = _dot(u, wu_ref[...])
    act = (a * jax.nn.sigmoid(a) * b).astype(BF16)
    acc_ref[...] += _dot(act, wd_ref[...])

    @pl.when(f == pl.num_programs(1) - 1)
    def _():
        o_ref[...] = h_ref[...] + _rms(acc_ref[...], pg_ref[...])


def _ffn(h, pre_gain, w_gate, w_up, w_down, post_gain, tm, tf):
    t, d = h.shape
    ff = w_gate.shape[1]
    return pl.pallas_call(
        _ffn_kernel,
        grid=(t // tm, ff // tf),
        in_specs=[pl.BlockSpec((tm, d), lambda i, f: (i, 0)),
                  pl.BlockSpec((1, d), lambda i, f: (0, 0)),
                  pl.BlockSpec((d, tf), lambda i, f: (0, f)),
                  pl.BlockSpec((d, tf), lambda i, f: (0, f)),
                  pl.BlockSpec((tf, d), lambda i, f: (f, 0)),
                  pl.BlockSpec((1, d), lambda i, f: (0, 0))],
        out_specs=pl.BlockSpec((tm, d), lambda i, f: (i, 0)),
        out_shape=jax.ShapeDtypeStruct((t, d), F32),
        scratch_shapes=[pltpu.VMEM((tm, d), BF16), pltpu.VMEM((tm, d), F32)],
        compiler_params=_cparams(("parallel", "arbitrary")),
        name="ffn",
    )(h, pre_gain, w_gate, w_up, w_down, post_gain)


def _row(v):
    return v.reshape(1, -1).astype(F32)


def _layer(h, mem, w_in, gla_w_a2, gla_b_a, gla_out_norm, dsa_w_uk, dsa_w_uv, dsa_latent_norm,
           idx_k_norm_w, idx_k_norm_b, band, w_gla_branch, w_dsa_branch, w_mix_out,
           mix_pre_norm, mix_post_norm, xa_pre_norm, xa_post_norm, xa_mem_norm,
           w_xa_q, w_xa_kv, w_xa_o, ffn_pre_norm, ffn_post_norm, w_ffn_gate, w_ffn_up, w_ffn_down,
           batch, seq):
    d = D_MODEL
    t = batch * seq
    cols = lambda a, b: w_in[:, a:b]
    w_main = jnp.concatenate(
        [cols(_O_GQ, _O_GA), cols(_O_DQ, _O_DC), cols(_O_IQ, _O_IK), cols(_O_GG, _O_END)], axis=1).astype(BF16)
    w_small = jnp.concatenate(
        [cols(_O_DC, _O_IQ), cols(_O_IK, _O_IW), cols(_O_GA, _O_DQ), cols(_O_IW, _O_GG),
         jnp.zeros((d, SMALL_COLS - 416), w_in.dtype)], axis=1).astype(BF16)

    pre = _row(mix_pre_norm)
    proj = _norm_matmul(h, pre, w_main, BF16, 1024, 512)
    clat, kidx, misc = _small_proj(h, pre, w_small, _row(dsa_latent_norm), _row(idx_k_norm_w),
                                   _row(idx_k_norm_b), batch, seq)

    y_gla = _gla(proj, misc, gla_w_a2.astype(BF16), _row(gla_b_a), _row(gla_out_norm), batch, seq)
    y_dsa = _dsa(proj, misc, kidx, clat, dsa_w_uk.astype(BF16), dsa_w_uv.astype(BF16), band, batch, seq)

    merged = _merge(y_gla, y_dsa, proj, w_gla_branch.astype(BF16), w_dsa_branch.astype(BF16), 1024, 512)
    h = _proj_norm_res(merged, w_mix_out.astype(BF16), _row(mix_post_norm), h, 512)

    n_mem = mem.shape[1]
    kv = _norm_matmul(mem.reshape(batch * n_mem, d), _row(xa_mem_norm), w_xa_kv.astype(BF16), BF16,
                      batch * n_mem, 512).reshape(batch, n_mem, 2 * d)
    q = _norm_matmul(h, _row(xa_pre_norm), w_xa_q.astype(BF16), BF16, 1024, 512)
    o = _xattn(q, kv, batch, seq, 512)
    h = _proj_norm_res(o, w_xa_o.astype(BF16), _row(xa_post_norm), h, 512)

    h = _ffn(h, _row(ffn_pre_norm), w_ffn_gate.astype(BF16), w_ffn_up.astype(BF16),
             w_ffn_down.astype(BF16), _row(ffn_post_norm), 512, 512)
    return h


def kernel(x, mem, w_in, gla_w_a2, gla_b_a, gla_out_norm, dsa_w_uk, dsa_w_uv, dsa_latent_norm,
           idx_k_norm_w, idx_k_norm_b, rel_bias, w_gla_branch, w_dsa_branch, w_mix_out,
           mix_pre_norm, mix_post_norm, xa_pre_norm, xa_post_norm, xa_mem_norm,
           w_xa_q, w_xa_kv, w_xa_o, ffn_pre_norm, ffn_post_norm, w_ffn_gate, w_ffn_up, w_ffn_down):
    batch, seq, d = x.shape
    depth = w_in.shape[0]
    band = _bias_band(rel_bias.astype(F32))
    h = x.reshape(batch * seq, d)
    for l in range(depth):
        h = _layer(h, mem, w_in[l], gla_w_a2[l], gla_b_a[l], gla_out_norm[l], dsa_w_uk[l], dsa_w_uv[l],
                   dsa_latent_norm[l], idx_k_norm_w[l], idx_k_norm_b[l], band, w_gla_branch[l],
                   w_dsa_branch[l], w_mix_out[l], mix_pre_norm[l], mix_post_norm[l], xa_pre_norm[l],
                   xa_post_norm[l], xa_mem_norm[l], w_xa_q[l], w_xa_kv[l], w_xa_o[l], ffn_pre_norm[l],
                   ffn_post_norm[l], w_ffn_gate[l], w_ffn_up[l], w_ffn_down[l], batch, seq)
    return h.reshape(batch, seq, d)
```

```python
import functools
import math

import jax
import jax.numpy as jnp
import numpy as np
from jax import lax
from jax.experimental import pallas as pl
from jax.experimental.pallas import tpu as pltpu

F32 = jnp.float32
BF16 = jnp.bfloat16
I32 = jnp.int32
I16 = jnp.int16

D_MODEL = 2048
CHUNK = 64
EPS = 1e-6

GLA_HEADS = 4
GLA_DK = 256
GLA_DV = 512
GLA_RANK = 16
GLA_TAU = 16.0

DSA_HEADS = 16
DSA_DH = 128
DSA_DV = 128
DSA_LATENT = 256
IDX_HEADS = 16
IDX_DIM = 128
INDEX_TOPK = 256

REL_BUCKETS = 32
REL_MAX_DIST = 128

XA_HEADS = 4
XA_DH = 512

_SPLITS = (1024, 1024, 2048, 2048, 16, 2048, 256, 2048, 128, 16, 2048, 2048)
_OFFS = tuple(int(v) for v in np.cumsum((0,) + _SPLITS))
(_O_GQ, _O_GK, _O_GV, _O_GR, _O_GA, _O_DQ, _O_DC, _O_IQ, _O_IK, _O_IW, _O_GG, _O_GD, _O_END) = _OFFS

MAIN_COLS = 14336
SMALL_COLS = 512

QBLK = 128
KCH = 512
PAD_FRONT = KCH
INT_MIN = -2 ** 31
NEG = -1e30
LOG2E = 1.4426950408889634

VMEM_LIMIT = 56 * 1024 * 1024


def _cparams(sem):
    return pltpu.CompilerParams(dimension_semantics=sem, vmem_limit_bytes=VMEM_LIMIT)


def _rms(x, gain):
    ms = jnp.mean(x * x, axis=-1, keepdims=True)
    return x * lax.rsqrt(ms + EPS) * gain


def _dot(a, b):
    return jnp.dot(a, b, preferred_element_type=F32)


def _dot_nt(a, b):
    return lax.dot_general(a, b, (((1,), (1,)), ((), ())), preferred_element_type=F32)


def _norm_matmul_kernel(x_ref, g_ref, w_ref, o_ref, u_ref):
    @pl.when(pl.program_id(1) == 0)
    def _():
        u_ref[...] = _rms(x_ref[...], g_ref[...]).astype(BF16)

    o_ref[...] = _dot(u_ref[...], w_ref[...]).astype(o_ref.dtype)


def _norm_matmul(x, gain, w, out_dtype, tm, tn):
    t, d = x.shape
    n = w.shape[1]
    return pl.pallas_call(
        _norm_matmul_kernel,
        grid=(t // tm, n // tn),
        in_specs=[pl.BlockSpec((tm, d), lambda i, j: (i, 0)),
                  pl.BlockSpec((1, d), lambda i, j: (0, 0)),
                  pl.BlockSpec((d, tn), lambda i, j: (0, j))],
        out_specs=pl.BlockSpec((tm, tn), lambda i, j: (i, j)),
        out_shape=jax.ShapeDtypeStruct((t, n), out_dtype),
        scratch_shapes=[pltpu.VMEM((tm, d), BF16)],
        compiler_params=_cparams(("parallel", "arbitrary")),
        name="norm_matmul",
    )(x, gain, w)


def _small_proj(x, gain, w_small, lat_g, ik_w, ik_b, batch, seq):
    tm = PAD_FRONT
    assert seq % tm == 0
    nblk = seq // tm
    npad = PAD_FRONT + seq
    grid = (batch, 1 + nblk)

    def x_map(b, i):
        return (b * nblk + jnp.maximum(i - 1, 0), 0)

    def kernel(x_ref, g_ref, w_ref, lat_g_ref, ik_w_ref, ik_b_ref, c_ref, k_ref, misc_ref):
        is_pad = pl.program_id(1) == 0

        @pl.when(is_pad)
        def _():
            c_ref[...] = jnp.zeros_like(c_ref)
            k_ref[...] = jnp.zeros_like(k_ref)
            misc_ref[...] = jnp.zeros_like(misc_ref)

        @pl.when(jnp.logical_not(is_pad))
        def _():
            u = _rms(x_ref[...], g_ref[...]).astype(BF16)
            p = _dot(u, w_ref[...])
            c_ref[0] = _rms(p[:, 0:256], lat_g_ref[...]).astype(BF16)
            ik = p[:, 256:384]
            mu = jnp.mean(ik, axis=-1, keepdims=True)
            xc = ik - mu
            var = jnp.mean(xc * xc, axis=-1, keepdims=True)
            k_ref[0] = (xc * lax.rsqrt(var + EPS) * ik_w_ref[...] + ik_b_ref[...]).astype(BF16)
            misc_ref[0] = p[:, 384:512]

    d = x.shape[1]
    const = lambda b, i: (0, 0)
    return pl.pallas_call(
        kernel,
        grid=grid,
        in_specs=[pl.BlockSpec((tm, d), x_map),
                  pl.BlockSpec((1, d), const),
                  pl.BlockSpec((d, SMALL_COLS), const),
                  pl.BlockSpec((1, DSA_LATENT), const),
                  pl.BlockSpec((1, IDX_DIM), const),
                  pl.BlockSpec((1, IDX_DIM), const)],
        out_specs=[pl.BlockSpec((1, tm, DSA_LATENT), lambda b, i: (b, i, 0)),
                   pl.BlockSpec((1, tm, IDX_DIM), lambda b, i: (b, i, 0)),
                   pl.BlockSpec((1, tm, 128), lambda b, i: (b, i, 0))],
        out_shape=[jax.ShapeDtypeStruct((batch, npad, DSA_LATENT), BF16),
                   jax.ShapeDtypeStruct((batch, npad, IDX_DIM), BF16),
                   jax.ShapeDtypeStruct((batch, npad, 128), F32)],
        compiler_params=_cparams(("parallel", "arbitrary")),
        name="small_proj",
    )(x, gain, w_small, lat_g, ik_w, ik_b)


def _log_sigmoid(z):
    return jnp.minimum(z, 0.0) - jnp.log1p(jnp.exp(-jnp.abs(z)))


def _split3(x):
    h = x.astype(BF16)
    r = x - h.astype(F32)
    m = r.astype(BF16)
    l = (r - m.astype(F32)).astype(BF16)
    return h, m, l


def _gla_kernel(nb, q_ref, k_ref, v_ref, r_ref, misc_ref, wa2_ref, ba_ref, on_ref, o_ref, state_ref):
    @pl.when(pl.program_id(0) == 0)
    def _():
        state_ref[...] = jnp.zeros_like(state_ref)

    rows = nb * CHUNK
    stack = lambda ref, cols: jnp.concatenate([ref[i][:, cols] for i in range(nb)], axis=0)
    row = lax.broadcasted_iota(I32, (rows, rows), 0)
    col = lax.broadcasted_iota(I32, (rows, rows), 1)
    same = (row // CHUNK) == (col // CHUNK)
    lower = jnp.logical_and(same, col <= row)
    upper = jnp.logical_and(same, col > row)
    tril = jnp.where(lower, 1.0, 0.0).astype(BF16)

    a_low = stack(misc_ref, slice(0, GLA_RANK)).astype(BF16)
    z = _dot(a_low, wa2_ref[...]) + ba_ref[...]
    la = _log_sigmoid(z) * (1.0 / GLA_TAU)
    l_h, l_m, l_l = _split3(la)
    b_all = _dot(tril, l_h) + _dot(tril, l_m) + _dot(tril, l_l)

    for h in range(GLA_HEADS):
        ks = slice(h * GLA_DK, (h + 1) * GLA_DK)
        vs = slice(h * GLA_DV, (h + 1) * GLA_DV)
        b = b_all[:, ks]
        eb = jnp.exp(b)
        ebi = jnp.exp(-b)
        q = stack(q_ref, ks).astype(F32) * (GLA_DK ** -0.5)
        k = stack(k_ref, ks).astype(F32)
        v = stack(v_ref, vs)
        q_fwd = (q * eb).astype(BF16)
        a_lo = _dot_nt(q_fwd, (k * ebi).astype(BF16))
        a_up = _dot_nt((q * ebi).astype(BF16), (k * eb).astype(BF16))
        scores = jnp.where(lower, a_lo, jnp.where(upper, a_up, 0.0)).astype(BF16)
        o_intra = _dot(scores, v)
        outs = []
        for i in range(nb):
            rs = slice(i * CHUNK, (i + 1) * CHUNK)
            b_i = b[rs]
            b_last = b_i[CHUNK - 1:CHUNK, :]
            st = state_ref[i * GLA_HEADS + h]
            outs.append(o_intra[rs] + _dot_nt(q_fwd[rs], st.astype(BF16)))
            k_dec = (k[rs] * jnp.exp(b_last - b_i)).astype(BF16)
            v_t = v[rs].astype(F32).T.astype(BF16)
            state_ref[i * GLA_HEADS + h] = st * jnp.exp(b_last) + _dot(v_t, k_dec)
        o = _rms(jnp.concatenate(outs, axis=0), on_ref[...])
        r = stack(r_ref, vs).astype(F32)
        y = (o * (r * jax.nn.sigmoid(r))).astype(o_ref.dtype)
        for i in range(nb):
            o_ref[i, :, vs] = y[i * CHUNK:(i + 1) * CHUNK]


def _gla(proj, misc, w_a2, b_a, out_norm, batch, seq):
    nc = seq // CHUNK
    proj3 = proj.reshape(batch, seq, proj.shape[1])
    y = pl.pallas_call(
        functools.partial(_gla_kernel, batch),
        grid=(nc,),
        in_specs=[pl.BlockSpec((batch, CHUNK, 1024), lambda c: (0, c, 0)),
                  pl.BlockSpec((batch, CHUNK, 1024), lambda c: (0, c, 1)),
                  pl.BlockSpec((batch, CHUNK, 2048), lambda c: (0, c, 1)),
                  pl.BlockSpec((batch, CHUNK, 2048), lambda c: (0, c, 2)),
                  pl.BlockSpec((batch, CHUNK, 128), lambda c: (0, c + PAD_FRONT // CHUNK, 0)),
                  pl.BlockSpec((GLA_RANK, GLA_HEADS * GLA_DK), lambda c: (0, 0)),
                  pl.BlockSpec((1, GLA_HEADS * GLA_DK), lambda c: (0, 0)),
                  pl.BlockSpec((1, GLA_DV), lambda c: (0, 0))],
        out_specs=pl.BlockSpec((batch, CHUNK, GLA_HEADS * GLA_DV), lambda c: (0, c, 0)),
        out_shape=jax.ShapeDtypeStruct((batch, seq, GLA_HEADS * GLA_DV), BF16),
        scratch_shapes=[pltpu.VMEM((batch * GLA_HEADS, GLA_DV, GLA_DK), F32)],
        compiler_params=_cparams(("arbitrary",)),
        name="gla",
    )(proj3, proj3, proj3, proj3, misc, w_a2, b_a, out_norm)
    return y.reshape(batch * seq, GLA_HEADS * GLA_DV)


def _t5_bucket(rel):
    half = REL_BUCKETS // 2
    max_exact = half // 2
    ret = jnp.where(rel > 0, half, 0)
    n = jnp.abs(rel)
    nf = jnp.maximum(n, 1).astype(jnp.float32)
    large = max_exact + (jnp.log(nf / max_exact) / math.log(REL_MAX_DIST / max_exact)
                         * (half - max_exact)).astype(jnp.int32)
    large = jnp.minimum(large, half - 1)
    return ret + jnp.where(n < max_exact, n, large)


def _bias_band_kernel(bucket_ref, rb_ref, o_ref):
    far = REL_BUCKETS // 2 - 1
    bucket = bucket_ref[...]
    for h in range(DSA_HEADS):
        acc = jnp.zeros(bucket.shape, F32)
        for b in range(REL_BUCKETS):
            acc = jnp.where(bucket == b, rb_ref[b, h], acc)
        o_ref[h] = (acc - rb_ref[far, h]) * LOG2E


def _bias_band(rel_bias):
    t = jnp.arange(QBLK, dtype=jnp.int32)[:, None]
    j = jnp.arange(2 * QBLK, dtype=jnp.int32)[None, :]
    bucket = _t5_bucket(j - QBLK - t).astype(jnp.int32)
    return pl.pallas_call(
        _bias_band_kernel,
        in_specs=[pl.BlockSpec(memory_space=pltpu.VMEM), pl.BlockSpec(memory_space=pltpu.SMEM)],
        out_specs=pl.BlockSpec(memory_space=pltpu.VMEM),
        out_shape=jax.ShapeDtypeStruct((DSA_HEADS, QBLK, 2 * QBLK), F32),
        name="bias_band",
    )(bucket, rel_bias)


def _sortable(x):
    i = pltpu.bitcast(x, I32)
    return jnp.where(i < 0, i ^ jnp.int32(0x7FFFFFFF), i)


def _dsa_kernel(top_k, dq_ref, iq_ref, misc_ref, kidx_ref, clat_ref, wuk_ref, wuv_ref, band_ref,
                o_ref, keys_ref, hi_ref, lo_ref, iqs_ref, wb_ref, qlat_ref, madd_ref, s_ref, s2_ref, p_ref, alpha_ref,
                m_ref, l_ref, acc_ref):
    qb = pl.program_id(1)
    start = qb * QBLK
    hrows = lambda h: slice(h * QBLK, (h + 1) * QBLK)

    w_scale = IDX_HEADS ** -0.5 * IDX_DIM ** -0.5
    wq = misc_ref[0][:, GLA_RANK:GLA_RANK + IDX_HEADS] * w_scale
    for h in range(IDX_HEADS):
        wb_ref[hrows(h), :] = jnp.broadcast_to(wq[:, h:h + 1], (QBLK, 128))
        iqs_ref[hrows(h), :] = iq_ref[:, h * IDX_DIM:(h + 1) * IDX_DIM]

    row = lax.broadcasted_iota(I32, (QBLK, KCH), 0)
    lane = lax.broadcasted_iota(I32, (QBLK, KCH), 1)
    p_lim = start + (row // CHUNK + 1) * CHUNK + PAD_FRONT

    n_chunks = (start + PAD_FRONT + QBLK + KCH - 1) // KCH
    keys_ref[:, 0:KCH] = jnp.full((QBLK, KCH), INT_MIN, I32)

    def idx_body(c, carry):
        off = pl.multiple_of(c * KCH, KCH)
        kc = kidx_ref[0, pl.ds(off, KCH), :]
        s_ref[...] = _dot_nt(iqs_ref[...], kc)
        acc = jnp.zeros((QBLK, KCH), F32)
        for h in range(IDX_HEADS):
            wbh = wb_ref[hrows(h), :]
            acc = acc + jnp.concatenate([wbh] * (KCH // 128), axis=1) * jnp.maximum(s_ref[hrows(h), :], 0.0)
        key = jnp.where(lane + off < p_lim, _sortable(acc), INT_MIN)
        keys_ref[:, pl.ds(off, KCH)] = key
        hi_ref[:, pl.ds(off, KCH)] = (key >> 16).astype(I16)
        lo_ref[:, pl.ds(off, KCH)] = ((key & 0xFFFF) - 32768).astype(I16)
        return carry

    lax.fori_loop(1, n_chunks, idx_body, 0)

    one16 = jnp.ones((QBLK, 128), I16)
    zero16 = jnp.zeros((QBLK, 128), I16)
    min16 = jnp.full((QBLK, 128), -32768, I16)

    def row_total(cnt16):
        return jnp.broadcast_to(jnp.sum(cnt16.astype(F32), axis=1, keepdims=True), (QBLK, 128))

    def kth_largest16(ref, k):
        def count_ge(cand16):
            def body(c, cnt):
                kk = ref[:, pl.ds(pl.multiple_of(c * KCH, KCH), KCH)]
                for s in range(KCH // 128):
                    cnt = cnt + jnp.where(kk[:, s * 128:(s + 1) * 128] >= cand16, one16, zero16)
                return cnt
            return row_total(lax.fori_loop(1, n_chunks, body, zero16))

        def bit_body(i, prefix):
            cand = prefix + (jnp.int32(1) << (15 - i))
            return jnp.where(count_ge(cand.astype(I16)) >= k, cand, prefix)

        return lax.fori_loop(0, 16, bit_body, jnp.full((QBLK, 128), -32768, I32))

    k_all = jnp.full((QBLK, 128), top_k, F32)
    t_hi = kth_largest16(hi_ref, k_all)
    t_hi16 = t_hi.astype(I16)

    def tie_body(c, cnt):
        off = pl.multiple_of(c * KCH, KCH)
        hh = hi_ref[:, pl.ds(off, KCH)]
        ll = lo_ref[:, pl.ds(off, KCH)]
        t4 = jnp.concatenate([t_hi16] * (KCH // 128), axis=1)
        lo_ref[:, pl.ds(off, KCH)] = jnp.where(hh == t4, ll, jnp.concatenate([min16] * (KCH // 128), axis=1))
        for s in range(KCH // 128):
            cnt = cnt + jnp.where(hh[:, s * 128:(s + 1) * 128] > t_hi16, one16, zero16)
        return cnt

    n_above = row_total(lax.fori_loop(1, n_chunks, tie_body, zero16))
    t_lo = kth_largest16(lo_ref, k_all - n_above)
    thr = t_hi * 65536 + (t_lo + 32768)
    thr = jnp.maximum(thr, INT_MIN + 1)

    for h in range(DSA_HEADS):
        ql = _dot(dq_ref[:, h * DSA_DH:(h + 1) * DSA_DH], wuk_ref[h]) * (LOG2E * DSA_DH ** -0.5)
        qlat_ref[hrows(h), :] = ql.astype(BF16)

    m_ref[...] = jnp.full(m_ref.shape, NEG, F32)
    l_ref[...] = jnp.zeros(l_ref.shape, F32)
    acc_ref[...] = jnp.zeros(acc_ref.shape, F32)

    end = start + PAD_FRONT + QBLK
    n_att = (start + QBLK + KCH - 1) // KCH

    def chunk_off(j):
        return pl.multiple_of(end - KCH * (j + 1), 128)

    def scores(j, dst_ref):
        cc = clat_ref[0, pl.ds(chunk_off(j), KCH), :]
        dst_ref[...] = _dot_nt(qlat_ref[...], cc)

    def softmax_update(j, src_ref, near):
        off = chunk_off(j)
        kk = keys_ref[:, pl.ds(off, KCH)]
        madd_ref[...] = jnp.where(kk >= jnp.concatenate([thr] * (KCH // 128), axis=1), 0.0, NEG)
        cc = clat_ref[0, pl.ds(off, KCH), :]
        for h in range(DSA_HEADS):
            s = src_ref[hrows(h), :] + madd_ref[...]
            if near:
                s = jnp.concatenate([s[:, :KCH - 2 * QBLK], s[:, KCH - 2 * QBLK:] + band_ref[h]], axis=1)
            m_old = m_ref[hrows(h), :]
            m_new = jnp.maximum(m_old, jnp.broadcast_to(jnp.max(s, axis=1, keepdims=True), (QBLK, 128)))
            alpha = jnp.exp2(m_old - m_new)
            p = jnp.exp2(s - jnp.concatenate([m_new] * (KCH // 128), axis=1))
            l_ref[hrows(h), :] = (alpha * l_ref[hrows(h), :]
                                  + jnp.broadcast_to(jnp.sum(p, axis=1, keepdims=True), (QBLK, 128)))
            m_ref[hrows(h), :] = m_new
            alpha_ref[hrows(h), :] = alpha
            p_ref[hrows(h), :] = p.astype(BF16)
        al = alpha_ref[...]
        acc_ref[...] = (jnp.concatenate([al] * (DSA_LATENT // 128), axis=1) * acc_ref[...]
                        + _dot(p_ref[...], cc))

    def step(j, cur_ref, nxt_ref, near):
        scores(jnp.minimum(j + 1, n_att - 1), nxt_ref)
        softmax_update(j, cur_ref, near)

    scores(0, s_ref)
    step(0, s_ref, s2_ref, True)

    def att_body(j, carry):
        @pl.when(j % 2 == 1)
        def _():
            step(j, s2_ref, s_ref, False)

        @pl.when(j % 2 == 0)
        def _():
            step(j, s_ref, s2_ref, False)

        return carry

    lax.fori_loop(1, n_att, att_body, 0)

    for h in range(DSA_HEADS):
        inv = 1.0 / l_ref[hrows(h), :]
        o_lat = acc_ref[hrows(h), :] * jnp.concatenate([inv] * (DSA_LATENT // 128), axis=1)
        o_ref[:, h * DSA_DV:(h + 1) * DSA_DV] = _dot(o_lat.astype(BF16), wuv_ref[h]).astype(o_ref.dtype)


def _dsa(proj, misc, kidx, clat, w_uk, w_uv, band, batch, seq):
    nqb = seq // QBLK
    t = batch * seq
    npad = clat.shape[1]
    top_k = min(INDEX_TOPK, seq // 4)
    hq = DSA_HEADS * QBLK
    tokb = lambda b, i: b * nqb + i
    const3 = lambda b, i: (0, 0, 0)
    return pl.pallas_call(
        functools.partial(_dsa_kernel, top_k),
        grid=(batch, nqb),
        in_specs=[pl.BlockSpec((QBLK, 2048), lambda b, i: (tokb(b, i), 3)),
                  pl.BlockSpec((QBLK, 2048), lambda b, i: (tokb(b, i), 4)),
                  pl.BlockSpec((1, QBLK, 128), lambda b, i: (b, i + PAD_FRONT // QBLK, 0)),
                  pl.BlockSpec((1, npad, IDX_DIM), lambda b, i: (b, 0, 0)),
                  pl.BlockSpec((1, npad, DSA_LATENT), lambda b, i: (b, 0, 0)),
                  pl.BlockSpec((DSA_HEADS, DSA_DH, DSA_LATENT), const3),
                  pl.BlockSpec((DSA_HEADS, DSA_LATENT, DSA_DV), const3),
                  pl.BlockSpec((DSA_HEADS, QBLK, 2 * QBLK), const3)],
        out_specs=pl.BlockSpec((QBLK, DSA_HEADS * DSA_DV), lambda b, i: (tokb(b, i), 0)),
        out_shape=jax.ShapeDtypeStruct((t, DSA_HEADS * DSA_DV), BF16),
        scratch_shapes=[pltpu.VMEM((QBLK, npad), I32),
                        pltpu.VMEM((QBLK, npad), I16),
                        pltpu.VMEM((QBLK, npad), I16),
                        pltpu.VMEM((hq, IDX_DIM), BF16),
                        pltpu.VMEM((hq, 128), F32),
                        pltpu.VMEM((hq, DSA_LATENT), BF16),
                        pltpu.VMEM((QBLK, KCH), F32),
                        pltpu.VMEM((hq, KCH), F32),
                        pltpu.VMEM((hq, KCH), F32),
                        pltpu.VMEM((hq, KCH), BF16),
                        pltpu.VMEM((hq, 128), F32),
                        pltpu.VMEM((hq, 128), F32),
                        pltpu.VMEM((hq, 128), F32),
                        pltpu.VMEM((hq, DSA_LATENT), F32)],
        compiler_params=_cparams(("parallel", "arbitrary")),
        name="dsa",
    )(proj, proj, misc, kidx, clat, w_uk, w_uv, band)


def _merge_kernel(yg_ref, yd_ref, gg_ref, gd_ref, wg_ref, wd_ref, o_ref):
    a = _dot(yg_ref[...], wg_ref[...])
    b = _dot(yd_ref[...], wd_ref[...])
    gg = jax.nn.sigmoid(gg_ref[...].astype(F32))
    gd = jax.nn.sigmoid(gd_ref[...].astype(F32))
    o_ref[...] = (gg * a + gd * b).astype(o_ref.dtype)


def _merge(y_gla, y_dsa, proj, w_g, w_d, tm, tn):
    t, d = y_gla.shape
    n = w_g.shape[1]
    ncb = n // tn
    return pl.pallas_call(
        _merge_kernel,
        grid=(t // tm, ncb),
        in_specs=[pl.BlockSpec((tm, d), lambda i, j: (i, 0)),
                  pl.BlockSpec((tm, d), lambda i, j: (i, 0)),
                  pl.BlockSpec((tm, tn), lambda i, j: (i, 5 * ncb + j)),
                  pl.BlockSpec((tm, tn), lambda i, j: (i, 6 * ncb + j)),
                  pl.BlockSpec((d, tn), lambda i, j: (0, j)),
                  pl.BlockSpec((d, tn), lambda i, j: (0, j))],
        out_specs=pl.BlockSpec((tm, tn), lambda i, j: (i, j)),
        out_shape=jax.ShapeDtypeStruct((t, n), BF16),
        compiler_params=_cparams(("parallel", "arbitrary")),
        name="merge",
    )(y_gla, y_dsa, proj, proj, w_g, w_d)


def _proj_norm_res_kernel(y_ref, w_ref, g_ref, h_ref, o_ref):
    y = _dot(y_ref[...], w_ref[...])
    o_ref[...] = h_ref[...] + _rms(y, g_ref[...])


def _proj_norm_res(y, w, gain, h, tm):
    t, d = y.shape
    n = w.shape[1]
    return pl.pallas_call(
        _proj_norm_res_kernel,
        grid=(t // tm,),
        in_specs=[pl.BlockSpec((tm, d), lambda i: (i, 0)),
                  pl.BlockSpec((d, n), lambda i: (0, 0)),
                  pl.BlockSpec((1, n), lambda i: (0, 0)),
                  pl.BlockSpec((tm, n), lambda i: (i, 0))],
        out_specs=pl.BlockSpec((tm, n), lambda i: (i, 0)),
        out_shape=jax.ShapeDtypeStruct((t, n), F32),
        compiler_params=_cparams(("parallel",)),
        name="proj_norm_res",
    )(y, w, gain, h)


def _xattn_kernel(q_ref, k_ref, v_ref, o_ref):
    for h in range(XA_HEADS):
        hs = slice(h * XA_DH, (h + 1) * XA_DH)
        s = _dot_nt(q_ref[:, hs], k_ref[0][:, hs]) * (XA_DH ** -0.5)
        m = jnp.max(s, axis=-1, keepdims=True)
        p = jnp.exp(s - m)
        p = p / jnp.sum(p, axis=-1, keepdims=True)
        o_ref[:, hs] = _dot(p.astype(BF16), v_ref[0][:, hs]).astype(o_ref.dtype)


def _xattn(q, kv, batch, seq, tm):
    t, d = q.shape
    n_mem = kv.shape[1]
    nb = seq // tm
    return pl.pallas_call(
        _xattn_kernel,
        grid=(batch, nb),
        in_specs=[pl.BlockSpec((tm, d), lambda b, i: (b * nb + i, 0)),
                  pl.BlockSpec((1, n_mem, d), lambda b, i: (b, 0, 0)),
                  pl.BlockSpec((1, n_mem, d), lambda b, i: (b, 0, 1))],
        out_specs=pl.BlockSpec((tm, d), lambda b, i: (b * nb + i, 0)),
        out_shape=jax.ShapeDtypeStruct((t, d), BF16),
        compiler_params=_cparams(("parallel", "parallel")),
        name="xattn",
    )(q, kv, kv)


def _ffn_kernel(h_ref, g_ref, wg_ref, wu_ref, wd_ref, pg_ref, o_ref, u_ref, acc_ref):
    f = pl.program_id(1)

    @pl.when(f == 0)
    def _():
        u_ref[...] = _rms(h_ref[...], g_ref[...]).astype(BF16)
        acc_ref[...] = jnp.zeros_like(acc_ref)

    u = u_ref[...]
    a = _dot(u, wg_ref[...])
    b = _dot(u, wu_ref[...])
    act = (a * jax.nn.sigmoid(a) * b).astype(BF16)
    acc_ref[...] += _dot(act, wd_ref[...])

    @pl.when(f == pl.num_programs(1) - 1)
    def _():
        o_ref[...] = h_ref[...] + _rms(acc_ref[...], pg_ref[...])


def _ffn(h, pre_gain, w_gate, w_up, w_down, post_gain, tm, tf):
    t, d = h.shape
    ff = w_gate.shape[1]
    return pl.pallas_call(
        _ffn_kernel,
        grid=(t // tm, ff // tf),
        in_specs=[pl.BlockSpec((tm, d), lambda i, f: (i, 0)),
                  pl.BlockSpec((1, d), lambda i, f: (0, 0)),
                  pl.BlockSpec((d, tf), lambda i, f: (0, f)),
                  pl.BlockSpec((d, tf), lambda i, f: (0, f)),
                  pl.BlockSpec((tf, d), lambda i, f: (f, 0)),
                  pl.BlockSpec((1, d), lambda i, f: (0, 0))],
        out_specs=pl.BlockSpec((tm, d), lambda i, f: (i, 0)),
        out_shape=jax.ShapeDtypeStruct((t, d), F32),
        scratch_shapes=[pltpu.VMEM((tm, d), BF16), pltpu.VMEM((tm, d), F32)],
        compiler_params=_cparams(("parallel", "arbitrary")),
        name="ffn",
    )(h, pre_gain, w_gate, w_up, w_down, post_gain)


def _row(v):
    return v.reshape(1, -1).astype(F32)


def _layer(h, mem, w_in, gla_w_a2, gla_b_a, gla_out_norm, dsa_w_uk, dsa_w_uv, dsa_latent_norm,
           idx_k_norm_w, idx_k_norm_b, band, w_gla_branch, w_dsa_branch, w_mix_out,
           mix_pre_norm, mix_post_norm, xa_pre_norm, xa_post_norm, xa_mem_norm,
           w_xa_q, w_xa_kv, w_xa_o, ffn_pre_norm, ffn_post_norm, w_ffn_gate, w_ffn_up, w_ffn_down,
           batch, seq):
    d = D_MODEL
    t = batch * seq
    cols = lambda a, b: w_in[:, a:b]
    w_main = jnp.concatenate(
        [cols(_O_GQ, _O_GA), cols(_O_DQ, _O_DC), cols(_O_IQ, _O_IK), cols(_O_GG, _O_END)], axis=1).astype(BF16)
    w_small = jnp.concatenate(
        [cols(_O_DC, _O_IQ), cols(_O_IK, _O_IW), cols(_O_GA, _O_DQ), cols(_O_IW, _O_GG),
         jnp.zeros((d, SMALL_COLS - 416), w_in.dtype)], axis=1).astype(BF16)

    pre = _row(mix_pre_norm)
    proj = _norm_matmul(h, pre, w_main, BF16, 1024, 512)
    clat, kidx, misc = _small_proj(h, pre, w_small, _row(dsa_latent_norm), _row(idx_k_norm_w),
                                   _row(idx_k_norm_b), batch, seq)

    y_gla = _gla(proj, misc, gla_w_a2.astype(BF16), _row(gla_b_a), _row(gla_out_norm), batch, seq)
    y_dsa = _dsa(proj, misc, kidx, clat, dsa_w_uk.astype(BF16), dsa_w_uv.astype(BF16), band, batch, seq)

    merged = _merge(y_gla, y_dsa, proj, w_gla_branch.astype(BF16), w_dsa_branch.astype(BF16), 1024, 512)
    h = _proj_norm_res(merged, w_mix_out.astype(BF16), _row(mix_post_norm), h, 512)

    n_mem = mem.shape[1]
    kv = _norm_matmul(mem.reshape(batch * n_mem, d), _row(xa_mem_norm), w_xa_kv.astype(BF16), BF16,
                      batch * n_mem, 512).reshape(batch, n_mem, 2 * d)
    q = _norm_matmul(h, _row(xa_pre_norm), w_xa_q.astype(BF16), BF16, 1024, 512)
    o = _xattn(q, kv, batch, seq, 512)
    h = _proj_norm_res(o, w_xa_o.astype(BF16), _row(xa_post_norm), h, 512)

    h = _ffn(h, _row(ffn_pre_norm), w_ffn_gate.astype(BF16), w_ffn_up.astype(BF16),
             w_ffn_down.astype(BF16), _row(ffn_post_norm), 512, 512)
    return h


def kernel(x, mem, w_in, gla_w_a2, gla_b_a, gla_out_norm, dsa_w_uk, dsa_w_uv, dsa_latent_norm,
           idx_k_norm_w, idx_k_norm_b, rel_bias, w_gla_branch, w_dsa_branch, w_mix_out,
           mix_pre_norm, mix_post_norm, xa_pre_norm, xa_post_norm, xa_mem_norm,
           w_xa_q, w_xa_kv, w_xa_o, ffn_pre_norm, ffn_post_norm, w_ffn_gate, w_ffn_up, w_ffn_down):
    batch, seq, d = x.shape
    depth = w_in.shape[0]
    band = _bias_band(rel_bias.astype(F32))
    h = x.reshape(batch * seq, d)
    for l in range(depth):
        h = _layer(h, mem, w_in[l], gla_w_a2[l], gla_b_a[l], gla_out_norm[l], dsa_w_uk[l], dsa_w_uv[l],
                   dsa_latent_norm[l], idx_k_norm_w[l], idx_k_norm_b[l], band, w_gla_branch[l],
                   w_dsa_branch[l], w_mix_out[l], mix_pre_norm[l], mix_post_norm[l], xa_pre_norm[l],
                   xa_post_norm[l], xa_mem_norm[l], w_xa_q[l], w_xa_kv[l], w_xa_o[l], ffn_pre_norm[l],
                   ffn_post_norm[l], w_ffn_gate[l], w_ffn_up[l], w_ffn_down[l], batch, seq)
    return h.reshape(batch, seq, d)
```

```python
import functools
import math

import jax
import jax.numpy as jnp
import numpy as np
from jax import lax
from jax.experimental import pallas as pl
from jax.experimental.pallas import tpu as pltpu

F32 = jnp.float32
BF16 = jnp.bfloat16
I32 = jnp.int32

D_MODEL = 2048
CHUNK = 64
EPS = 1e-6

GLA_HEADS = 4
GLA_DK = 256
GLA_DV = 512
GLA_RANK = 16
GLA_TAU = 16.0

DSA_HEADS = 16
DSA_DH = 128
DSA_DV = 128
DSA_LATENT = 256
IDX_HEADS = 16
IDX_DIM = 128
INDEX_TOPK = 256

REL_BUCKETS = 32
REL_MAX_DIST = 128

XA_HEADS = 4
XA_DH = 512

_SPLITS = (1024, 1024, 2048, 2048, 16, 2048, 256, 2048, 128, 16, 2048, 2048)
_OFFS = tuple(int(v) for v in np.cumsum((0,) + _SPLITS))
(_O_GQ, _O_GK, _O_GV, _O_GR, _O_GA, _O_DQ, _O_DC, _O_IQ, _O_IK, _O_IW, _O_GG, _O_GD, _O_END) = _OFFS

MAIN_COLS = 14336
SMALL_COLS = 512

QBLK = 128
KCH = 512
GROUP_KEYS = 32 * 128
PAD_FRONT = KCH
INT_MIN = -2 ** 31
NEG = -1e30
LOG2E = 1.4426950408889634

VMEM_LIMIT = 56 * 1024 * 1024


def _cparams(sem):
    return pltpu.CompilerParams(dimension_semantics=sem, vmem_limit_bytes=VMEM_LIMIT)


def _rms(x, gain):
    ms = jnp.mean(x * x, axis=-1, keepdims=True)
    return x * lax.rsqrt(ms + EPS) * gain


def _dot(a, b):
    return jnp.dot(a, b, preferred_element_type=F32)


def _dot_nt(a, b):
    return lax.dot_general(a, b, (((1,), (1,)), ((), ())), preferred_element_type=F32)


def _norm_matmul_kernel(x_ref, g_ref, w_ref, o_ref, u_ref):
    @pl.when(pl.program_id(1) == 0)
    def _():
        u_ref[...] = _rms(x_ref[...], g_ref[...]).astype(BF16)

    o_ref[...] = _dot(u_ref[...], w_ref[...]).astype(o_ref.dtype)


def _norm_matmul(x, gain, w, out_dtype, tm, tn):
    t, d = x.shape
    n = w.shape[1]
    return pl.pallas_call(
        _norm_matmul_kernel,
        grid=(t // tm, n // tn),
        in_specs=[pl.BlockSpec((tm, d), lambda i, j: (i, 0)),
                  pl.BlockSpec((1, d), lambda i, j: (0, 0)),
                  pl.BlockSpec((d, tn), lambda i, j: (0, j))],
        out_specs=pl.BlockSpec((tm, tn), lambda i, j: (i, j)),
        out_shape=jax.ShapeDtypeStruct((t, n), out_dtype),
        scratch_shapes=[pltpu.VMEM((tm, d), BF16)],
        compiler_params=_cparams(("parallel", "arbitrary")),
        name="norm_matmul",
    )(x, gain, w)


def _small_proj(x, gain, w_small, lat_g, ik_w, ik_b, batch, seq):
    tm = PAD_FRONT
    assert seq % tm == 0
    nblk = seq // tm
    npad = PAD_FRONT + seq
    grid = (batch, 1 + nblk)

    def x_map(b, i):
        return (b * nblk + jnp.maximum(i - 1, 0), 0)

    def kernel(x_ref, g_ref, w_ref, lat_g_ref, ik_w_ref, ik_b_ref, c_ref, k_ref, misc_ref):
        is_pad = pl.program_id(1) == 0

        @pl.when(is_pad)
        def _():
            c_ref[...] = jnp.zeros_like(c_ref)
            k_ref[...] = jnp.zeros_like(k_ref)
            misc_ref[...] = jnp.zeros_like(misc_ref)

        @pl.when(jnp.logical_not(is_pad))
        def _():
            u = _rms(x_ref[...], g_ref[...]).astype(BF16)
            p = _dot(u, w_ref[...])
            c_ref[0] = _rms(p[:, 0:256], lat_g_ref[...]).astype(BF16)
            ik = p[:, 256:384]
            mu = jnp.mean(ik, axis=-1, keepdims=True)
            xc = ik - mu
            var = jnp.mean(xc * xc, axis=-1, keepdims=True)
            k_ref[0] = (xc * lax.rsqrt(var + EPS) * ik_w_ref[...] + ik_b_ref[...]).astype(BF16)
            misc_ref[0] = p[:, 384:512]

    d = x.shape[1]
    const = lambda b, i: (0, 0)
    return pl.pallas_call(
        kernel,
        grid=grid,
        in_specs=[pl.BlockSpec((tm, d), x_map),
                  pl.BlockSpec((1, d), const),
                  pl.BlockSpec((d, SMALL_COLS), const),
                  pl.BlockSpec((1, DSA_LATENT), const),
                  pl.BlockSpec((1, IDX_DIM), const),
                  pl.BlockSpec((1, IDX_DIM), const)],
        out_specs=[pl.BlockSpec((1, tm, DSA_LATENT), lambda b, i: (b, i, 0)),
                   pl.BlockSpec((1, tm, IDX_DIM), lambda b, i: (b, i, 0)),
                   pl.BlockSpec((1, tm, 128), lambda b, i: (b, i, 0))],
        out_shape=[jax.ShapeDtypeStruct((batch, npad, DSA_LATENT), BF16),
                   jax.ShapeDtypeStruct((batch, npad, IDX_DIM), BF16),
                   jax.ShapeDtypeStruct((batch, npad, 128), F32)],
        compiler_params=_cparams(("parallel", "arbitrary")),
        name="small_proj",
    )(x, gain, w_small, lat_g, ik_w, ik_b)


def _log_sigmoid(z):
    return jnp.minimum(z, 0.0) - jnp.log1p(jnp.exp(-jnp.abs(z)))


def _split3(x):
    h = x.astype(BF16)
    r = x - h.astype(F32)
    m = r.astype(BF16)
    l = (r - m.astype(F32)).astype(BF16)
    return h, m, l


def _gla_kernel(nb, q_ref, k_ref, v_ref, r_ref, misc_ref, wa2_ref, ba_ref, on_ref, o_ref, state_ref):
    @pl.when(pl.program_id(0) == 0)
    def _():
        state_ref[...] = jnp.zeros_like(state_ref)

    rows = nb * CHUNK
    stack = lambda ref, cols: jnp.concatenate([ref[i][:, cols] for i in range(nb)], axis=0)
    row = lax.broadcasted_iota(I32, (rows, rows), 0)
    col = lax.broadcasted_iota(I32, (rows, rows), 1)
    same = (row // CHUNK) == (col // CHUNK)
    lower = jnp.logical_and(same, col <= row)
    upper = jnp.logical_and(same, col > row)
    tril = jnp.where(lower, 1.0, 0.0).astype(BF16)

    a_low = stack(misc_ref, slice(0, GLA_RANK)).astype(BF16)
    z = _dot(a_low, wa2_ref[...]) + ba_ref[...]
    la = _log_sigmoid(z) * (1.0 / GLA_TAU)
    l_h, l_m, l_l = _split3(la)
    b_all = _dot(tril, l_h) + _dot(tril, l_m) + _dot(tril, l_l)

    for h in range(GLA_HEADS):
        ks = slice(h * GLA_DK, (h + 1) * GLA_DK)
        vs = slice(h * GLA_DV, (h + 1) * GLA_DV)
        b = b_all[:, ks]
        eb = jnp.exp(b)
        ebi = jnp.exp(-b)
        q = stack(q_ref, ks).astype(F32) * (GLA_DK ** -0.5)
        k = stack(k_ref, ks).astype(F32)
        v = stack(v_ref, vs)
        q_fwd = (q * eb).astype(BF16)
        a_lo = _dot_nt(q_fwd, (k * ebi).astype(BF16))
        a_up = _dot_nt((q * ebi).astype(BF16), (k * eb).astype(BF16))
        scores = jnp.where(lower, a_lo, jnp.where(upper, a_up, 0.0)).astype(BF16)
        o_intra = _dot(scores, v)
        outs = []
        for i in range(nb):
            rs = slice(i * CHUNK, (i + 1) * CHUNK)
            b_i = b[rs]
            b_last = b_i[CHUNK - 1:CHUNK, :]
            st = state_ref[i * GLA_HEADS + h]
            outs.append(o_intra[rs] + _dot_nt(q_fwd[rs], st.astype(BF16)))
            k_dec = (k[rs] * jnp.exp(b_last - b_i)).astype(BF16)
            v_t = v[rs].astype(F32).T.astype(BF16)
            state_ref[i * GLA_HEADS + h] = st * jnp.exp(b_last) + _dot(v_t, k_dec)
        o = _rms(jnp.concatenate(outs, axis=0), on_ref[...])
        r = stack(r_ref, vs).astype(F32)
        y = (o * (r * jax.nn.sigmoid(r))).astype(o_ref.dtype)
        for i in range(nb):
            o_ref[i, :, vs] = y[i * CHUNK:(i + 1) * CHUNK]


def _gla(proj, misc, w_a2, b_a, out_norm, batch, seq):
    nc = seq // CHUNK
    proj3 = proj.reshape(batch, seq, proj.shape[1])
    y = pl.pallas_call(
        functools.partial(_gla_kernel, batch),
        grid=(nc,),
        in_specs=[pl.BlockSpec((batch, CHUNK, 1024), lambda c: (0, c, 0)),
                  pl.BlockSpec((batch, CHUNK, 1024), lambda c: (0, c, 1)),
                  pl.BlockSpec((batch, CHUNK, 2048), lambda c: (0, c, 1)),
                  pl.BlockSpec((batch, CHUNK, 2048), lambda c: (0, c, 2)),
                  pl.BlockSpec((batch, CHUNK, 128), lambda c: (0, c + PAD_FRONT // CHUNK, 0)),
                  pl.BlockSpec((GLA_RANK, GLA_HEADS * GLA_DK), lambda c: (0, 0)),
                  pl.BlockSpec((1, GLA_HEADS * GLA_DK), lambda c: (0, 0)),
                  pl.BlockSpec((1, GLA_DV), lambda c: (0, 0))],
        out_specs=pl.BlockSpec((batch, CHUNK, GLA_HEADS * GLA_DV), lambda c: (0, c, 0)),
        out_shape=jax.ShapeDtypeStruct((batch, seq, GLA_HEADS * GLA_DV), BF16),
        scratch_shapes=[pltpu.VMEM((batch * GLA_HEADS, GLA_DV, GLA_DK), F32)],
        compiler_params=_cparams(("arbitrary",)),
        name="gla",
    )(proj3, proj3, proj3, proj3, misc, w_a2, b_a, out_norm)
    return y.reshape(batch * seq, GLA_HEADS * GLA_DV)


def _t5_bucket(rel):
    half = REL_BUCKETS // 2
    max_exact = half // 2
    ret = jnp.where(rel > 0, half, 0)
    n = jnp.abs(rel)
    nf = jnp.maximum(n, 1).astype(jnp.float32)
    large = max_exact + (jnp.log(nf / max_exact) / math.log(REL_MAX_DIST / max_exact)
                         * (half - max_exact)).astype(jnp.int32)
    large = jnp.minimum(large, half - 1)
    return ret + jnp.where(n < max_exact, n, large)


def _bias_band_kernel(bucket_ref, rb_ref, o_ref):
    far = REL_BUCKETS // 2 - 1
    bucket = bucket_ref[...]
    for h in range(DSA_HEADS):
        acc = jnp.zeros(bucket.shape, F32)
        for b in range(REL_BUCKETS):
            acc = jnp.where(bucket == b, rb_ref[b, h], acc)
        o_ref[h] = (acc - rb_ref[far, h]) * LOG2E


def _bias_band(rel_bias):
    t = jnp.arange(QBLK, dtype=jnp.int32)[:, None]
    j = jnp.arange(2 * QBLK, dtype=jnp.int32)[None, :]
    bucket = _t5_bucket(j - QBLK - t).astype(jnp.int32)
    return pl.pallas_call(
        _bias_band_kernel,
        in_specs=[pl.BlockSpec(memory_space=pltpu.VMEM), pl.BlockSpec(memory_space=pltpu.SMEM)],
        out_specs=pl.BlockSpec(memory_space=pltpu.VMEM),
        out_shape=jax.ShapeDtypeStruct((DSA_HEADS, QBLK, 2 * QBLK), F32),
        name="bias_band",
    )(bucket, rel_bias)


def _transpose_bits32(words):
    a = list(words)
    j, m = 16, 0x0000FFFF
    while j:
        mask = jnp.int32(m - (1 << 32) if m >= (1 << 31) else m)
        k = 0
        while k < 32:
            t = (a[k] ^ lax.shift_right_logical(a[k + j], jnp.int32(j))) & mask
            a[k] = a[k] ^ t
            a[k + j] = a[k + j] ^ (t << j)
            k = (k + j + 1) & ~j
        j >>= 1
        m = (m ^ (m << j)) & 0xFFFFFFFF
    return a


def _sortable(x):
    i = pltpu.bitcast(x, I32)
    return jnp.where(i < 0, i ^ jnp.int32(0x7FFFFFFF), i)


def _dsa_kernel(top_k, dq_ref, iq_ref, misc_ref, kidx_ref, clat_ref, wuk_ref, wuv_ref, band_ref,
                o_ref, keys_ref, planes_ref, eq_ref, iqs_ref, wb_ref, qlat_ref, madd_ref, s_ref, s2_ref, p_ref, alpha_ref,
                m_ref, l_ref, acc_ref):
    qb = pl.program_id(1)
    start = qb * QBLK
    hrows = lambda h: slice(h * QBLK, (h + 1) * QBLK)

    w_scale = IDX_HEADS ** -0.5 * IDX_DIM ** -0.5
    wq = misc_ref[0][:, GLA_RANK:GLA_RANK + IDX_HEADS] * w_scale
    for h in range(IDX_HEADS):
        wb_ref[hrows(h), :] = jnp.broadcast_to(wq[:, h:h + 1], (QBLK, 128))
        iqs_ref[hrows(h), :] = iq_ref[:, h * IDX_DIM:(h + 1) * IDX_DIM]

    row = lax.broadcasted_iota(I32, (QBLK, KCH), 0)
    lane = lax.broadcasted_iota(I32, (QBLK, KCH), 1)
    p_lim = start + (row // CHUNK + 1) * CHUNK + PAD_FRONT

    n_chunks = (start + PAD_FRONT + QBLK + KCH - 1) // KCH
    keys_ref[:, 0:KCH] = jnp.full((QBLK, KCH), INT_MIN, I32)

    def idx_body(c, carry):
        off = pl.multiple_of(c * KCH, KCH)
        kc = kidx_ref[0, pl.ds(off, KCH), :]
        s_ref[...] = _dot_nt(iqs_ref[...], kc)
        acc = jnp.zeros((QBLK, KCH), F32)
        for h in range(IDX_HEADS):
            wbh = wb_ref[hrows(h), :]
            acc = acc + jnp.concatenate([wbh] * (KCH // 128), axis=1) * jnp.maximum(s_ref[hrows(h), :], 0.0)
        keys_ref[:, pl.ds(off, KCH)] = jnp.where(lane + off < p_lim, _sortable(acc), INT_MIN)
        return carry

    lax.fori_loop(1, n_chunks, idx_body, 0)

    n_groups = (n_chunks * KCH + GROUP_KEYS - 1) // GROUP_KEYS
    ngrp_max = eq_ref.shape[1] // 128

    def pad_body(c, carry):
        keys_ref[:, pl.ds(pl.multiple_of(c * KCH, KCH), KCH)] = jnp.full((QBLK, KCH), INT_MIN, I32)
        return carry

    lax.fori_loop(n_chunks, n_groups * (GROUP_KEYS // KCH), pad_body, 0)

    def plane_body(idx, carry):
        g = idx // (QBLK // 16)
        base = g * GROUP_KEYS
        for half in range(2):
            r0 = pl.multiple_of((idx % (QBLK // 16)) * 16 + half * 8, 8)
            words = [keys_ref[pl.ds(r0, 8), pl.ds(pl.multiple_of(base + j * 128, 128), 128)] ^ INT_MIN
                     for j in range(32)]
            words = _transpose_bits32(words)
            for i in range(32):
                planes_ref[31 - i, pl.ds(r0, 8), pl.ds(pl.multiple_of(g * 128, 128), 128)] = words[i]
        return carry

    lax.fori_loop(0, n_groups * (QBLK // 16), plane_body, 0)

    for g in range(ngrp_max):
        eq_ref[:, g * 128:(g + 1) * 128] = jnp.broadcast_to(jnp.where(g < n_groups, -1, 0), (QBLK, 128))

    def row_count(t):
        pc = lax.population_count(t)
        tot = pc[:, 0:128]
        for g in range(1, ngrp_max):
            tot = tot + pc[:, g * 128:(g + 1) * 128]
        return jnp.broadcast_to(jnp.sum(tot.astype(F32), axis=1, keepdims=True), (QBLK, 128))

    def pair_body(i, carry):
        prefix, above = carry
        b0 = 30 - 2 * i
        p1 = planes_ref[b0 + 1]
        p0 = planes_ref[b0]
        eq = eq_ref[...]
        e1 = eq & p1
        e0 = eq & ~p1
        t11 = e1 & p0
        t10 = e1 & ~p0
        t01 = e0 & p0
        t00 = e0 & ~p0
        s3 = above + row_count(t11)
        s2 = s3 + row_count(t10)
        s1 = s2 + row_count(t01)
        is3 = s3 >= top_k
        is2 = s2 >= top_k
        is1 = s1 >= top_k
        for g in range(ngrp_max):
            gs = slice(g * 128, (g + 1) * 128)
            eq_ref[:, gs] = jnp.where(is3, t11[:, gs], jnp.where(is2, t10[:, gs], jnp.where(is1, t01[:, gs], t00[:, gs])))
        above = jnp.where(is3, above, jnp.where(is2, s3, jnp.where(is1, s2, s1)))
        digit = jnp.where(is3, 3, jnp.where(is2, 2, jnp.where(is1, 1, 0)))
        return prefix | (digit << b0), above

    prefix, _ = lax.fori_loop(0, 16, pair_body,
                              (jnp.zeros((QBLK, 128), I32), jnp.zeros((QBLK, 128), F32)))
    thr = prefix ^ INT_MIN
    thr = jnp.maximum(thr, INT_MIN + 1)

    for h in range(DSA_HEADS):
        ql = _dot(dq_ref[:, h * DSA_DH:(h + 1) * DSA_DH], wuk_ref[h]) * (LOG2E * DSA_DH ** -0.5)
        qlat_ref[hrows(h), :] = ql.astype(BF16)

    m_ref[...] = jnp.full(m_ref.shape, NEG, F32)
    l_ref[...] = jnp.zeros(l_ref.shape, F32)
    acc_ref[...] = jnp.zeros(acc_ref.shape, F32)

    end = start + PAD_FRONT + QBLK
    n_att = (start + QBLK + KCH - 1) // KCH

    def chunk_off(j):
        return pl.multiple_of(end - KCH * (j + 1), 128)

    def scores(j, dst_ref):
        cc = clat_ref[0, pl.ds(chunk_off(j), KCH), :]
        dst_ref[...] = _dot_nt(qlat_ref[...], cc)

    def softmax_update(j, src_ref, near):
        off = chunk_off(j)
        kk = keys_ref[:, pl.ds(off, KCH)]
        madd_ref[...] = jnp.where(kk >= jnp.concatenate([thr] * (KCH // 128), axis=1), 0.0, NEG)
        cc = clat_ref[0, pl.ds(off, KCH), :]
        for h in range(DSA_HEADS):
            s = src_ref[hrows(h), :] + madd_ref[...]
            if near:
                s = jnp.concatenate([s[:, :KCH - 2 * QBLK], s[:, KCH - 2 * QBLK:] + band_ref[h]], axis=1)
            m_old = m_ref[hrows(h), :]
            m_new = jnp.maximum(m_old, jnp.broadcast_to(jnp.max(s, axis=1, keepdims=True), (QBLK, 128)))
            alpha = jnp.exp2(m_old - m_new)
            p = jnp.exp2(s - jnp.concatenate([m_new] * (KCH // 128), axis=1))
            l_ref[hrows(h), :] = (alpha * l_ref[hrows(h), :]
                                  + jnp.broadcast_to(jnp.sum(p, axis=1, keepdims=True), (QBLK, 128)))
            m_ref[hrows(h), :] = m_new
            alpha_ref[hrows(h), :] = alpha
            p_ref[hrows(h), :] = p.astype(BF16)
        al = alpha_ref[...]
        acc_ref[...] = (jnp.concatenate([al] * (DSA_LATENT // 128), axis=1) * acc_ref[...]
                        + _dot(p_ref[...], cc))

    def step(j, cur_ref, nxt_ref, near):
        scores(jnp.minimum(j + 1, n_att - 1), nxt_ref)
        softmax_update(j, cur_ref, near)

    scores(0, s_ref)
    step(0, s_ref, s2_ref, True)

    def att_body(j, carry):
        @pl.when(j % 2 == 1)
        def _():
            step(j, s2_ref, s_ref, False)

        @pl.when(j % 2 == 0)
        def _():
            step(j, s_ref, s2_ref, False)

        return carry

    lax.fori_loop(1, n_att, att_body, 0)

    for h in range(DSA_HEADS):
        inv = 1.0 / l_ref[hrows(h), :]
        o_lat = acc_ref[hrows(h), :] * jnp.concatenate([inv] * (DSA_LATENT // 128), axis=1)
        o_ref[:, h * DSA_DV:(h + 1) * DSA_DV] = _dot(o_lat.astype(BF16), wuv_ref[h]).astype(o_ref.dtype)


def _dsa(proj, misc, kidx, clat, w_uk, w_uv, band, batch, seq):
    nqb = seq // QBLK
    t = batch * seq
    npad = clat.shape[1]
    top_k = min(INDEX_TOPK, seq // 4)
    hq = DSA_HEADS * QBLK
    tokb = lambda b, i: b * nqb + i
    const3 = lambda b, i: (0, 0, 0)
    ngrp = (npad + GROUP_KEYS - 1) // GROUP_KEYS
    once = pl.Buffered(1)
    return pl.pallas_call(
        functools.partial(_dsa_kernel, top_k),
        grid=(batch, nqb),
        in_specs=[pl.BlockSpec((QBLK, 2048), lambda b, i: (tokb(b, i), 3)),
                  pl.BlockSpec((QBLK, 2048), lambda b, i: (tokb(b, i), 4)),
                  pl.BlockSpec((1, QBLK, 128), lambda b, i: (b, i + PAD_FRONT // QBLK, 0)),
                  pl.BlockSpec((1, npad, IDX_DIM), lambda b, i: (b, 0, 0), pipeline_mode=once),
                  pl.BlockSpec((1, npad, DSA_LATENT), lambda b, i: (b, 0, 0), pipeline_mode=once),
                  pl.BlockSpec((DSA_HEADS, DSA_DH, DSA_LATENT), const3, pipeline_mode=once),
                  pl.BlockSpec((DSA_HEADS, DSA_LATENT, DSA_DV), const3, pipeline_mode=once),
                  pl.BlockSpec((DSA_HEADS, QBLK, 2 * QBLK), const3, pipeline_mode=once)],
        out_specs=pl.BlockSpec((QBLK, DSA_HEADS * DSA_DV), lambda b, i: (tokb(b, i), 0)),
        out_shape=jax.ShapeDtypeStruct((t, DSA_HEADS * DSA_DV), BF16),
        scratch_shapes=[pltpu.VMEM((QBLK, ngrp * GROUP_KEYS), I32),
                        pltpu.VMEM((32, QBLK, ngrp * 128), I32),
                        pltpu.VMEM((QBLK, ngrp * 128), I32),
                        pltpu.VMEM((hq, IDX_DIM), BF16),
                        pltpu.VMEM((hq, 128), F32),
                        pltpu.VMEM((hq, DSA_LATENT), BF16),
                        pltpu.VMEM((QBLK, KCH), F32),
                        pltpu.VMEM((hq, KCH), F32),
                        pltpu.VMEM((hq, KCH), F32),
                        pltpu.VMEM((hq, KCH), BF16),
                        pltpu.VMEM((hq, 128), F32),
                        pltpu.VMEM((hq, 128), F32),
                        pltpu.VMEM((hq, 128), F32),
                        pltpu.VMEM((hq, DSA_LATENT), F32)],
        compiler_params=_cparams(("parallel", "arbitrary")),
        name="dsa",
    )(proj, proj, misc, kidx, clat, w_uk, w_uv, band)


def _merge_kernel(yg_ref, yd_ref, gg_ref, gd_ref, wg_ref, wd_ref, o_ref):
    a = _dot(yg_ref[...], wg_ref[...])
    b = _dot(yd_ref[...], wd_ref[...])
    gg = jax.nn.sigmoid(gg_ref[...].astype(F32))
    gd = jax.nn.sigmoid(gd_ref[...].astype(F32))
    o_ref[...] = (gg * a + gd * b).astype(o_ref.dtype)


def _merge(y_gla, y_dsa, proj, w_g, w_d, tm, tn):
    t, d = y_gla.shape
    n = w_g.shape[1]
    ncb = n // tn
    return pl.pallas_call(
        _merge_kernel,
        grid=(t // tm, ncb),
        in_specs=[pl.BlockSpec((tm, d), lambda i, j: (i, 0)),
                  pl.BlockSpec((tm, d), lambda i, j: (i, 0)),
                  pl.BlockSpec((tm, tn), lambda i, j: (i, 5 * ncb + j)),
                  pl.BlockSpec((tm, tn), lambda i, j: (i, 6 * ncb + j)),
                  pl.BlockSpec((d, tn), lambda i, j: (0, j)),
                  pl.BlockSpec((d, tn), lambda i, j: (0, j))],
        out_specs=pl.BlockSpec((tm, tn), lambda i, j: (i, j)),
        out_shape=jax.ShapeDtypeStruct((t, n), BF16),
        compiler_params=_cparams(("parallel", "arbitrary")),
        name="merge",
    )(y_gla, y_dsa, proj, proj, w_g, w_d)


def _proj_norm_res_kernel(y_ref, w_ref, g_ref, h_ref, o_ref):
    y = _dot(y_ref[...], w_ref[...])
    o_ref[...] = h_ref[...] + _rms(y, g_ref[...])


def _proj_norm_res(y, w, gain, h, tm):
    t, d = y.shape
    n = w.shape[1]
    return pl.pallas_call(
        _proj_norm_res_kernel,
        grid=(t // tm,),
        in_specs=[pl.BlockSpec((tm, d), lambda i: (i, 0)),
                  pl.BlockSpec((d, n), lambda i: (0, 0)),
                  pl.BlockSpec((1, n), lambda i: (0, 0)),
                  pl.BlockSpec((tm, n), lambda i: (i, 0))],
        out_specs=pl.BlockSpec((tm, n), lambda i: (i, 0)),
        out_shape=jax.ShapeDtypeStruct((t, n), F32),
        compiler_params=_cparams(("parallel",)),
        name="proj_norm_res",
    )(y, w, gain, h)


def _xattn_kernel(q_ref, k_ref, v_ref, o_ref):
    for h in range(XA_HEADS):
        hs = slice(h * XA_DH, (h + 1) * XA_DH)
        s = _dot_nt(q_ref[:, hs], k_ref[0][:, hs]) * (XA_DH ** -0.5)
        m = jnp.max(s, axis=-1, keepdims=True)
        p = jnp.exp(s - m)
        p = p / jnp.sum(p, axis=-1, keepdims=True)
        o_ref[:, hs] = _dot(p.astype(BF16), v_ref[0][:, hs]).astype(o_ref.dtype)


def _xattn(q, kv, batch, seq, tm):
    t, d = q.shape
    n_mem = kv.shape[1]
    nb = seq // tm
    return pl.pallas_call(
        _xattn_kernel,
        grid=(batch, nb),
        in_specs=[pl.BlockSpec((tm, d), lambda b, i: (b * nb + i, 0)),
                  pl.BlockSpec((1, n_mem, d), lambda b, i: (b, 0, 0)),
                  pl.BlockSpec((1, n_mem, d), lambda b, i: (b, 0, 1))],
        out_specs=pl.BlockSpec((tm, d), lambda b, i: (b * nb + i, 0)),
        out_shape=jax.ShapeDtypeStruct((t, d), BF16),
        compiler_params=_cparams(("parallel", "parallel")),
        name="xattn",
    )(q, kv, kv)


def _ffn_kernel(h_ref, g_ref, wg_ref, wu_ref, wd_ref, pg_ref, o_ref, u_ref, acc_ref):
    f = pl.program_id(1)

    @pl.when(f == 0)
    def _():
        u_ref[...] = _rms(h_ref[...], g_ref[...]).astype(BF16)
        acc_ref[...] = jnp.zeros_like(acc_ref)

    u = u_ref[...]
    a = _dot(u, wg_ref[...])
    b = _dot(u, wu_ref[...])
    act = (a * jax.nn.sigmoid(a) * b).astype(BF16)
    acc_ref[...] += _dot(act, wd_ref[...])

    @pl.when(f == pl.num_programs(1) - 1)
    def _():
        o_ref[...] = h_ref[...] + _rms(acc_ref[...], pg_ref[...])


def _ffn(h, pre_gain, w_gate, w_up, w_down, post_gain, tm, tf):
    t, d = h.shape
    ff = w_gate.shape[1]
    return pl.pallas_call(
        _ffn_kernel,
        grid=(t // tm, ff // tf),
        in_specs=[pl.BlockSpec((tm, d), lambda i, f: (i, 0)),
                  pl.BlockSpec((1, d), lambda i, f: (0, 0)),
                  pl.BlockSpec((d, tf), lambda i, f: (0, f)),
                  pl.BlockSpec((d, tf), lambda i, f: (0, f)),
                  pl.BlockSpec((tf, d), lambda i, f: (f, 0)),
                  pl.BlockSpec((1, d), lambda i, f: (0, 0))],
        out_specs=pl.BlockSpec((tm, d), lambda i, f: (i, 0)),
        out_shape=jax.ShapeDtypeStruct((t, d), F32),
        scratch_shapes=[pltpu.VMEM((tm, d), BF16), pltpu.VMEM((tm, d), F32)],
        compiler_params=_cparams(("parallel", "arbitrary")),
        name="ffn",
    )(h, pre_gain, w_gate, w_up, w_down, post_gain)


def _row(v):
    return v.reshape(1, -1).astype(F32)


def _layer(h, mem, w_in, gla_w_a2, gla_b_a, gla_out_norm, dsa_w_uk, dsa_w_uv, dsa_latent_norm,
           idx_k_norm_w, idx_k_norm_b, band, w_gla_branch, w_dsa_branch, w_mix_out,
           mix_pre_norm, mix_post_norm, xa_pre_norm, xa_post_norm, xa_mem_norm,
           w_xa_q, w_xa_kv, w_xa_o, ffn_pre_norm, ffn_post_norm, w_ffn_gate, w_ffn_up, w_ffn_down,
           batch, seq):
    d = D_MODEL
    t = batch * seq
    cols = lambda a, b: w_in[:, a:b]
    w_main = jnp.concatenate(
        [cols(_O_GQ, _O_GA), cols(_O_DQ, _O_DC), cols(_O_IQ, _O_IK), cols(_O_GG, _O_END)], axis=1).astype(BF16)
    w_small = jnp.concatenate(
        [cols(_O_DC, _O_IQ), cols(_O_IK, _O_IW), cols(_O_GA, _O_DQ), cols(_O_IW, _O_GG),
         jnp.zeros((d, SMALL_COLS - 416), w_in.dtype)], axis=1).astype(BF16)

    pre = _row(mix_pre_norm)
    proj = _norm_matmul(h, pre, w_main, BF16, 1024, 512)
    clat, kidx, misc = _small_proj(h, pre, w_small, _row(dsa_latent_norm), _row(idx_k_norm_w),
                                   _row(idx_k_norm_b), batch, seq)

    y_gla = _gla(proj, misc, gla_w_a2.astype(BF16), _row(gla_b_a), _row(gla_out_norm), batch, seq)
    y_dsa = _dsa(proj, misc, kidx, clat, dsa_w_uk.astype(BF16), dsa_w_uv.astype(BF16), band, batch, seq)

    merged = _merge(y_gla, y_dsa, proj, w_gla_branch.astype(BF16), w_dsa_branch.astype(BF16), 1024, 512)
    h = _proj_norm_res(merged, w_mix_out.astype(BF16), _row(mix_post_norm), h, 512)

    n_mem = mem.shape[1]
    kv = _norm_matmul(mem.reshape(batch * n_mem, d), _row(xa_mem_norm), w_xa_kv.astype(BF16), BF16,
                      batch * n_mem, 512).reshape(batch, n_mem, 2 * d)
    q = _norm_matmul(h, _row(xa_pre_norm), w_xa_q.astype(BF16), BF16, 1024, 512)
    o = _xattn(q, kv, batch, seq, 512)
    h = _proj_norm_res(o, w_xa_o.astype(BF16), _row(xa_post_norm), h, 512)

    h = _ffn(h, _row(ffn_pre_norm), w_ffn_gate.astype(BF16), w_ffn_up.astype(BF16),
             w_ffn_down.astype(BF16), _row(ffn_post_norm), 512, 512)
    return h


def kernel(x, mem, w_in, gla_w_a2, gla_b_a, gla_out_norm, dsa_w_uk, dsa_w_uv, dsa_latent_norm,
           idx_k_norm_w, idx_k_norm_b, rel_bias, w_gla_branch, w_dsa_branch, w_mix_out,
           mix_pre_norm, mix_post_norm, xa_pre_norm, xa_post_norm, xa_mem_norm,
           w_xa_q, w_xa_kv, w_xa_o, ffn_pre_norm, ffn_post_norm, w_ffn_gate, w_ffn_up, w_ffn_down):
    batch, seq, d = x.shape
    depth = w_in.shape[0]
    band = _bias_band(rel_bias.astype(F32))
    h = x.reshape(batch * seq, d)
    for l in range(depth):
        h = _layer(h, mem, w_in[l], gla_w_a2[l], gla_b_a[l], gla_out_norm[l], dsa_w_uk[l], dsa_w_uv[l],
                   dsa_latent_norm[l], idx_k_norm_w[l], idx_k_norm_b[l], band, w_gla_branch[l],
                   w_dsa_branch[l], w_mix_out[l], mix_pre_norm[l], mix_post_norm[l], xa_pre_norm[l],
                   xa_post_norm[l], xa_mem_norm[l], w_xa_q[l], w_xa_kv[l], w_xa_o[l], ffn_pre_norm[l],
                   ffn_post_norm[l], w_ffn_gate[l], w_ffn_up[l], w_ffn_down[l], batch, seq)
    return h.reshape(batch, seq, d)
```

```python
import functools
import math

import jax
import jax.numpy as jnp
import numpy as np
from jax import lax
from jax.experimental import pallas as pl
from jax.experimental.pallas import tpu as pltpu

F32 = jnp.float32
BF16 = jnp.bfloat16
I32 = jnp.int32

D_MODEL = 2048
CHUNK = 64
EPS = 1e-6

GLA_HEADS = 4
GLA_DK = 256
GLA_DV = 512
GLA_RANK = 16
GLA_TAU = 16.0

DSA_HEADS = 16
DSA_DH = 128
DSA_DV = 128
DSA_LATENT = 256
IDX_HEADS = 16
IDX_DIM = 128
INDEX_TOPK = 256

REL_BUCKETS = 32
REL_MAX_DIST = 128

XA_HEADS = 4
XA_DH = 512

_SPLITS = (1024, 1024, 2048, 2048, 16, 2048, 256, 2048, 128, 16, 2048, 2048)
_OFFS = tuple(int(v) for v in np.cumsum((0,) + _SPLITS))
(_O_GQ, _O_GK, _O_GV, _O_GR, _O_GA, _O_DQ, _O_DC, _O_IQ, _O_IK, _O_IW, _O_GG, _O_GD, _O_END) = _OFFS

MAIN_COLS = 14336
SMALL_COLS = 512

QBLK = 128
KCH = 512
GROUP_KEYS = 32 * 128
PAD_FRONT = KCH
INT_MIN = -2 ** 31
NEG = -1e30
LOG2E = 1.4426950408889634

VMEM_LIMIT = 56 * 1024 * 1024


def _cparams(sem):
    return pltpu.CompilerParams(dimension_semantics=sem, vmem_limit_bytes=VMEM_LIMIT)


def _rms(x, gain):
    ms = jnp.mean(x * x, axis=-1, keepdims=True)
    return x * lax.rsqrt(ms + EPS) * gain


def _dot(a, b):
    return jnp.dot(a, b, preferred_element_type=F32)


def _dot_nt(a, b):
    return lax.dot_general(a, b, (((1,), (1,)), ((), ())), preferred_element_type=F32)


def _norm_matmul_kernel(x_ref, g_ref, w_ref, o_ref, u_ref):
    @pl.when(pl.program_id(1) == 0)
    def _():
        u_ref[...] = _rms(x_ref[...], g_ref[...]).astype(BF16)

    o_ref[...] = _dot(u_ref[...], w_ref[...]).astype(o_ref.dtype)


def _norm_matmul(x, gain, w, out_dtype, tm, tn):
    t, d = x.shape
    n = w.shape[1]
    return pl.pallas_call(
        _norm_matmul_kernel,
        grid=(t // tm, n // tn),
        in_specs=[pl.BlockSpec((tm, d), lambda i, j: (i, 0)),
                  pl.BlockSpec((1, d), lambda i, j: (0, 0)),
                  pl.BlockSpec((d, tn), lambda i, j: (0, j))],
        out_specs=pl.BlockSpec((tm, tn), lambda i, j: (i, j)),
        out_shape=jax.ShapeDtypeStruct((t, n), out_dtype),
        scratch_shapes=[pltpu.VMEM((tm, d), BF16)],
        compiler_params=_cparams(("parallel", "arbitrary")),
        name="norm_matmul",
    )(x, gain, w)


def _small_proj(x, gain, w_small, lat_g, ik_w, ik_b, batch, seq):
    tm = PAD_FRONT
    assert seq % tm == 0
    nblk = seq // tm
    npad = PAD_FRONT + seq
    grid = (batch, 1 + nblk)

    def x_map(b, i):
        return (b * nblk + jnp.maximum(i - 1, 0), 0)

    def kernel(x_ref, g_ref, w_ref, lat_g_ref, ik_w_ref, ik_b_ref, c_ref, k_ref, misc_ref):
        is_pad = pl.program_id(1) == 0

        @pl.when(is_pad)
        def _():
            c_ref[...] = jnp.zeros_like(c_ref)
            k_ref[...] = jnp.zeros_like(k_ref)
            misc_ref[...] = jnp.zeros_like(misc_ref)

        @pl.when(jnp.logical_not(is_pad))
        def _():
            u = _rms(x_ref[...], g_ref[...]).astype(BF16)
            p = _dot(u, w_ref[...])
            c_ref[0] = _rms(p[:, 0:256], lat_g_ref[...]).astype(BF16)
            ik = p[:, 256:384]
            mu = jnp.mean(ik, axis=-1, keepdims=True)
            xc = ik - mu
            var = jnp.mean(xc * xc, axis=-1, keepdims=True)
            k_ref[0] = (xc * lax.rsqrt(var + EPS) * ik_w_ref[...] + ik_b_ref[...]).astype(BF16)
            misc_ref[0] = p[:, 384:512]

    d = x.shape[1]
    const = lambda b, i: (0, 0)
    return pl.pallas_call(
        kernel,
        grid=grid,
        in_specs=[pl.BlockSpec((tm, d), x_map),
                  pl.BlockSpec((1, d), const),
                  pl.BlockSpec((d, SMALL_COLS), const),
                  pl.BlockSpec((1, DSA_LATENT), const),
                  pl.BlockSpec((1, IDX_DIM), const),
                  pl.BlockSpec((1, IDX_DIM), const)],
        out_specs=[pl.BlockSpec((1, tm, DSA_LATENT), lambda b, i: (b, i, 0)),
                   pl.BlockSpec((1, tm, IDX_DIM), lambda b, i: (b, i, 0)),
                   pl.BlockSpec((1, tm, 128), lambda b, i: (b, i, 0))],
        out_shape=[jax.ShapeDtypeStruct((batch, npad, DSA_LATENT), BF16),
                   jax.ShapeDtypeStruct((batch, npad, IDX_DIM), BF16),
                   jax.ShapeDtypeStruct((batch, npad, 128), F32)],
        compiler_params=_cparams(("parallel", "arbitrary")),
        name="small_proj",
    )(x, gain, w_small, lat_g, ik_w, ik_b)


def _log_sigmoid(z):
    return jnp.minimum(z, 0.0) - jnp.log1p(jnp.exp(-jnp.abs(z)))


def _split3(x):
    h = x.astype(BF16)
    r = x - h.astype(F32)
    m = r.astype(BF16)
    l = (r - m.astype(F32)).astype(BF16)
    return h, m, l


def _gla_kernel(nb, q_ref, k_ref, v_ref, r_ref, misc_ref, wa2_ref, ba_ref, on_ref, o_ref, state_ref):
    @pl.when(pl.program_id(0) == 0)
    def _():
        state_ref[...] = jnp.zeros_like(state_ref)

    rows = nb * CHUNK
    stack = lambda ref, cols: jnp.concatenate([ref[i][:, cols] for i in range(nb)], axis=0)
    row = lax.broadcasted_iota(I32, (rows, rows), 0)
    col = lax.broadcasted_iota(I32, (rows, rows), 1)
    same = (row // CHUNK) == (col // CHUNK)
    lower = jnp.logical_and(same, col <= row)
    upper = jnp.logical_and(same, col > row)
    tril = jnp.where(lower, 1.0, 0.0).astype(BF16)

    a_low = stack(misc_ref, slice(0, GLA_RANK)).astype(BF16)
    z = _dot(a_low, wa2_ref[...]) + ba_ref[...]
    la = _log_sigmoid(z) * (1.0 / GLA_TAU)
    l_h, l_m, l_l = _split3(la)
    b_all = _dot(tril, l_h) + _dot(tril, l_m) + _dot(tril, l_l)

    for h in range(GLA_HEADS):
        ks = slice(h * GLA_DK, (h + 1) * GLA_DK)
        vs = slice(h * GLA_DV, (h + 1) * GLA_DV)
        b = b_all[:, ks]
        eb = jnp.exp(b)
        ebi = jnp.exp(-b)
        q = stack(q_ref, ks).astype(F32) * (GLA_DK ** -0.5)
        k = stack(k_ref, ks).astype(F32)
        v = stack(v_ref, vs)
        q_fwd = (q * eb).astype(BF16)
        a_lo = _dot_nt(q_fwd, (k * ebi).astype(BF16))
        a_up = _dot_nt((q * ebi).astype(BF16), (k * eb).astype(BF16))
        scores = jnp.where(lower, a_lo, jnp.where(upper, a_up, 0.0)).astype(BF16)
        o_intra = _dot(scores, v)
        outs = []
        for i in range(nb):
            rs = slice(i * CHUNK, (i + 1) * CHUNK)
            b_i = b[rs]
            b_last = b_i[CHUNK - 1:CHUNK, :]
            st = state_ref[i * GLA_HEADS + h]
            outs.append(o_intra[rs] + _dot_nt(q_fwd[rs], st.astype(BF16)))
            k_dec = (k[rs] * jnp.exp(b_last - b_i)).astype(BF16)
            v_t = v[rs].astype(F32).T.astype(BF16)
            state_ref[i * GLA_HEADS + h] = st * jnp.exp(b_last) + _dot(v_t, k_dec)
        o = _rms(jnp.concatenate(outs, axis=0), on_ref[...])
        r = stack(r_ref, vs).astype(F32)
        y = (o * (r * jax.nn.sigmoid(r))).astype(o_ref.dtype)
        for i in range(nb):
            o_ref[i, :, vs] = y[i * CHUNK:(i + 1) * CHUNK]


def _gla(proj, misc, w_a2, b_a, out_norm, batch, seq):
    nc = seq // CHUNK
    proj3 = proj.reshape(batch, seq, proj.shape[1])
    y = pl.pallas_call(
        functools.partial(_gla_kernel, batch),
        grid=(nc,),
        in_specs=[pl.BlockSpec((batch, CHUNK, 1024), lambda c: (0, c, 0)),
                  pl.BlockSpec((batch, CHUNK, 1024), lambda c: (0, c, 1)),
                  pl.BlockSpec((batch, CHUNK, 2048), lambda c: (0, c, 1)),
                  pl.BlockSpec((batch, CHUNK, 2048), lambda c: (0, c, 2)),
                  pl.BlockSpec((batch, CHUNK, 128), lambda c: (0, c + PAD_FRONT // CHUNK, 0)),
                  pl.BlockSpec((GLA_RANK, GLA_HEADS * GLA_DK), lambda c: (0, 0)),
                  pl.BlockSpec((1, GLA_HEADS * GLA_DK), lambda c: (0, 0)),
                  pl.BlockSpec((1, GLA_DV), lambda c: (0, 0))],
        out_specs=pl.BlockSpec((batch, CHUNK, GLA_HEADS * GLA_DV), lambda c: (0, c, 0)),
        out_shape=jax.ShapeDtypeStruct((batch, seq, GLA_HEADS * GLA_DV), BF16),
        scratch_shapes=[pltpu.VMEM((batch * GLA_HEADS, GLA_DV, GLA_DK), F32)],
        compiler_params=_cparams(("arbitrary",)),
        name="gla",
    )(proj3, proj3, proj3, proj3, misc, w_a2, b_a, out_norm)
    return y.reshape(batch * seq, GLA_HEADS * GLA_DV)


def _t5_bucket(rel):
    half = REL_BUCKETS // 2
    max_exact = half // 2
    ret = jnp.where(rel > 0, half, 0)
    n = jnp.abs(rel)
    nf = jnp.maximum(n, 1).astype(jnp.float32)
    large = max_exact + (jnp.log(nf / max_exact) / math.log(REL_MAX_DIST / max_exact)
                         * (half - max_exact)).astype(jnp.int32)
    large = jnp.minimum(large, half - 1)
    return ret + jnp.where(n < max_exact, n, large)


def _bias_band_kernel(bucket_ref, rb_ref, o_ref):
    far = REL_BUCKETS // 2 - 1
    bucket = bucket_ref[...]
    for h in range(DSA_HEADS):
        acc = jnp.zeros(bucket.shape, F32)
        for b in range(REL_BUCKETS):
            acc = jnp.where(bucket == b, rb_ref[b, h], acc)
        o_ref[h] = (acc - rb_ref[far, h]) * LOG2E


def _bias_band(rel_bias):
    t = jnp.arange(QBLK, dtype=jnp.int32)[:, None]
    j = jnp.arange(2 * QBLK, dtype=jnp.int32)[None, :]
    bucket = _t5_bucket(j - QBLK - t).astype(jnp.int32)
    return pl.pallas_call(
        _bias_band_kernel,
        in_specs=[pl.BlockSpec(memory_space=pltpu.VMEM), pl.BlockSpec(memory_space=pltpu.SMEM)],
        out_specs=pl.BlockSpec(memory_space=pltpu.VMEM),
        out_shape=jax.ShapeDtypeStruct((DSA_HEADS, QBLK, 2 * QBLK), F32),
        name="bias_band",
    )(bucket, rel_bias)


def _transpose_bits32(words):
    a = list(words)
    j, m = 16, 0x0000FFFF
    while j:
        mask = jnp.int32(m - (1 << 32) if m >= (1 << 31) else m)
        k = 0
        while k < 32:
            t = (a[k] ^ lax.shift_right_logical(a[k + j], jnp.int32(j))) & mask
            a[k] = a[k] ^ t
            a[k + j] = a[k + j] ^ (t << j)
            k = (k + j + 1) & ~j
        j >>= 1
        m = (m ^ (m << j)) & 0xFFFFFFFF
    return a


def _sortable(x):
    i = pltpu.bitcast(x, I32)
    return jnp.where(i < 0, i ^ jnp.int32(0x7FFFFFFF), i)


def _dsa_kernel(top_k, dq_ref, iq_ref, misc_ref, kidx_ref, clat_ref, wuk_ref, wuv_ref, band_ref,
                o_ref, keys_ref, planes_ref, eq_ref, iqs_ref, wb_ref, qlat_ref, madd_ref, s_ref, s2_ref, p_ref, alpha_ref,
                m_ref, l_ref, acc_ref):
    qb = pl.program_id(1)
    start = qb * QBLK
    hrows = lambda h: slice(h * QBLK, (h + 1) * QBLK)

    w_scale = IDX_HEADS ** -0.5 * IDX_DIM ** -0.5
    wq = misc_ref[0][:, GLA_RANK:GLA_RANK + IDX_HEADS] * w_scale
    for h in range(IDX_HEADS):
        wb_ref[hrows(h), :] = jnp.broadcast_to(wq[:, h:h + 1], (QBLK, 128))
        iqs_ref[hrows(h), :] = iq_ref[:, h * IDX_DIM:(h + 1) * IDX_DIM]

    row = lax.broadcasted_iota(I32, (QBLK, KCH), 0)
    lane = lax.broadcasted_iota(I32, (QBLK, KCH), 1)
    p_lim = start + (row // CHUNK + 1) * CHUNK + PAD_FRONT

    n_chunks = (start + PAD_FRONT + QBLK + KCH - 1) // KCH
    keys_ref[:, 0:KCH] = jnp.full((QBLK, KCH), INT_MIN, I32)

    def idx_body(c, carry):
        off = pl.multiple_of(c * KCH, KCH)
        kc = kidx_ref[0, pl.ds(off, KCH), :]
        s_ref[...] = _dot_nt(iqs_ref[...], kc)
        acc = jnp.zeros((QBLK, KCH), F32)
        for h in range(IDX_HEADS):
            wbh = wb_ref[hrows(h), :]
            acc = acc + jnp.concatenate([wbh] * (KCH // 128), axis=1) * jnp.maximum(s_ref[hrows(h), :], 0.0)
        keys_ref[:, pl.ds(off, KCH)] = jnp.where(lane + off < p_lim, _sortable(acc), INT_MIN)
        return carry

    lax.fori_loop(1, n_chunks, idx_body, 0)

    n_groups = (n_chunks * KCH + GROUP_KEYS - 1) // GROUP_KEYS
    ngrp_max = eq_ref.shape[1] // 128

    def pad_body(c, carry):
        keys_ref[:, pl.ds(pl.multiple_of(c * KCH, KCH), KCH)] = jnp.full((QBLK, KCH), INT_MIN, I32)
        return carry

    lax.fori_loop(n_chunks, n_groups * (GROUP_KEYS // KCH), pad_body, 0)

    def plane_body(idx, carry):
        g = idx // (QBLK // 16)
        base = g * GROUP_KEYS
        for half in range(2):
            r0 = pl.multiple_of((idx % (QBLK // 16)) * 16 + half * 8, 8)
            words = [keys_ref[pl.ds(r0, 8), pl.ds(pl.multiple_of(base + j * 128, 128), 128)]
                     for j in range(32)]
            words = _transpose_bits32(words)
            words[0] = ~words[0]
            for i in range(32):
                planes_ref[31 - i, pl.ds(r0, 8), pl.ds(pl.multiple_of(g * 128, 128), 128)] = words[i]
        return carry

    lax.fori_loop(0, n_groups * (QBLK // 16), plane_body, 0)

    for g in range(ngrp_max):
        eq_ref[:, g * 128:(g + 1) * 128] = jnp.broadcast_to(jnp.where(g < n_groups, -1, 0), (QBLK, 128))

    def row_count(t):
        pc = lax.population_count(t)
        tot = pc[:, 0:128]
        for g in range(1, ngrp_max):
            tot = tot + pc[:, g * 128:(g + 1) * 128]
        return jnp.broadcast_to(jnp.sum(tot.astype(F32), axis=1, keepdims=True), (QBLK, 128))

    def pair_body(i, carry):
        prefix, above = carry
        b0 = 30 - 2 * i
        p1 = planes_ref[b0 + 1]
        p0 = planes_ref[b0]
        eq = eq_ref[...]
        e1 = eq & p1
        e0 = eq & ~p1
        t11 = e1 & p0
        t10 = e1 & ~p0
        t01 = e0 & p0
        t00 = e0 & ~p0
        s3 = above + row_count(t11)
        s2 = s3 + row_count(t10)
        s1 = s2 + row_count(t01)
        is3 = s3 >= top_k
        is2 = s2 >= top_k
        is1 = s1 >= top_k
        for g in range(ngrp_max):
            gs = slice(g * 128, (g + 1) * 128)
            eq_ref[:, gs] = jnp.where(is3, t11[:, gs], jnp.where(is2, t10[:, gs], jnp.where(is1, t01[:, gs], t00[:, gs])))
        above = jnp.where(is3, above, jnp.where(is2, s3, jnp.where(is1, s2, s1)))
        digit = jnp.where(is3, 3, jnp.where(is2, 2, jnp.where(is1, 1, 0)))
        return prefix | (digit << b0), above

    end = start + PAD_FRONT + QBLK
    n_att = (start + QBLK + KCH - 1) // KCH

    def chunk_off(j):
        return pl.multiple_of(end - KCH * (j + 1), 128)

    def absorbed_query(h):
        ql = _dot(dq_ref[:, h * DSA_DH:(h + 1) * DSA_DH], wuk_ref[h]) * (LOG2E * DSA_DH ** -0.5)
        qlat_ref[hrows(h), :] = ql.astype(BF16)

    def first_scores(h):
        s_ref[hrows(h), :] = _dot_nt(qlat_ref[hrows(h), :], clat_ref[0, pl.ds(chunk_off(0), KCH), :])

    carry = (jnp.zeros((QBLK, 128), I32), jnp.zeros((QBLK, 128), F32))
    for i in range(16):
        carry = pair_body(i, carry)
        for h in range(i * DSA_HEADS // 16, (i + 1) * DSA_HEADS // 16):
            absorbed_query(h)
            if h > 0:
                first_scores(h - 1)
    first_scores(DSA_HEADS - 1)
    thr = carry[0] ^ INT_MIN
    thr = jnp.maximum(thr, INT_MIN + 1)

    m_ref[...] = jnp.full(m_ref.shape, NEG, F32)
    l_ref[...] = jnp.zeros(l_ref.shape, F32)
    acc_ref[...] = jnp.zeros(acc_ref.shape, F32)

    def scores(j, dst_ref):
        cc = clat_ref[0, pl.ds(chunk_off(j), KCH), :]
        dst_ref[...] = _dot_nt(qlat_ref[...], cc)

    def softmax_update(j, src_ref, near):
        off = chunk_off(j)
        kk = keys_ref[:, pl.ds(off, KCH)]
        madd_ref[...] = jnp.where(kk >= jnp.concatenate([thr] * (KCH // 128), axis=1), 0.0, NEG)
        cc = clat_ref[0, pl.ds(off, KCH), :]
        for h in range(DSA_HEADS):
            s = src_ref[hrows(h), :] + madd_ref[...]
            if near:
                s = jnp.concatenate([s[:, :KCH - 2 * QBLK], s[:, KCH - 2 * QBLK:] + band_ref[h]], axis=1)
            m_old = m_ref[hrows(h), :]
            m_new = jnp.maximum(m_old, jnp.broadcast_to(jnp.max(s, axis=1, keepdims=True), (QBLK, 128)))
            alpha = jnp.exp2(m_old - m_new)
            p = jnp.exp2(s - jnp.concatenate([m_new] * (KCH // 128), axis=1))
            l_ref[hrows(h), :] = (alpha * l_ref[hrows(h), :]
                                  + jnp.broadcast_to(jnp.sum(p, axis=1, keepdims=True), (QBLK, 128)))
            m_ref[hrows(h), :] = m_new
            alpha_ref[hrows(h), :] = alpha
            p_ref[hrows(h), :] = p.astype(BF16)
        al = alpha_ref[...]
        acc_ref[...] = (jnp.concatenate([al] * (DSA_LATENT // 128), axis=1) * acc_ref[...]
                        + _dot(p_ref[...], cc))

    def step(j, cur_ref, nxt_ref, near):
        scores(jnp.minimum(j + 1, n_att - 1), nxt_ref)
        softmax_update(j, cur_ref, near)

    step(0, s_ref, s2_ref, True)

    def att_body(j, carry):
        @pl.when(j % 2 == 1)
        def _():
            step(j, s2_ref, s_ref, False)

        @pl.when(j % 2 == 0)
        def _():
            step(j, s_ref, s2_ref, False)

        return carry

    lax.fori_loop(1, n_att, att_body, 0)

    for h in range(DSA_HEADS):
        inv = 1.0 / l_ref[hrows(h), :]
        o_lat = acc_ref[hrows(h), :] * jnp.concatenate([inv] * (DSA_LATENT // 128), axis=1)
        o_ref[:, h * DSA_DV:(h + 1) * DSA_DV] = _dot(o_lat.astype(BF16), wuv_ref[h]).astype(o_ref.dtype)


def _dsa(proj, misc, kidx, clat, w_uk, w_uv, band, batch, seq):
    nqb = seq // QBLK
    t = batch * seq
    npad = clat.shape[1]
    top_k = min(INDEX_TOPK, seq // 4)
    hq = DSA_HEADS * QBLK
    tokb = lambda b, i: b * nqb + i
    const3 = lambda b, i: (0, 0, 0)
    ngrp = (npad + GROUP_KEYS - 1) // GROUP_KEYS
    once = pl.Buffered(1)
    return pl.pallas_call(
        functools.partial(_dsa_kernel, top_k),
        grid=(batch, nqb),
        in_specs=[pl.BlockSpec((QBLK, 2048), lambda b, i: (tokb(b, i), 3)),
                  pl.BlockSpec((QBLK, 2048), lambda b, i: (tokb(b, i), 4)),
                  pl.BlockSpec((1, QBLK, 128), lambda b, i: (b, i + PAD_FRONT // QBLK, 0)),
                  pl.BlockSpec((1, npad, IDX_DIM), lambda b, i: (b, 0, 0), pipeline_mode=once),
                  pl.BlockSpec((1, npad, DSA_LATENT), lambda b, i: (b, 0, 0), pipeline_mode=once),
                  pl.BlockSpec((DSA_HEADS, DSA_DH, DSA_LATENT), const3, pipeline_mode=once),
                  pl.BlockSpec((DSA_HEADS, DSA_LATENT, DSA_DV), const3, pipeline_mode=once),
                  pl.BlockSpec((DSA_HEADS, QBLK, 2 * QBLK), const3, pipeline_mode=once)],
        out_specs=pl.BlockSpec((QBLK, DSA_HEADS * DSA_DV), lambda b, i: (tokb(b, i), 0)),
        out_shape=jax.ShapeDtypeStruct((t, DSA_HEADS * DSA_DV), BF16),
        scratch_shapes=[pltpu.VMEM((QBLK, ngrp * GROUP_KEYS), I32),
                        pltpu.VMEM((32, QBLK, ngrp * 128), I32),
                        pltpu.VMEM((QBLK, ngrp * 128), I32),
                        pltpu.VMEM((hq, IDX_DIM), BF16),
                        pltpu.VMEM((hq, 128), F32),
                        pltpu.VMEM((hq, DSA_LATENT), BF16),
                        pltpu.VMEM((QBLK, KCH), F32),
                        pltpu.VMEM((hq, KCH), F32),
                        pltpu.VMEM((hq, KCH), F32),
                        pltpu.VMEM((hq, KCH), BF16),
                        pltpu.VMEM((hq, 128), F32),
                        pltpu.VMEM((hq, 128), F32),
                        pltpu.VMEM((hq, 128), F32),
                        pltpu.VMEM((hq, DSA_LATENT), F32)],
        compiler_params=_cparams(("parallel", "arbitrary")),
        name="dsa",
    )(proj, proj, misc, kidx, clat, w_uk, w_uv, band)


def _merge_kernel(yg_ref, yd_ref, gg_ref, gd_ref, wg_ref, wd_ref, o_ref):
    a = _dot(yg_ref[...], wg_ref[...])
    b = _dot(yd_ref[...], wd_ref[...])
    gg = jax.nn.sigmoid(gg_ref[...].astype(F32))
    gd = jax.nn.sigmoid(gd_ref[...].astype(F32))
    o_ref[...] = (gg * a + gd * b).astype(o_ref.dtype)


def _merge(y_gla, y_dsa, proj, w_g, w_d, tm, tn):
    t, d = y_gla.shape
    n = w_g.shape[1]
    ncb = n // tn
    return pl.pallas_call(
        _merge_kernel,
        grid=(t // tm, ncb),
        in_specs=[pl.BlockSpec((tm, d), lambda i, j: (i, 0)),
                  pl.BlockSpec((tm, d), lambda i, j: (i, 0)),
                  pl.BlockSpec((tm, tn), lambda i, j: (i, 5 * ncb + j)),
                  pl.BlockSpec((tm, tn), lambda i, j: (i, 6 * ncb + j)),
                  pl.BlockSpec((d, tn), lambda i, j: (0, j)),
                  pl.BlockSpec((d, tn), lambda i, j: (0, j))],
        out_specs=pl.BlockSpec((tm, tn), lambda i, j: (i, j)),
        out_shape=jax.ShapeDtypeStruct((t, n), BF16),
        compiler_params=_cparams(("parallel", "arbitrary")),
        name="merge",
    )(y_gla, y_dsa, proj, proj, w_g, w_d)


def _proj_norm_res_kernel(y_ref, w_ref, g_ref, h_ref, o_ref):
    y = _dot(y_ref[...], w_ref[...])
    o_ref[...] = h_ref[...] + _rms(y, g_ref[...])


def _proj_norm_res(y, w, gain, h, tm):
    t, d = y.shape
    n = w.shape[1]
    return pl.pallas_call(
        _proj_norm_res_kernel,
        grid=(t // tm,),
        in_specs=[pl.BlockSpec((tm, d), lambda i: (i, 0)),
                  pl.BlockSpec((d, n), lambda i: (0, 0)),
                  pl.BlockSpec((1, n), lambda i: (0, 0)),
                  pl.BlockSpec((tm, n), lambda i: (i, 0))],
        out_specs=pl.BlockSpec((tm, n), lambda i: (i, 0)),
        out_shape=jax.ShapeDtypeStruct((t, n), F32),
        compiler_params=_cparams(("parallel",)),
        name="proj_norm_res",
    )(y, w, gain, h)


def _xattn_kernel(q_ref, k_ref, v_ref, o_ref):
    for h in range(XA_HEADS):
        hs = slice(h * XA_DH, (h + 1) * XA_DH)
        s = _dot_nt(q_ref[:, hs], k_ref[0][:, hs]) * (XA_DH ** -0.5)
        m = jnp.max(s, axis=-1, keepdims=True)
        p = jnp.exp(s - m)
        p = p / jnp.sum(p, axis=-1, keepdims=True)
        o_ref[:, hs] = _dot(p.astype(BF16), v_ref[0][:, hs]).astype(o_ref.dtype)


def _xattn(q, kv, batch, seq, tm):
    t, d = q.shape
    n_mem = kv.shape[1]
    nb = seq // tm
    return pl.pallas_call(
        _xattn_kernel,
        grid=(batch, nb),
        in_specs=[pl.BlockSpec((tm, d), lambda b, i: (b * nb + i, 0)),
                  pl.BlockSpec((1, n_mem, d), lambda b, i: (b, 0, 0)),
                  pl.BlockSpec((1, n_mem, d), lambda b, i: (b, 0, 1))],
        out_specs=pl.BlockSpec((tm, d), lambda b, i: (b * nb + i, 0)),
        out_shape=jax.ShapeDtypeStruct((t, d), BF16),
        compiler_params=_cparams(("parallel", "parallel")),
        name="xattn",
    )(q, kv, kv)


def _ffn_kernel(h_ref, g_ref, wg_ref, wu_ref, wd_ref, pg_ref, o_ref, u_ref, acc_ref):
    f = pl.program_id(1)

    @pl.when(f == 0)
    def _():
        u_ref[...] = _rms(h_ref[...], g_ref[...]).astype(BF16)
        acc_ref[...] = jnp.zeros_like(acc_ref)

    u = u_ref[...]
    a = _dot(u, wg_ref[...])
    b = _dot(u, wu_ref[...])
    act = (a * jax.nn.sigmoid(a) * b).astype(BF16)
    acc_ref[...] += _dot(act, wd_ref[...])

    @pl.when(f == pl.num_programs(1) - 1)
    def _():
        o_ref[...] = h_ref[...] + _rms(acc_ref[...], pg_ref[...])


def _ffn(h, pre_gain, w_gate, w_up, w_down, post_gain, tm, tf):
    t, d = h.shape
    ff = w_gate.shape[1]
    return pl.pallas_call(
        _ffn_kernel,
        grid=(t // tm, ff // tf),
        in_specs=[pl.BlockSpec((tm, d), lambda i, f: (i, 0)),
                  pl.BlockSpec((1, d), lambda i, f: (0, 0)),
                  pl.BlockSpec((d, tf), lambda i, f: (0, f)),
                  pl.BlockSpec((d, tf), lambda i, f: (0, f)),
                  pl.BlockSpec((tf, d), lambda i, f: (f, 0)),
                  pl.BlockSpec((1, d), lambda i, f: (0, 0))],
        out_specs=pl.BlockSpec((tm, d), lambda i, f: (i, 0)),
        out_shape=jax.ShapeDtypeStruct((t, d), F32),
        scratch_shapes=[pltpu.VMEM((tm, d), BF16), pltpu.VMEM((tm, d), F32)],
        compiler_params=_cparams(("parallel", "arbitrary")),
        name="ffn",
    )(h, pre_gain, w_gate, w_up, w_down, post_gain)


def _row(v):
    return v.reshape(1, -1).astype(F32)


def _layer(h, mem, w_in, gla_w_a2, gla_b_a, gla_out_norm, dsa_w_uk, dsa_w_uv, dsa_latent_norm,
           idx_k_norm_w, idx_k_norm_b, band, w_gla_branch, w_dsa_branch, w_mix_out,
           mix_pre_norm, mix_post_norm, xa_pre_norm, xa_post_norm, xa_mem_norm,
           w_xa_q, w_xa_kv, w_xa_o, ffn_pre_norm, ffn_post_norm, w_ffn_gate, w_ffn_up, w_ffn_down,
           batch, seq):
    d = D_MODEL
    t = batch * seq
    cols = lambda a, b: w_in[:, a:b]
    w_main = jnp.concatenate(
        [cols(_O_GQ, _O_GA), cols(_O_DQ, _O_DC), cols(_O_IQ, _O_IK), cols(_O_GG, _O_END)], axis=1).astype(BF16)
    w_small = jnp.concatenate(
        [cols(_O_DC, _O_IQ), cols(_O_IK, _O_IW), cols(_O_GA, _O_DQ), cols(_O_IW, _O_GG),
         jnp.zeros((d, SMALL_COLS - 416), w_in.dtype)], axis=1).astype(BF16)

    pre = _row(mix_pre_norm)
    proj = _norm_matmul(h, pre, w_main, BF16, 1024, 1024)
    clat, kidx, misc = _small_proj(h, pre, w_small, _row(dsa_latent_norm), _row(idx_k_norm_w),
                                   _row(idx_k_norm_b), batch, seq)

    y_gla = _gla(proj, misc, gla_w_a2.astype(BF16), _row(gla_b_a), _row(gla_out_norm), batch, seq)
    y_dsa = _dsa(proj, misc, kidx, clat, dsa_w_uk.astype(BF16), dsa_w_uv.astype(BF16), band, batch, seq)

    merged = _merge(y_gla, y_dsa, proj, w_gla_branch.astype(BF16), w_dsa_branch.astype(BF16), 1024, 512)
    h = _proj_norm_res(merged, w_mix_out.astype(BF16), _row(mix_post_norm), h, 512)

    n_mem = mem.shape[1]
    kv = _norm_matmul(mem.reshape(batch * n_mem, d), _row(xa_mem_norm), w_xa_kv.astype(BF16), BF16,
                      batch * n_mem, 512).reshape(batch, n_mem, 2 * d)
    q = _norm_matmul(h, _row(xa_pre_norm), w_xa_q.astype(BF16), BF16, 1024, 1024)
    o = _xattn(q, kv, batch, seq, 512)
    h = _proj_norm_res(o, w_xa_o.astype(BF16), _row(xa_post_norm), h, 512)

    h = _ffn(h, _row(ffn_pre_norm), w_ffn_gate.astype(BF16), w_ffn_up.astype(BF16),
             w_ffn_down.astype(BF16), _row(ffn_post_norm), 512, 512)
    return h


def kernel(x, mem, w_in, gla_w_a2, gla_b_a, gla_out_norm, dsa_w_uk, dsa_w_uv, dsa_latent_norm,
           idx_k_norm_w, idx_k_norm_b, rel_bias, w_gla_branch, w_dsa_branch, w_mix_out,
           mix_pre_norm, mix_post_norm, xa_pre_norm, xa_post_norm, xa_mem_norm,
           w_xa_q, w_xa_kv, w_xa_o, ffn_pre_norm, ffn_post_norm, w_ffn_gate, w_ffn_up, w_ffn_down):
    batch, seq, d = x.shape
    depth = w_in.shape[0]
    band = _bias_band(rel_bias.astype(F32))
    h = x.reshape(batch * seq, d)
    for l in range(depth):
        h = _layer(h, mem, w_in[l], gla_w_a2[l], gla_b_a[l], gla_out_norm[l], dsa_w_uk[l], dsa_w_uv[l],
                   dsa_latent_norm[l], idx_k_norm_w[l], idx_k_norm_b[l], band, w_gla_branch[l],
                   w_dsa_branch[l], w_mix_out[l], mix_pre_norm[l], mix_post_norm[l], xa_pre_norm[l],
                   xa_post_norm[l], xa_mem_norm[l], w_xa_q[l], w_xa_kv[l], w_xa_o[l], ffn_pre_norm[l],
                   ffn_post_norm[l], w_ffn_gate[l], w_ffn_up[l], w_ffn_down[l], batch, seq)
    return h.reshape(batch, seq, d)
```

```python
import functools
import math

import jax
import jax.numpy as jnp
import numpy as np
from jax import lax
from jax.experimental import pallas as pl
from jax.experimental.pallas import tpu as pltpu

F32 = jnp.float32
BF16 = jnp.bfloat16
I32 = jnp.int32

D_MODEL = 2048
CHUNK = 64
EPS = 1e-6

GLA_HEADS = 4
GLA_DK = 256
GLA_DV = 512
GLA_RANK = 16
GLA_TAU = 16.0
GLA_SUB = 1

DSA_HEADS = 16
DSA_DH = 128
DSA_DV = 128
DSA_LATENT = 256
IDX_HEADS = 16
IDX_DIM = 128
INDEX_TOPK = 256

REL_BUCKETS = 32
REL_MAX_DIST = 128

XA_HEADS = 4
XA_DH = 512

_SPLITS = (1024, 1024, 2048, 2048, 16, 2048, 256, 2048, 128, 16, 2048, 2048)
_OFFS = tuple(int(v) for v in np.cumsum((0,) + _SPLITS))
(_O_GQ, _O_GK, _O_GV, _O_GR, _O_GA, _O_DQ, _O_DC, _O_IQ, _O_IK, _O_IW, _O_GG, _O_GD, _O_END) = _OFFS

MAIN_COLS = 14336
SMALL_COLS = 512

QBLK = 128
KCH = 512
GROUP_KEYS = 32 * 128
PAD_FRONT = KCH
INT_MIN = -2 ** 31
NEG = -1e30
LOG2E = 1.4426950408889634

VMEM_LIMIT = 58 * 1024 * 1024


def _cparams(sem):
    return pltpu.CompilerParams(dimension_semantics=sem, vmem_limit_bytes=VMEM_LIMIT)


def _rms(x, gain):
    ms = jnp.mean(x * x, axis=-1, keepdims=True)
    return x * lax.rsqrt(ms + EPS) * gain


def _dot(a, b):
    return jnp.dot(a, b, preferred_element_type=F32)


def _dot_nt(a, b):
    return lax.dot_general(a, b, (((1,), (1,)), ((), ())), preferred_element_type=F32)


def _norm_matmul_kernel(x_ref, g_ref, w_ref, o_ref, u_ref):
    @pl.when(pl.program_id(1) == 0)
    def _():
        u_ref[...] = _rms(x_ref[...], g_ref[...]).astype(BF16)

    o_ref[...] = _dot(u_ref[...], w_ref[...]).astype(o_ref.dtype)


def _norm_matmul(x, gain, w, out_dtype, tm, tn):
    t, d = x.shape
    n = w.shape[1]
    return pl.pallas_call(
        _norm_matmul_kernel,
        grid=(t // tm, n // tn),
        in_specs=[pl.BlockSpec((tm, d), lambda i, j: (i, 0)),
                  pl.BlockSpec((1, d), lambda i, j: (0, 0)),
                  pl.BlockSpec((d, tn), lambda i, j: (0, j))],
        out_specs=pl.BlockSpec((tm, tn), lambda i, j: (i, j)),
        out_shape=jax.ShapeDtypeStruct((t, n), out_dtype),
        scratch_shapes=[pltpu.VMEM((tm, d), BF16)],
        compiler_params=_cparams(("parallel", "arbitrary")),
        name="norm_matmul",
    )(x, gain, w)


def _small_proj(x, gain, w_small, lat_g, ik_w, ik_b, batch, seq):
    tm = PAD_FRONT
    assert seq % tm == 0
    nblk = seq // tm
    npad = PAD_FRONT + seq
    grid = (batch, 1 + nblk)

    def x_map(b, i):
        return (b * nblk + jnp.maximum(i - 1, 0), 0)

    def kernel(x_ref, g_ref, w_ref, lat_g_ref, ik_w_ref, ik_b_ref, c_ref, k_ref, misc_ref):
        is_pad = pl.program_id(1) == 0

        @pl.when(is_pad)
        def _():
            c_ref[...] = jnp.zeros_like(c_ref)
            k_ref[...] = jnp.zeros_like(k_ref)
            misc_ref[...] = jnp.zeros_like(misc_ref)

        @pl.when(jnp.logical_not(is_pad))
        def _():
            u = _rms(x_ref[...], g_ref[...]).astype(BF16)
            p = _dot(u, w_ref[...])
            c_ref[0] = _rms(p[:, 0:256], lat_g_ref[...]).astype(BF16)
            ik = p[:, 256:384]
            mu = jnp.mean(ik, axis=-1, keepdims=True)
            xc = ik - mu
            var = jnp.mean(xc * xc, axis=-1, keepdims=True)
            k_ref[0] = (xc * lax.rsqrt(var + EPS) * ik_w_ref[...] + ik_b_ref[...]).astype(BF16)
            misc_ref[0] = p[:, 384:512]

    d = x.shape[1]
    const = lambda b, i: (0, 0)
    return pl.pallas_call(
        kernel,
        grid=grid,
        in_specs=[pl.BlockSpec((tm, d), x_map),
                  pl.BlockSpec((1, d), const),
                  pl.BlockSpec((d, SMALL_COLS), const),
                  pl.BlockSpec((1, DSA_LATENT), const),
                  pl.BlockSpec((1, IDX_DIM), const),
                  pl.BlockSpec((1, IDX_DIM), const)],
        out_specs=[pl.BlockSpec((1, tm, DSA_LATENT), lambda b, i: (b, i, 0)),
                   pl.BlockSpec((1, tm, IDX_DIM), lambda b, i: (b, i, 0)),
                   pl.BlockSpec((1, tm, 128), lambda b, i: (b, i, 0))],
        out_shape=[jax.ShapeDtypeStruct((batch, npad, DSA_LATENT), BF16),
                   jax.ShapeDtypeStruct((batch, npad, IDX_DIM), BF16),
                   jax.ShapeDtypeStruct((batch, npad, 128), F32)],
        compiler_params=_cparams(("parallel", "arbitrary")),
        name="small_proj",
    )(x, gain, w_small, lat_g, ik_w, ik_b)


def _log_sigmoid(z):
    return jnp.minimum(z, 0.0) - jnp.log1p(jnp.exp(-jnp.abs(z)))


def _split3(x):
    h = x.astype(BF16)
    r = x - h.astype(F32)
    m = r.astype(BF16)
    l = (r - m.astype(F32)).astype(BF16)
    return h, m, l


def _gla_kernel(nb, q_ref, k_ref, v_ref, r_ref, misc_ref, wa2_ref, ba_ref, on_ref, o_ref, state_ref):
    @pl.when(pl.program_id(0) == 0)
    def _():
        state_ref[...] = jnp.zeros_like(state_ref)

    rows = nb * GLA_SUB * CHUNK
    stack = lambda ref, cols: jnp.concatenate([ref[i][:, cols] for i in range(nb)], axis=0)
    row = lax.broadcasted_iota(I32, (rows, rows), 0)
    col = lax.broadcasted_iota(I32, (rows, rows), 1)
    same = (row // CHUNK) == (col // CHUNK)
    lower = jnp.logical_and(same, col <= row)
    upper = jnp.logical_and(same, col > row)
    tril = jnp.where(lower, 1.0, 0.0).astype(BF16)

    a_low = stack(misc_ref, slice(0, GLA_RANK)).astype(BF16)
    z = _dot(a_low, wa2_ref[...]) + ba_ref[...]
    la = _log_sigmoid(z) * (1.0 / GLA_TAU)
    l_h, l_m, l_l = _split3(la)
    b_all = _dot(tril, l_h) + _dot(tril, l_m) + _dot(tril, l_l)

    for h in range(GLA_HEADS):
        ks = slice(h * GLA_DK, (h + 1) * GLA_DK)
        vs = slice(h * GLA_DV, (h + 1) * GLA_DV)
        b = b_all[:, ks]
        eb = jnp.exp(b)
        ebi = jnp.exp(-b)
        q = stack(q_ref, ks).astype(F32) * (GLA_DK ** -0.5)
        k = stack(k_ref, ks).astype(F32)
        v = stack(v_ref, vs)
        q_fwd = (q * eb).astype(BF16)
        a_lo = _dot_nt(q_fwd, (k * ebi).astype(BF16))
        a_up = _dot_nt((q * ebi).astype(BF16), (k * eb).astype(BF16))
        scores = jnp.where(lower, a_lo, jnp.where(upper, a_up, 0.0)).astype(BF16)
        o_intra = _dot(scores, v)
        outs = []
        for i in range(nb):
            st = state_ref[i * GLA_HEADS + h]
            for sub in range(GLA_SUB):
                rs = slice((i * GLA_SUB + sub) * CHUNK, (i * GLA_SUB + sub + 1) * CHUNK)
                b_i = b[rs]
                b_last = b_i[CHUNK - 1:CHUNK, :]
                outs.append(o_intra[rs] + _dot_nt(q_fwd[rs], st.astype(BF16)))
                k_dec = (k[rs] * jnp.exp(b_last - b_i)).astype(BF16)
                v_t = v[rs].astype(F32).T.astype(BF16)
                st = st * jnp.exp(b_last) + _dot(v_t, k_dec)
            state_ref[i * GLA_HEADS + h] = st
        o = _rms(jnp.concatenate(outs, axis=0), on_ref[...])
        r = stack(r_ref, vs).astype(F32)
        y = (o * (r * jax.nn.sigmoid(r))).astype(o_ref.dtype)
        blk = GLA_SUB * CHUNK
        for i in range(nb):
            o_ref[i, :, vs] = y[i * blk:(i + 1) * blk]


def _gla(proj, misc, w_a2, b_a, out_norm, batch, seq):
    blk = GLA_SUB * CHUNK
    assert seq % blk == 0 and PAD_FRONT % blk == 0
    proj3 = proj.reshape(batch, seq, proj.shape[1])
    y = pl.pallas_call(
        functools.partial(_gla_kernel, batch),
        grid=(seq // blk,),
        in_specs=[pl.BlockSpec((batch, blk, 1024), lambda c: (0, c, 0)),
                  pl.BlockSpec((batch, blk, 1024), lambda c: (0, c, 1)),
                  pl.BlockSpec((batch, blk, 2048), lambda c: (0, c, 1)),
                  pl.BlockSpec((batch, blk, 2048), lambda c: (0, c, 2)),
                  pl.BlockSpec((batch, blk, 128), lambda c: (0, c + PAD_FRONT // blk, 0)),
                  pl.BlockSpec((GLA_RANK, GLA_HEADS * GLA_DK), lambda c: (0, 0)),
                  pl.BlockSpec((1, GLA_HEADS * GLA_DK), lambda c: (0, 0)),
                  pl.BlockSpec((1, GLA_DV), lambda c: (0, 0))],
        out_specs=pl.BlockSpec((batch, blk, GLA_HEADS * GLA_DV), lambda c: (0, c, 0)),
        out_shape=jax.ShapeDtypeStruct((batch, seq, GLA_HEADS * GLA_DV), BF16),
        scratch_shapes=[pltpu.VMEM((batch * GLA_HEADS, GLA_DV, GLA_DK), F32)],
        compiler_params=_cparams(("arbitrary",)),
        name="gla",
    )(proj3, proj3, proj3, proj3, misc, w_a2, b_a, out_norm)
    return y.reshape(batch * seq, GLA_HEADS * GLA_DV)


def _t5_bucket(rel):
    half = REL_BUCKETS // 2
    max_exact = half // 2
    ret = jnp.where(rel > 0, half, 0)
    n = jnp.abs(rel)
    nf = jnp.maximum(n, 1).astype(jnp.float32)
    large = max_exact + (jnp.log(nf / max_exact) / math.log(REL_MAX_DIST / max_exact)
                         * (half - max_exact)).astype(jnp.int32)
    large = jnp.minimum(large, half - 1)
    return ret + jnp.where(n < max_exact, n, large)


def _bias_band_kernel(bucket_ref, rb_ref, o_ref):
    far = REL_BUCKETS // 2 - 1
    bucket = bucket_ref[...]
    for h in range(DSA_HEADS):
        acc = jnp.zeros(bucket.shape, F32)
        for b in range(REL_BUCKETS):
            acc = jnp.where(bucket == b, rb_ref[b, h], acc)
        o_ref[h] = (acc - rb_ref[far, h]) * LOG2E


def _bias_band(rel_bias):
    t = jnp.arange(QBLK, dtype=jnp.int32)[:, None]
    j = jnp.arange(2 * QBLK, dtype=jnp.int32)[None, :]
    bucket = _t5_bucket(j - QBLK - t).astype(jnp.int32)
    return pl.pallas_call(
        _bias_band_kernel,
        in_specs=[pl.BlockSpec(memory_space=pltpu.VMEM), pl.BlockSpec(memory_space=pltpu.SMEM)],
        out_specs=pl.BlockSpec(memory_space=pltpu.VMEM),
        out_shape=jax.ShapeDtypeStruct((DSA_HEADS, QBLK, 2 * QBLK), F32),
        name="bias_band",
    )(bucket, rel_bias)


def _transpose_bits32(words):
    a = list(words)
    j, m = 16, 0x0000FFFF
    while j:
        mask = jnp.int32(m - (1 << 32) if m >= (1 << 31) else m)
        k = 0
        while k < 32:
            t = (a[k] ^ lax.shift_right_logical(a[k + j], jnp.int32(j))) & mask
            a[k] = a[k] ^ t
            a[k + j] = a[k + j] ^ (t << j)
            k = (k + j + 1) & ~j
        j >>= 1
        m = (m ^ (m << j)) & 0xFFFFFFFF
    return a


def _sortable(x):
    i = pltpu.bitcast(x, I32)
    return jnp.where(i < 0, i ^ jnp.int32(0x7FFFFFFF), i)


def _dsa_kernel(top_k, dq_ref, iq_ref, misc_ref, kidx_ref, clat_ref, wuk_ref, wuv_ref, band_ref,
                o_ref, keys_ref, planes_ref, eq_ref, iqs_ref, wb_ref, qlat_ref, madd_ref, s_ref, s2_ref, p_ref, alpha_ref,
                m_ref, l_ref, acc_ref):
    qb = pl.program_id(1)
    start = qb * QBLK
    hrows = lambda h: slice(h * QBLK, (h + 1) * QBLK)

    w_scale = IDX_HEADS ** -0.5 * IDX_DIM ** -0.5
    wq = misc_ref[0][:, GLA_RANK:GLA_RANK + IDX_HEADS] * w_scale
    for h in range(IDX_HEADS):
        wb_ref[hrows(h), :] = jnp.broadcast_to(wq[:, h:h + 1], (QBLK, 128))
        iqs_ref[hrows(h), :] = iq_ref[:, h * IDX_DIM:(h + 1) * IDX_DIM]

    row = lax.broadcasted_iota(I32, (QBLK, KCH), 0)
    lane = lax.broadcasted_iota(I32, (QBLK, KCH), 1)
    p_lim = start + (row // CHUNK + 1) * CHUNK + PAD_FRONT

    n_chunks = (start + PAD_FRONT + QBLK + KCH - 1) // KCH
    keys_ref[:, 0:KCH] = jnp.full((QBLK, KCH), INT_MIN, I32)

    def index_chunk(c, dots_ref):
        off = pl.multiple_of(c * KCH, KCH)
        kc = kidx_ref[0, pl.ds(off, KCH), :]
        dots_ref[...] = _dot_nt(iqs_ref[...], kc)
        acc = jnp.zeros((QBLK, KCH), F32)
        for h in range(IDX_HEADS):
            wbh = wb_ref[hrows(h), :]
            acc = acc + jnp.concatenate([wbh] * (KCH // 128), axis=1) * jnp.maximum(dots_ref[hrows(h), :], 0.0)
        keys_ref[:, pl.ds(off, KCH)] = jnp.where(lane + off < p_lim, _sortable(acc), INT_MIN)

    n_odd = (n_chunks - 1) % 2

    @pl.when(n_odd == 1)
    def _():
        index_chunk(1, s_ref)

    def idx_body(i, carry):
        c = 1 + n_odd + 2 * i
        index_chunk(c, s_ref)
        index_chunk(c + 1, s2_ref)
        return carry

    lax.fori_loop(0, (n_chunks - 1) // 2, idx_body, 0)

    n_groups = (n_chunks * KCH + GROUP_KEYS - 1) // GROUP_KEYS
    ngrp_max = eq_ref.shape[1] // 128

    def pad_body(c, carry):
        keys_ref[:, pl.ds(pl.multiple_of(c * KCH, KCH), KCH)] = jnp.full((QBLK, KCH), INT_MIN, I32)
        return carry

    lax.fori_loop(n_chunks, n_groups * (GROUP_KEYS // KCH), pad_body, 0)

    def plane_body(idx, carry):
        g = idx // (QBLK // 16)
        base = g * GROUP_KEYS
        for half in range(2):
            r0 = pl.multiple_of((idx % (QBLK // 16)) * 16 + half * 8, 8)
            words = [keys_ref[pl.ds(r0, 8), pl.ds(pl.multiple_of(base + j * 128, 128), 128)]
                     for j in range(32)]
            words = _transpose_bits32(words)
            words[0] = ~words[0]
            for i in range(32):
                planes_ref[31 - i, pl.ds(r0, 8), pl.ds(pl.multiple_of(g * 128, 128), 128)] = words[i]
        return carry

    lax.fori_loop(0, n_groups * (QBLK // 16), plane_body, 0)

    for g in range(ngrp_max):
        eq_ref[:, g * 128:(g + 1) * 128] = jnp.broadcast_to(jnp.where(g < n_groups, -1, 0), (QBLK, 128))

    def row_count(t):
        pc = lax.population_count(t)
        tot = pc[:, 0:128]
        for g in range(1, ngrp_max):
            tot = tot + pc[:, g * 128:(g + 1) * 128]
        return jnp.broadcast_to(jnp.sum(tot.astype(F32), axis=1, keepdims=True), (QBLK, 128))

    def pair_body(i, carry):
        prefix, above = carry
        b0 = 30 - 2 * i
        p1 = planes_ref[b0 + 1]
        p0 = planes_ref[b0]
        eq = eq_ref[...]
        e1 = eq & p1
        e0 = eq & ~p1
        t11 = e1 & p0
        t10 = e1 & ~p0
        t01 = e0 & p0
        t00 = e0 & ~p0
        s3 = above + row_count(t11)
        s2 = s3 + row_count(t10)
        s1 = s2 + row_count(t01)
        is3 = s3 >= top_k
        is2 = s2 >= top_k
        is1 = s1 >= top_k
        for g in range(ngrp_max):
            gs = slice(g * 128, (g + 1) * 128)
            eq_ref[:, gs] = jnp.where(is3, t11[:, gs], jnp.where(is2, t10[:, gs], jnp.where(is1, t01[:, gs], t00[:, gs])))
        above = jnp.where(is3, above, jnp.where(is2, s3, jnp.where(is1, s2, s1)))
        digit = jnp.where(is3, 3, jnp.where(is2, 2, jnp.where(is1, 1, 0)))
        return prefix | (digit << b0), above

    end = start + PAD_FRONT + QBLK
    n_att = (start + QBLK + KCH - 1) // KCH

    def chunk_off(j):
        return pl.multiple_of(end - KCH * (j + 1), 128)

    def absorbed_query(h):
        ql = _dot(dq_ref[:, h * DSA_DH:(h + 1) * DSA_DH], wuk_ref[h]) * (LOG2E * DSA_DH ** -0.5)
        qlat_ref[hrows(h), :] = ql.astype(BF16)

    def first_scores(h):
        s_ref[hrows(h), :] = _dot_nt(qlat_ref[hrows(h), :], clat_ref[0, pl.ds(chunk_off(0), KCH), :])

    carry = (jnp.zeros((QBLK, 128), I32), jnp.zeros((QBLK, 128), F32))
    for i in range(16):
        carry = pair_body(i, carry)
        for h in range(i * DSA_HEADS // 16, (i + 1) * DSA_HEADS // 16):
            absorbed_query(h)
            if h > 0:
                first_scores(h - 1)
    first_scores(DSA_HEADS - 1)
    thr = carry[0] ^ INT_MIN
    thr = jnp.maximum(thr, INT_MIN + 1)

    m_ref[...] = jnp.full(m_ref.shape, NEG, F32)
    l_ref[...] = jnp.zeros(l_ref.shape, F32)
    acc_ref[...] = jnp.zeros(acc_ref.shape, F32)

    def scores(j, dst_ref):
        cc = clat_ref[0, pl.ds(chunk_off(j), KCH), :]
        dst_ref[...] = _dot_nt(qlat_ref[...], cc)

    def softmax_update(j, src_ref, near):
        off = chunk_off(j)
        kk = keys_ref[:, pl.ds(off, KCH)]
        madd_ref[...] = jnp.where(kk >= jnp.concatenate([thr] * (KCH // 128), axis=1), 0.0, NEG)
        cc = clat_ref[0, pl.ds(off, KCH), :]
        for h in range(DSA_HEADS):
            s = src_ref[hrows(h), :] + madd_ref[...]
            if near:
                s = jnp.concatenate([s[:, :KCH - 2 * QBLK], s[:, KCH - 2 * QBLK:] + band_ref[h]], axis=1)
            m_old = m_ref[hrows(h), :]
            m_new = jnp.maximum(m_old, jnp.broadcast_to(jnp.max(s, axis=1, keepdims=True), (QBLK, 128)))
            alpha = jnp.exp2(m_old - m_new)
            p = jnp.exp2(s - jnp.concatenate([m_new] * (KCH // 128), axis=1))
            l_ref[hrows(h), :] = (alpha * l_ref[hrows(h), :]
                                  + jnp.broadcast_to(jnp.sum(p, axis=1, keepdims=True), (QBLK, 128)))
            m_ref[hrows(h), :] = m_new
            alpha_ref[hrows(h), :] = alpha
            p_ref[hrows(h), :] = p.astype(BF16)
        al = alpha_ref[...]
        acc_ref[...] = (jnp.concatenate([al] * (DSA_LATENT // 128), axis=1) * acc_ref[...]
                        + _dot(p_ref[...], cc))

    def step(j, cur_ref, nxt_ref, near):
        scores(jnp.minimum(j + 1, n_att - 1), nxt_ref)
        softmax_update(j, cur_ref, near)

    step(0, s_ref, s2_ref, True)

    def att_body(i, carry):
        j = 2 * i + 1
        step(j, s2_ref, s_ref, False)
        step(j + 1, s_ref, s2_ref, False)
        return carry

    lax.fori_loop(0, (n_att - 1) // 2, att_body, 0)

    @pl.when((n_att - 1) % 2 == 1)
    def _():
        step(n_att - 1, s2_ref, s_ref, False)

    for h in range(DSA_HEADS):
        inv = 1.0 / l_ref[hrows(h), :]
        o_lat = acc_ref[hrows(h), :] * jnp.concatenate([inv] * (DSA_LATENT // 128), axis=1)
        o_ref[:, h * DSA_DV:(h + 1) * DSA_DV] = _dot(o_lat.astype(BF16), wuv_ref[h]).astype(o_ref.dtype)


def _dsa(proj, misc, kidx, clat, w_uk, w_uv, band, batch, seq):
    nqb = seq // QBLK
    t = batch * seq
    npad = clat.shape[1]
    top_k = min(INDEX_TOPK, seq // 4)
    hq = DSA_HEADS * QBLK
    tokb = lambda b, i: b * nqb + i
    const3 = lambda b, i: (0, 0, 0)
    ngrp = (npad + GROUP_KEYS - 1) // GROUP_KEYS
    once = pl.Buffered(1)
    return pl.pallas_call(
        functools.partial(_dsa_kernel, top_k),
        grid=(batch, nqb),
        in_specs=[pl.BlockSpec((QBLK, 2048), lambda b, i: (tokb(b, i), 3)),
                  pl.BlockSpec((QBLK, 2048), lambda b, i: (tokb(b, i), 4)),
                  pl.BlockSpec((1, QBLK, 128), lambda b, i: (b, i + PAD_FRONT // QBLK, 0)),
                  pl.BlockSpec((1, npad, IDX_DIM), lambda b, i: (b, 0, 0), pipeline_mode=once),
                  pl.BlockSpec((1, npad, DSA_LATENT), lambda b, i: (b, 0, 0), pipeline_mode=once),
                  pl.BlockSpec((DSA_HEADS, DSA_DH, DSA_LATENT), const3, pipeline_mode=once),
                  pl.BlockSpec((DSA_HEADS, DSA_LATENT, DSA_DV), const3, pipeline_mode=once),
                  pl.BlockSpec((DSA_HEADS, QBLK, 2 * QBLK), const3, pipeline_mode=once)],
        out_specs=pl.BlockSpec((QBLK, DSA_HEADS * DSA_DV), lambda b, i: (tokb(b, i), 0)),
        out_shape=jax.ShapeDtypeStruct((t, DSA_HEADS * DSA_DV), BF16),
        scratch_shapes=[pltpu.VMEM((QBLK, ngrp * GROUP_KEYS), I32),
                        pltpu.VMEM((32, QBLK, ngrp * 128), I32),
                        pltpu.VMEM((QBLK, ngrp * 128), I32),
                        pltpu.VMEM((hq, IDX_DIM), BF16),
                        pltpu.VMEM((hq, 128), F32),
                        pltpu.VMEM((hq, DSA_LATENT), BF16),
                        pltpu.VMEM((QBLK, KCH), F32),
                        pltpu.VMEM((hq, KCH), F32),
                        pltpu.VMEM((hq, KCH), F32),
                        pltpu.VMEM((hq, KCH), BF16),
                        pltpu.VMEM((hq, 128), F32),
                        pltpu.VMEM((hq, 128), F32),
                        pltpu.VMEM((hq, 128), F32),
                        pltpu.VMEM((hq, DSA_LATENT), F32)],
        compiler_params=_cparams(("parallel", "arbitrary")),
        name="dsa",
    )(proj, proj, misc, kidx, clat, w_uk, w_uv, band)


def _merge_kernel(yg_ref, yd_ref, gg_ref, gd_ref, wg_ref, wd_ref, o_ref):
    a = _dot(yg_ref[...], wg_ref[...])
    b = _dot(yd_ref[...], wd_ref[...])
    gg = jax.nn.sigmoid(gg_ref[...].astype(F32))
    gd = jax.nn.sigmoid(gd_ref[...].astype(F32))
    o_ref[...] = (gg * a + gd * b).astype(o_ref.dtype)


def _merge(y_gla, y_dsa, proj, w_g, w_d, tm, tn):
    t, d = y_gla.shape
    n = w_g.shape[1]
    ncb = n // tn
    return pl.pallas_call(
        _merge_kernel,
        grid=(t // tm, ncb),
        in_specs=[pl.BlockSpec((tm, d), lambda i, j: (i, 0)),
                  pl.BlockSpec((tm, d), lambda i, j: (i, 0)),
                  pl.BlockSpec((tm, tn), lambda i, j: (i, 5 * ncb + j)),
                  pl.BlockSpec((tm, tn), lambda i, j: (i, 6 * ncb + j)),
                  pl.BlockSpec((d, tn), lambda i, j: (0, j)),
                  pl.BlockSpec((d, tn), lambda i, j: (0, j))],
        out_specs=pl.BlockSpec((tm, tn), lambda i, j: (i, j)),
        out_shape=jax.ShapeDtypeStruct((t, n), BF16),
        compiler_params=_cparams(("parallel", "arbitrary")),
        name="merge",
    )(y_gla, y_dsa, proj, proj, w_g, w_d)


def _proj_norm_res_kernel(y_ref, w_ref, g_ref, h_ref, o_ref):
    y = _dot(y_ref[...], w_ref[...])
    o_ref[...] = h_ref[...] + _rms(y, g_ref[...])


def _proj_norm_res(y, w, gain, h, tm):
    t, d = y.shape
    n = w.shape[1]
    return pl.pallas_call(
        _proj_norm_res_kernel,
        grid=(t // tm,),
        in_specs=[pl.BlockSpec((tm, d), lambda i: (i, 0)),
                  pl.BlockSpec((d, n), lambda i: (0, 0)),
                  pl.BlockSpec((1, n), lambda i: (0, 0)),
                  pl.BlockSpec((tm, n), lambda i: (i, 0))],
        out_specs=pl.BlockSpec((tm, n), lambda i: (i, 0)),
        out_shape=jax.ShapeDtypeStruct((t, n), F32),
        compiler_params=_cparams(("parallel",)),
        name="proj_norm_res",
    )(y, w, gain, h)


def _xattn_kernel(q_ref, k_ref, v_ref, o_ref):
    for h in range(XA_HEADS):
        hs = slice(h * XA_DH, (h + 1) * XA_DH)
        s = _dot_nt(q_ref[:, hs], k_ref[0][:, hs]) * (XA_DH ** -0.5)
        m = jnp.max(s, axis=-1, keepdims=True)
        p = jnp.exp(s - m)
        p = p / jnp.sum(p, axis=-1, keepdims=True)
        o_ref[:, hs] = _dot(p.astype(BF16), v_ref[0][:, hs]).astype(o_ref.dtype)


def _xattn(q, kv, batch, seq, tm):
    t, d = q.shape
    n_mem = kv.shape[1]
    nb = seq // tm
    return pl.pallas_call(
        _xattn_kernel,
        grid=(batch, nb),
        in_specs=[pl.BlockSpec((tm, d), lambda b, i: (b * nb + i, 0)),
                  pl.BlockSpec((1, n_mem, d), lambda b, i: (b, 0, 0)),
                  pl.BlockSpec((1, n_mem, d), lambda b, i: (b, 0, 1))],
        out_specs=pl.BlockSpec((tm, d), lambda b, i: (b * nb + i, 0)),
        out_shape=jax.ShapeDtypeStruct((t, d), BF16),
        compiler_params=_cparams(("parallel", "parallel")),
        name="xattn",
    )(q, kv, kv)


def _ffn_kernel(h_ref, g_ref, wg_ref, wu_ref, wd_ref, pg_ref, o_ref, u_ref, acc_ref):
    f = pl.program_id(1)

    @pl.when(f == 0)
    def _():
        u_ref[...] = _rms(h_ref[...], g_ref[...]).astype(BF16)
        acc_ref[...] = jnp.zeros_like(acc_ref)

    u = u_ref[...]
    a = _dot(u, wg_ref[...])
    b = _dot(u, wu_ref[...])
    act = (a * jax.nn.sigmoid(a) * b).astype(BF16)
    acc_ref[...] += _dot(act, wd_ref[...])

    @pl.when(f == pl.num_programs(1) - 1)
    def _():
        o_ref[...] = h_ref[...] + _rms(acc_ref[...], pg_ref[...])


def _ffn(h, pre_gain, w_gate, w_up, w_down, post_gain, tm, tf):
    t, d = h.shape
    ff = w_gate.shape[1]
    return pl.pallas_call(
        _ffn_kernel,
        grid=(t // tm, ff // tf),
        in_specs=[pl.BlockSpec((tm, d), lambda i, f: (i, 0)),
                  pl.BlockSpec((1, d), lambda i, f: (0, 0)),
                  pl.BlockSpec((d, tf), lambda i, f: (0, f)),
                  pl.BlockSpec((d, tf), lambda i, f: (0, f)),
                  pl.BlockSpec((tf, d), lambda i, f: (f, 0)),
                  pl.BlockSpec((1, d), lambda i, f: (0, 0))],
        out_specs=pl.BlockSpec((tm, d), lambda i, f: (i, 0)),
        out_shape=jax.ShapeDtypeStruct((t, d), F32),
        scratch_shapes=[pltpu.VMEM((tm, d), BF16), pltpu.VMEM((tm, d), F32)],
        compiler_params=_cparams(("parallel", "arbitrary")),
        name="ffn",
    )(h, pre_gain, w_gate, w_up, w_down, post_gain)


def _row(v):
    return v.reshape(1, -1).astype(F32)


def _layer(h, mem, w_in, gla_w_a2, gla_b_a, gla_out_norm, dsa_w_uk, dsa_w_uv, dsa_latent_norm,
           idx_k_norm_w, idx_k_norm_b, band, w_gla_branch, w_dsa_branch, w_mix_out,
           mix_pre_norm, mix_post_norm, xa_pre_norm, xa_post_norm, xa_mem_norm,
           w_xa_q, w_xa_kv, w_xa_o, ffn_pre_norm, ffn_post_norm, w_ffn_gate, w_ffn_up, w_ffn_down,
           batch, seq):
    d = D_MODEL
    t = batch * seq
    cols = lambda a, b: w_in[:, a:b]
    w_main = jnp.concatenate(
        [cols(_O_GQ, _O_GA), cols(_O_DQ, _O_DC), cols(_O_IQ, _O_IK), cols(_O_GG, _O_END)], axis=1).astype(BF16)
    w_small = jnp.concatenate(
        [cols(_O_DC, _O_IQ), cols(_O_IK, _O_IW), cols(_O_GA, _O_DQ), cols(_O_IW, _O_GG),
         jnp.zeros((d, SMALL_COLS - 416), w_in.dtype)], axis=1).astype(BF16)

    pre = _row(mix_pre_norm)
    proj = _norm_matmul(h, pre, w_main, BF16, 1024, 1024)
    clat, kidx, misc = _small_proj(h, pre, w_small, _row(dsa_latent_norm), _row(idx_k_norm_w),
                                   _row(idx_k_norm_b), batch, seq)

    y_gla = _gla(proj, misc, gla_w_a2.astype(BF16), _row(gla_b_a), _row(gla_out_norm), batch, seq)
    y_dsa = _dsa(proj, misc, kidx, clat, dsa_w_uk.astype(BF16), dsa_w_uv.astype(BF16), band, batch, seq)

    merged = _merge(y_gla, y_dsa, proj, w_gla_branch.astype(BF16), w_dsa_branch.astype(BF16), 1024, 512)
    h = _proj_norm_res(merged, w_mix_out.astype(BF16), _row(mix_post_norm), h, 512)

    n_mem = mem.shape[1]
    kv = _norm_matmul(mem.reshape(batch * n_mem, d), _row(xa_mem_norm), w_xa_kv.astype(BF16), BF16,
                      batch * n_mem, 512).reshape(batch, n_mem, 2 * d)
    q = _norm_matmul(h, _row(xa_pre_norm), w_xa_q.astype(BF16), BF16, 1024, 1024)
    o = _xattn(q, kv, batch, seq, 512)
    h = _proj_norm_res(o, w_xa_o.astype(BF16), _row(xa_post_norm), h, 512)

    h = _ffn(h, _row(ffn_pre_norm), w_ffn_gate.astype(BF16), w_ffn_up.astype(BF16),
             w_ffn_down.astype(BF16), _row(ffn_post_norm), 512, 512)
    return h


def kernel(x, mem, w_in, gla_w_a2, gla_b_a, gla_out_norm, dsa_w_uk, dsa_w_uv, dsa_latent_norm,
           idx_k_norm_w, idx_k_norm_b, rel_bias, w_gla_branch, w_dsa_branch, w_mix_out,
           mix_pre_norm, mix_post_norm, xa_pre_norm, xa_post_norm, xa_mem_norm,
           w_xa_q, w_xa_kv, w_xa_o, ffn_pre_norm, ffn_post_norm, w_ffn_gate, w_ffn_up, w_ffn_down):
    batch, seq, d = x.shape
    depth = w_in.shape[0]
    band = _bias_band(rel_bias.astype(F32))
    h = x.reshape(batch * seq, d)
    for l in range(depth):
        h = _layer(h, mem, w_in[l], gla_w_a2[l], gla_b_a[l], gla_out_norm[l], dsa_w_uk[l], dsa_w_uv[l],
                   dsa_latent_norm[l], idx_k_norm_w[l], idx_k_norm_b[l], band, w_gla_branch[l],
                   w_dsa_branch[l], w_mix_out[l], mix_pre_norm[l], mix_post_norm[l], xa_pre_norm[l],
                   xa_post_norm[l], xa_mem_norm[l], w_xa_q[l], w_xa_kv[l], w_xa_o[l], ffn_pre_norm[l],
                   ffn_post_norm[l], w_ffn_gate[l], w_ffn_up[l], w_ffn_down[l], batch, seq)
    return h.reshape(batch, seq, d)
```

```python
import functools
import math

import jax
import jax.numpy as jnp
import numpy as np
from jax import lax
from jax.experimental import pallas as pl
from jax.experimental.pallas import tpu as pltpu

F32 = jnp.float32
BF16 = jnp.bfloat16
I32 = jnp.int32

D_MODEL = 2048
CHUNK = 64
EPS = 1e-6

GLA_HEADS = 4
GLA_DK = 256
GLA_DV = 512
GLA_RANK = 16
GLA_TAU = 16.0
GLA_SUB = 1

DSA_HEADS = 16
DSA_DH = 128
DSA_DV = 128
DSA_LATENT = 256
IDX_HEADS = 16
IDX_DIM = 128
INDEX_TOPK = 256

REL_BUCKETS = 32
REL_MAX_DIST = 128

XA_HEADS = 4
XA_DH = 512

_SPLITS = (1024, 1024, 2048, 2048, 16, 2048, 256, 2048, 128, 16, 2048, 2048)
_OFFS = tuple(int(v) for v in np.cumsum((0,) + _SPLITS))
(_O_GQ, _O_GK, _O_GV, _O_GR, _O_GA, _O_DQ, _O_DC, _O_IQ, _O_IK, _O_IW, _O_GG, _O_GD, _O_END) = _OFFS

MAIN_COLS = 14336
SMALL_COLS = 512

QBLK = 128
KCH = 512
GROUP_KEYS = 32 * 128
PAD_FRONT = KCH
INT_MIN = -2 ** 31
NEG = -1e30
LOG2E = 1.4426950408889634

VMEM_LIMIT = 58 * 1024 * 1024


def _cparams(sem):
    return pltpu.CompilerParams(dimension_semantics=sem, vmem_limit_bytes=VMEM_LIMIT)


def _rms(x, gain):
    ms = jnp.mean(x * x, axis=-1, keepdims=True)
    return x * lax.rsqrt(ms + EPS) * gain


def _dot(a, b):
    return jnp.dot(a, b, preferred_element_type=F32)


def _dot_nt(a, b):
    return lax.dot_general(a, b, (((1,), (1,)), ((), ())), preferred_element_type=F32)


def _norm_matmul_kernel(x_ref, g_ref, w_ref, o_ref, u_ref):
    @pl.when(pl.program_id(1) == 0)
    def _():
        u_ref[...] = _rms(x_ref[...], g_ref[...]).astype(BF16)

    o_ref[...] = _dot(u_ref[...], w_ref[...]).astype(o_ref.dtype)


def _norm_matmul(x, gain, w, out_dtype, tm, tn):
    t, d = x.shape
    n = w.shape[1]
    return pl.pallas_call(
        _norm_matmul_kernel,
        grid=(t // tm, n // tn),
        in_specs=[pl.BlockSpec((tm, d), lambda i, j: (i, 0)),
                  pl.BlockSpec((1, d), lambda i, j: (0, 0)),
                  pl.BlockSpec((d, tn), lambda i, j: (0, j))],
        out_specs=pl.BlockSpec((tm, tn), lambda i, j: (i, j)),
        out_shape=jax.ShapeDtypeStruct((t, n), out_dtype),
        scratch_shapes=[pltpu.VMEM((tm, d), BF16)],
        compiler_params=_cparams(("parallel", "arbitrary")),
        name="norm_matmul",
    )(x, gain, w)


def _small_proj(x, gain, w_small, lat_g, ik_w, ik_b, batch, seq):
    tm = PAD_FRONT
    assert seq % tm == 0
    nblk = seq // tm
    npad = PAD_FRONT + seq
    grid = (batch, 1 + nblk)

    def x_map(b, i):
        return (b * nblk + jnp.maximum(i - 1, 0), 0)

    def kernel(x_ref, g_ref, w_ref, lat_g_ref, ik_w_ref, ik_b_ref, c_ref, k_ref, misc_ref):
        is_pad = pl.program_id(1) == 0

        @pl.when(is_pad)
        def _():
            c_ref[...] = jnp.zeros_like(c_ref)
            k_ref[...] = jnp.zeros_like(k_ref)
            misc_ref[...] = jnp.zeros_like(misc_ref)

        @pl.when(jnp.logical_not(is_pad))
        def _():
            u = _rms(x_ref[...], g_ref[...]).astype(BF16)
            p = _dot(u, w_ref[...])
            c_ref[0] = _rms(p[:, 0:256], lat_g_ref[...]).astype(BF16)
            ik = p[:, 256:384]
            mu = jnp.mean(ik, axis=-1, keepdims=True)
            xc = ik - mu
            var = jnp.mean(xc * xc, axis=-1, keepdims=True)
            k_ref[0] = (xc * lax.rsqrt(var + EPS) * ik_w_ref[...] + ik_b_ref[...]).astype(BF16)
            misc_ref[0] = p[:, 384:512]

    d = x.shape[1]
    const = lambda b, i: (0, 0)
    return pl.pallas_call(
        kernel,
        grid=grid,
        in_specs=[pl.BlockSpec((tm, d), x_map),
                  pl.BlockSpec((1, d), const),
                  pl.BlockSpec((d, SMALL_COLS), const),
                  pl.BlockSpec((1, DSA_LATENT), const),
                  pl.BlockSpec((1, IDX_DIM), const),
                  pl.BlockSpec((1, IDX_DIM), const)],
        out_specs=[pl.BlockSpec((1, tm, DSA_LATENT), lambda b, i: (b, i, 0)),
                   pl.BlockSpec((1, tm, IDX_DIM), lambda b, i: (b, i, 0)),
                   pl.BlockSpec((1, tm, 128), lambda b, i: (b, i, 0))],
        out_shape=[jax.ShapeDtypeStruct((batch, npad, DSA_LATENT), BF16),
                   jax.ShapeDtypeStruct((batch, npad, IDX_DIM), BF16),
                   jax.ShapeDtypeStruct((batch, npad, 128), F32)],
        compiler_params=_cparams(("parallel", "arbitrary")),
        name="small_proj",
    )(x, gain, w_small, lat_g, ik_w, ik_b)


def _log_sigmoid(z):
    return jnp.minimum(z, 0.0) - jnp.log1p(jnp.exp(-jnp.abs(z)))


def _split3(x):
    h = x.astype(BF16)
    r = x - h.astype(F32)
    m = r.astype(BF16)
    l = (r - m.astype(F32)).astype(BF16)
    return h, m, l


def _gla_kernel(nb, q_ref, k_ref, v_ref, r_ref, misc_ref, wa2_ref, ba_ref, on_ref, o_ref, state_ref):
    @pl.when(pl.program_id(0) == 0)
    def _():
        state_ref[...] = jnp.zeros_like(state_ref)

    rows = nb * GLA_SUB * CHUNK
    stack = lambda ref, cols: jnp.concatenate([ref[i][:, cols] for i in range(nb)], axis=0)
    row = lax.broadcasted_iota(I32, (rows, rows), 0)
    col = lax.broadcasted_iota(I32, (rows, rows), 1)
    same = (row // CHUNK) == (col // CHUNK)
    lower = jnp.logical_and(same, col <= row)
    upper = jnp.logical_and(same, col > row)
    tril = jnp.where(lower, 1.0, 0.0).astype(BF16)

    a_low = stack(misc_ref, slice(0, GLA_RANK)).astype(BF16)
    z = _dot(a_low, wa2_ref[...]) + ba_ref[...]
    la = _log_sigmoid(z) * (1.0 / GLA_TAU)
    l_h, l_m, l_l = _split3(la)
    b_all = _dot(tril, l_h) + _dot(tril, l_m) + _dot(tril, l_l)

    for h in range(GLA_HEADS):
        ks = slice(h * GLA_DK, (h + 1) * GLA_DK)
        vs = slice(h * GLA_DV, (h + 1) * GLA_DV)
        b = b_all[:, ks]
        eb = jnp.exp(b)
        ebi = jnp.exp(-b)
        q = stack(q_ref, ks).astype(F32) * (GLA_DK ** -0.5)
        k = stack(k_ref, ks).astype(F32)
        v = stack(v_ref, vs)
        q_fwd = (q * eb).astype(BF16)
        a_lo = _dot_nt(q_fwd, (k * ebi).astype(BF16))
        a_up = _dot_nt((q * ebi).astype(BF16), (k * eb).astype(BF16))
        scores = jnp.where(lower, a_lo, jnp.where(upper, a_up, 0.0)).astype(BF16)
        o_intra = _dot(scores, v)
        outs = []
        for i in range(nb):
            st = state_ref[i * GLA_HEADS + h]
            for sub in range(GLA_SUB):
                rs = slice((i * GLA_SUB + sub) * CHUNK, (i * GLA_SUB + sub + 1) * CHUNK)
                b_i = b[rs]
                b_last = b_i[CHUNK - 1:CHUNK, :]
                outs.append(o_intra[rs] + _dot_nt(q_fwd[rs], st.astype(BF16)))
                k_dec = (k[rs] * jnp.exp(b_last - b_i)).astype(BF16)
                v_t = v[rs].astype(F32).T.astype(BF16)
                st = st * jnp.exp(b_last) + _dot(v_t, k_dec)
            state_ref[i * GLA_HEADS + h] = st
        o = _rms(jnp.concatenate(outs, axis=0), on_ref[...])
        r = stack(r_ref, vs).astype(F32)
        y = (o * (r * jax.nn.sigmoid(r))).astype(o_ref.dtype)
        blk = GLA_SUB * CHUNK
        for i in range(nb):
            o_ref[i, :, vs] = y[i * blk:(i + 1) * blk]


def _gla(proj, misc, w_a2, b_a, out_norm, batch, seq):
    blk = GLA_SUB * CHUNK
    assert seq % blk == 0 and PAD_FRONT % blk == 0
    proj3 = proj.reshape(batch, seq, proj.shape[1])
    y = pl.pallas_call(
        functools.partial(_gla_kernel, batch),
        grid=(seq // blk,),
        in_specs=[pl.BlockSpec((batch, blk, 1024), lambda c: (0, c, 0)),
                  pl.BlockSpec((batch, blk, 1024), lambda c: (0, c, 1)),
                  pl.BlockSpec((batch, blk, 2048), lambda c: (0, c, 1)),
                  pl.BlockSpec((batch, blk, 2048), lambda c: (0, c, 2)),
                  pl.BlockSpec((batch, blk, 128), lambda c: (0, c + PAD_FRONT // blk, 0)),
                  pl.BlockSpec((GLA_RANK, GLA_HEADS * GLA_DK), lambda c: (0, 0)),
                  pl.BlockSpec((1, GLA_HEADS * GLA_DK), lambda c: (0, 0)),
                  pl.BlockSpec((1, GLA_DV), lambda c: (0, 0))],
        out_specs=pl.BlockSpec((batch, blk, GLA_HEADS * GLA_DV), lambda c: (0, c, 0)),
        out_shape=jax.ShapeDtypeStruct((batch, seq, GLA_HEADS * GLA_DV), BF16),
        scratch_shapes=[pltpu.VMEM((batch * GLA_HEADS, GLA_DV, GLA_DK), F32)],
        compiler_params=_cparams(("arbitrary",)),
        name="gla",
    )(proj3, proj3, proj3, proj3, misc, w_a2, b_a, out_norm)
    return y.reshape(batch * seq, GLA_HEADS * GLA_DV)


def _t5_bucket(rel):
    half = REL_BUCKETS // 2
    max_exact = half // 2
    ret = jnp.where(rel > 0, half, 0)
    n = jnp.abs(rel)
    nf = jnp.maximum(n, 1).astype(jnp.float32)
    large = max_exact + (jnp.log(nf / max_exact) / math.log(REL_MAX_DIST / max_exact)
                         * (half - max_exact)).astype(jnp.int32)
    large = jnp.minimum(large, half - 1)
    return ret + jnp.where(n < max_exact, n, large)


def _bias_band_kernel(bucket_ref, rb_ref, o_ref):
    far = REL_BUCKETS // 2 - 1
    bucket = bucket_ref[...]
    for h in range(DSA_HEADS):
        acc = jnp.zeros(bucket.shape, F32)
        for b in range(REL_BUCKETS):
            acc = jnp.where(bucket == b, rb_ref[b, h], acc)
        o_ref[h] = (acc - rb_ref[far, h]) * LOG2E


def _bias_band(rel_bias):
    t = jnp.arange(QBLK, dtype=jnp.int32)[:, None]
    j = jnp.arange(2 * QBLK, dtype=jnp.int32)[None, :]
    bucket = _t5_bucket(j - QBLK - t).astype(jnp.int32)
    return pl.pallas_call(
        _bias_band_kernel,
        in_specs=[pl.BlockSpec(memory_space=pltpu.VMEM), pl.BlockSpec(memory_space=pltpu.SMEM)],
        out_specs=pl.BlockSpec(memory_space=pltpu.VMEM),
        out_shape=jax.ShapeDtypeStruct((DSA_HEADS, QBLK, 2 * QBLK), F32),
        name="bias_band",
    )(bucket, rel_bias)


def _transpose_bits32(words):
    a = list(words)
    j, m = 16, 0x0000FFFF
    while j:
        mask = jnp.int32(m - (1 << 32) if m >= (1 << 31) else m)
        k = 0
        while k < 32:
            t = (a[k] ^ lax.shift_right_logical(a[k + j], jnp.int32(j))) & mask
            a[k] = a[k] ^ t
            a[k + j] = a[k + j] ^ (t << j)
            k = (k + j + 1) & ~j
        j >>= 1
        m = (m ^ (m << j)) & 0xFFFFFFFF
    return a


def _sortable(x):
    i = pltpu.bitcast(x, I32)
    return jnp.where(i < 0, i ^ jnp.int32(0x7FFFFFFF), i)


def _dsa_kernel(top_k, dq_ref, iq_ref, misc_ref, kidx_ref, clat_ref, wuk_ref, wuv_ref, band_ref,
                o_ref, keys_ref, planes_ref, eq_ref, iqs_ref, wb_ref, qlat_ref, madd_ref, s_ref, s2_ref, p_ref, alpha_ref,
                m_ref, l_ref, acc_ref):
    qb = pl.program_id(1)
    start = qb * QBLK
    hrows = lambda h: slice(h * QBLK, (h + 1) * QBLK)

    w_scale = IDX_HEADS ** -0.5 * IDX_DIM ** -0.5
    wq = misc_ref[0][:, GLA_RANK:GLA_RANK + IDX_HEADS] * w_scale
    for h in range(IDX_HEADS):
        wb_ref[hrows(h), :] = jnp.broadcast_to(wq[:, h:h + 1], (QBLK, 128))
        iqs_ref[hrows(h), :] = iq_ref[:, h * IDX_DIM:(h + 1) * IDX_DIM]

    row = lax.broadcasted_iota(I32, (QBLK, KCH), 0)
    lane = lax.broadcasted_iota(I32, (QBLK, KCH), 1)
    p_lim = start + (row // CHUNK + 1) * CHUNK + PAD_FRONT

    n_chunks = (start + PAD_FRONT + QBLK + KCH - 1) // KCH
    keys_ref[:, 0:KCH] = jnp.full((QBLK, KCH), INT_MIN, I32)

    def index_chunk(c, dots_ref):
        off = pl.multiple_of(c * KCH, KCH)
        kc = kidx_ref[0, pl.ds(off, KCH), :]
        dots_ref[...] = _dot_nt(iqs_ref[...], kc)
        acc = jnp.zeros((QBLK, KCH), F32)
        for h in range(IDX_HEADS):
            wbh = wb_ref[hrows(h), :]
            acc = acc + jnp.concatenate([wbh] * (KCH // 128), axis=1) * jnp.maximum(dots_ref[hrows(h), :], 0.0)
        keys_ref[:, pl.ds(off, KCH)] = jnp.where(lane + off < p_lim, _sortable(acc), INT_MIN)

    n_odd = (n_chunks - 1) % 2

    @pl.when(n_odd == 1)
    def _():
        index_chunk(1, s_ref)

    def idx_body(i, carry):
        c = 1 + n_odd + 2 * i
        index_chunk(c, s_ref)
        index_chunk(c + 1, s2_ref)
        return carry

    lax.fori_loop(0, (n_chunks - 1) // 2, idx_body, 0)

    n_groups = (n_chunks * KCH + GROUP_KEYS - 1) // GROUP_KEYS
    ngrp_max = eq_ref.shape[1] // 128

    def pad_body(c, carry):
        keys_ref[:, pl.ds(pl.multiple_of(c * KCH, KCH), KCH)] = jnp.full((QBLK, KCH), INT_MIN, I32)
        return carry

    lax.fori_loop(n_chunks, n_groups * (GROUP_KEYS // KCH), pad_body, 0)

    def plane_body(idx, carry):
        g = idx // (QBLK // 16)
        base = g * GROUP_KEYS
        for half in range(2):
            r0 = pl.multiple_of((idx % (QBLK // 16)) * 16 + half * 8, 8)
            words = [keys_ref[pl.ds(r0, 8), pl.ds(pl.multiple_of(base + j * 128, 128), 128)]
                     for j in range(32)]
            words = _transpose_bits32(words)
            words[0] = ~words[0]
            for i in range(32):
                planes_ref[31 - i, pl.ds(r0, 8), pl.ds(pl.multiple_of(g * 128, 128), 128)] = words[i]
        return carry

    lax.fori_loop(0, n_groups * (QBLK // 16), plane_body, 0)

    for g in range(ngrp_max):
        eq_ref[:, g * 128:(g + 1) * 128] = jnp.broadcast_to(jnp.where(g < n_groups, -1, 0), (QBLK, 128))

    def row_count(t):
        pc = lax.population_count(t)
        tot = pc[:, 0:128]
        for g in range(1, ngrp_max):
            tot = tot + pc[:, g * 128:(g + 1) * 128]
        return jnp.broadcast_to(jnp.sum(tot.astype(F32), axis=1, keepdims=True), (QBLK, 128))

    def pair_body(i, carry):
        prefix, above = carry
        b0 = 30 - 2 * i
        p1 = planes_ref[b0 + 1]
        p0 = planes_ref[b0]
        eq = eq_ref[...]
        e1 = eq & p1
        e0 = eq & ~p1
        t11 = e1 & p0
        t10 = e1 & ~p0
        t01 = e0 & p0
        t00 = e0 & ~p0
        s3 = above + row_count(t11)
        s2 = s3 + row_count(t10)
        s1 = s2 + row_count(t01)
        is3 = s3 >= top_k
        is2 = s2 >= top_k
        is1 = s1 >= top_k
        for g in range(ngrp_max):
            gs = slice(g * 128, (g + 1) * 128)
            eq_ref[:, gs] = jnp.where(is3, t11[:, gs], jnp.where(is2, t10[:, gs], jnp.where(is1, t01[:, gs], t00[:, gs])))
        above = jnp.where(is3, above, jnp.where(is2, s3, jnp.where(is1, s2, s1)))
        digit = jnp.where(is3, 3, jnp.where(is2, 2, jnp.where(is1, 1, 0)))
        return prefix | (digit << b0), above

    end = start + PAD_FRONT + QBLK
    n_att = (start + QBLK + KCH - 1) // KCH

    def chunk_off(j):
        return pl.multiple_of(end - KCH * (j + 1), 128)

    def absorbed_query(h):
        ql = _dot(dq_ref[:, h * DSA_DH:(h + 1) * DSA_DH], wuk_ref[h]) * (LOG2E * DSA_DH ** -0.5)
        qlat_ref[hrows(h), :] = ql.astype(BF16)

    def first_scores(h):
        s_ref[hrows(h), :] = _dot_nt(qlat_ref[hrows(h), :], clat_ref[0, pl.ds(chunk_off(0), KCH), :])

    carry = (jnp.zeros((QBLK, 128), I32), jnp.zeros((QBLK, 128), F32))
    for i in range(16):
        carry = pair_body(i, carry)
        for h in range(i * DSA_HEADS // 16, (i + 1) * DSA_HEADS // 16):
            absorbed_query(h)
            if h > 0:
                first_scores(h - 1)
    first_scores(DSA_HEADS - 1)
    thr = carry[0] ^ INT_MIN
    thr = jnp.maximum(thr, INT_MIN + 1)

    m_ref[...] = jnp.full(m_ref.shape, NEG, F32)
    l_ref[...] = jnp.zeros(l_ref.shape, F32)
    acc_ref[...] = jnp.zeros(acc_ref.shape, F32)

    def scores(j, dst_ref):
        cc = clat_ref[0, pl.ds(chunk_off(j), KCH), :]
        dst_ref[...] = _dot_nt(qlat_ref[...], cc)

    def softmax_update(j, src_ref, near):
        off = chunk_off(j)
        kk = keys_ref[:, pl.ds(off, KCH)]
        madd_ref[...] = jnp.where(kk >= jnp.concatenate([thr] * (KCH // 128), axis=1), 0.0, NEG)
        cc = clat_ref[0, pl.ds(off, KCH), :]
        for h in range(DSA_HEADS):
            s = src_ref[hrows(h), :] + madd_ref[...]
            if near:
                s = jnp.concatenate([s[:, :KCH - 2 * QBLK], s[:, KCH - 2 * QBLK:] + band_ref[h]], axis=1)
            m_old = m_ref[hrows(h), :]
            m_new = jnp.maximum(m_old, jnp.broadcast_to(jnp.max(s, axis=1, keepdims=True), (QBLK, 128)))
            alpha = jnp.exp2(m_old - m_new)
            p = jnp.exp2(s - jnp.concatenate([m_new] * (KCH // 128), axis=1))
            l_ref[hrows(h), :] = (alpha * l_ref[hrows(h), :]
                                  + jnp.broadcast_to(jnp.sum(p, axis=1, keepdims=True), (QBLK, 128)))
            m_ref[hrows(h), :] = m_new
            alpha_ref[hrows(h), :] = alpha
            p_ref[hrows(h), :] = p.astype(BF16)
        al = alpha_ref[...]
        acc_ref[...] = (jnp.concatenate([al] * (DSA_LATENT // 128), axis=1) * acc_ref[...]
                        + _dot(p_ref[...], cc))

    def step(j, cur_ref, nxt_ref, near):
        scores(jnp.minimum(j + 1, n_att - 1), nxt_ref)
        softmax_update(j, cur_ref, near)

    step(0, s_ref, s2_ref, True)

    def att_body(i, carry):
        j = 2 * i + 1
        step(j, s2_ref, s_ref, False)
        step(j + 1, s_ref, s2_ref, False)
        return carry

    lax.fori_loop(0, (n_att - 1) // 2, att_body, 0)

    @pl.when((n_att - 1) % 2 == 1)
    def _():
        step(n_att - 1, s2_ref, s_ref, False)

    for h in range(DSA_HEADS):
        inv = 1.0 / l_ref[hrows(h), :]
        o_lat = acc_ref[hrows(h), :] * jnp.concatenate([inv] * (DSA_LATENT // 128), axis=1)
        o_ref[:, h * DSA_DV:(h + 1) * DSA_DV] = _dot(o_lat.astype(BF16), wuv_ref[h]).astype(o_ref.dtype)


def _dsa(proj, misc, kidx, clat, w_uk, w_uv, band, batch, seq):
    nqb = seq // QBLK
    t = batch * seq
    npad = clat.shape[1]
    top_k = min(INDEX_TOPK, seq // 4)
    hq = DSA_HEADS * QBLK
    tokb = lambda b, i: b * nqb + i
    const3 = lambda b, i: (0, 0, 0)
    ngrp = (npad + GROUP_KEYS - 1) // GROUP_KEYS
    once = pl.Buffered(1)
    return pl.pallas_call(
        functools.partial(_dsa_kernel, top_k),
        grid=(batch, nqb),
        in_specs=[pl.BlockSpec((QBLK, 2048), lambda b, i: (tokb(b, i), 3)),
                  pl.BlockSpec((QBLK, 2048), lambda b, i: (tokb(b, i), 4)),
                  pl.BlockSpec((1, QBLK, 128), lambda b, i: (b, i + PAD_FRONT // QBLK, 0)),
                  pl.BlockSpec((1, npad, IDX_DIM), lambda b, i: (b, 0, 0), pipeline_mode=once),
                  pl.BlockSpec((1, npad, DSA_LATENT), lambda b, i: (b, 0, 0), pipeline_mode=once),
                  pl.BlockSpec((DSA_HEADS, DSA_DH, DSA_LATENT), const3, pipeline_mode=once),
                  pl.BlockSpec((DSA_HEADS, DSA_LATENT, DSA_DV), const3, pipeline_mode=once),
                  pl.BlockSpec((DSA_HEADS, QBLK, 2 * QBLK), const3, pipeline_mode=once)],
        out_specs=pl.BlockSpec((QBLK, DSA_HEADS * DSA_DV), lambda b, i: (tokb(b, i), 0)),
        out_shape=jax.ShapeDtypeStruct((t, DSA_HEADS * DSA_DV), BF16),
        scratch_shapes=[pltpu.VMEM((QBLK, ngrp * GROUP_KEYS), I32),
                        pltpu.VMEM((32, QBLK, ngrp * 128), I32),
                        pltpu.VMEM((QBLK, ngrp * 128), I32),
                        pltpu.VMEM((hq, IDX_DIM), BF16),
                        pltpu.VMEM((hq, 128), F32),
                        pltpu.VMEM((hq, DSA_LATENT), BF16),
                        pltpu.VMEM((QBLK, KCH), F32),
                        pltpu.VMEM((hq, KCH), F32),
                        pltpu.VMEM((hq, KCH), F32),
                        pltpu.VMEM((hq, KCH), BF16),
                        pltpu.VMEM((hq, 128), F32),
                        pltpu.VMEM((hq, 128), F32),
                        pltpu.VMEM((hq, 128), F32),
                        pltpu.VMEM((hq, DSA_LATENT), F32)],
        compiler_params=_cparams(("parallel", "arbitrary")),
        name="dsa",
    )(proj, proj, misc, kidx, clat, w_uk, w_uv, band)


def _merge_kernel(yg_ref, yd_ref, gg_ref, gd_ref, wg_ref, wd_ref, o_ref):
    a = _dot(yg_ref[...], wg_ref[...])
    b = _dot(yd_ref[...], wd_ref[...])
    gg = jax.nn.sigmoid(gg_ref[...].astype(F32))
    gd = jax.nn.sigmoid(gd_ref[...].astype(F32))
    o_ref[...] = (gg * a + gd * b).astype(o_ref.dtype)


def _merge(y_gla, y_dsa, proj, w_g, w_d, tm, tn):
    t, d = y_gla.shape
    n = w_g.shape[1]
    ncb = n // tn
    return pl.pallas_call(
        _merge_kernel,
        grid=(t // tm, ncb),
        in_specs=[pl.BlockSpec((tm, d), lambda i, j: (i, 0)),
                  pl.BlockSpec((tm, d), lambda i, j: (i, 0)),
                  pl.BlockSpec((tm, tn), lambda i, j: (i, 5 * ncb + j)),
                  pl.BlockSpec((tm, tn), lambda i, j: (i, 6 * ncb + j)),
                  pl.BlockSpec((d, tn), lambda i, j: (0, j)),
                  pl.BlockSpec((d, tn), lambda i, j: (0, j))],
        out_specs=pl.BlockSpec((tm, tn), lambda i, j: (i, j)),
        out_shape=jax.ShapeDtypeStruct((t, n), BF16),
        compiler_params=_cparams(("parallel", "arbitrary")),
        name="merge",
    )(y_gla, y_dsa, proj, proj, w_g, w_d)


def _proj_norm_res_kernel(y_ref, w_ref, g_ref, h_ref, o_ref):
    y = _dot(y_ref[...], w_ref[...])
    o_ref[...] = h_ref[...] + _rms(y, g_ref[...])


def _proj_norm_res(y, w, gain, h, tm):
    t, d = y.shape
    n = w.shape[1]
    return pl.pallas_call(
        _proj_norm_res_kernel,
        grid=(t // tm,),
        in_specs=[pl.BlockSpec((tm, d), lambda i: (i, 0)),
                  pl.BlockSpec((d, n), lambda i: (0, 0)),
                  pl.BlockSpec((1, n), lambda i: (0, 0)),
                  pl.BlockSpec((tm, n), lambda i: (i, 0))],
        out_specs=pl.BlockSpec((tm, n), lambda i: (i, 0)),
        out_shape=jax.ShapeDtypeStruct((t, n), F32),
        compiler_params=_cparams(("parallel",)),
        name="proj_norm_res",
    )(y, w, gain, h)


def _xattn_kernel(q_ref, k_ref, v_ref, o_ref):
    for h in range(XA_HEADS):
        hs = slice(h * XA_DH, (h + 1) * XA_DH)
        s = _dot_nt(q_ref[:, hs], k_ref[0][:, hs]) * (XA_DH ** -0.5)
        m = jnp.max(s, axis=-1, keepdims=True)
        p = jnp.exp(s - m)
        p = p / jnp.sum(p, axis=-1, keepdims=True)
        o_ref[:, hs] = _dot(p.astype(BF16), v_ref[0][:, hs]).astype(o_ref.dtype)


def _xattn(q, kv, batch, seq, tm):
    t, d = q.shape
    n_mem = kv.shape[1]
    nb = seq // tm
    return pl.pallas_call(
        _xattn_kernel,
        grid=(batch, nb),
        in_specs=[pl.BlockSpec((tm, d), lambda b, i: (b * nb + i, 0)),
                  pl.BlockSpec((1, n_mem, d), lambda b, i: (b, 0, 0)),
                  pl.BlockSpec((1, n_mem, d), lambda b, i: (b, 0, 1))],
        out_specs=pl.BlockSpec((tm, d), lambda b, i: (b * nb + i, 0)),
        out_shape=jax.ShapeDtypeStruct((t, d), BF16),
        compiler_params=_cparams(("parallel", "parallel")),
        name="xattn",
    )(q, kv, kv)


def _ffn_kernel(h_ref, g_ref, wg_ref, wu_ref, wd_ref, pg_ref, o_ref, u_ref):
    f = pl.program_id(1)

    @pl.when(f == 0)
    def _():
        u_ref[...] = _rms(h_ref[...], g_ref[...]).astype(BF16)
        o_ref[...] = jnp.zeros_like(o_ref)

    u = u_ref[...]
    n_sub = max(wg_ref.shape[1] // 256, 1)
    half = wg_ref.shape[1] // n_sub
    down = None
    for s in range(n_sub):
        cs = slice(s * half, (s + 1) * half)
        a = _dot(u, wg_ref[:, cs])
        b = _dot(u, wu_ref[:, cs])
        act = (a * jax.nn.sigmoid(a) * b).astype(BF16)
        d = _dot(act, wd_ref[cs, :])
        down = d if down is None else down + d
    o_ref[...] += down

    @pl.when(f == pl.num_programs(1) - 1)
    def _():
        o_ref[...] = h_ref[...] + _rms(o_ref[...], pg_ref[...])


def _ffn(h, pre_gain, w_gate, w_up, w_down, post_gain, tm, tf):
    t, d = h.shape
    ff = w_gate.shape[1]
    return pl.pallas_call(
        _ffn_kernel,
        grid=(t // tm, ff // tf),
        in_specs=[pl.BlockSpec((tm, d), lambda i, f: (i, 0)),
                  pl.BlockSpec((1, d), lambda i, f: (0, 0)),
                  pl.BlockSpec((d, tf), lambda i, f: (0, f)),
                  pl.BlockSpec((d, tf), lambda i, f: (0, f)),
                  pl.BlockSpec((tf, d), lambda i, f: (f, 0)),
                  pl.BlockSpec((1, d), lambda i, f: (0, 0))],
        out_specs=pl.BlockSpec((tm, d), lambda i, f: (i, 0)),
        out_shape=jax.ShapeDtypeStruct((t, d), F32),
        scratch_shapes=[pltpu.VMEM((tm, d), BF16)],
        compiler_params=_cparams(("parallel", "arbitrary")),
        name="ffn",
    )(h, pre_gain, w_gate, w_up, w_down, post_gain)


def _row(v):
    return v.reshape(1, -1).astype(F32)


def _layer(h, mem, w_in, gla_w_a2, gla_b_a, gla_out_norm, dsa_w_uk, dsa_w_uv, dsa_latent_norm,
           idx_k_norm_w, idx_k_norm_b, band, w_gla_branch, w_dsa_branch, w_mix_out,
           mix_pre_norm, mix_post_norm, xa_pre_norm, xa_post_norm, xa_mem_norm,
           w_xa_q, w_xa_kv, w_xa_o, ffn_pre_norm, ffn_post_norm, w_ffn_gate, w_ffn_up, w_ffn_down,
           batch, seq):
    d = D_MODEL
    t = batch * seq
    cols = lambda a, b: w_in[:, a:b]
    w_main = jnp.concatenate(
        [cols(_O_GQ, _O_GA), cols(_O_DQ, _O_DC), cols(_O_IQ, _O_IK), cols(_O_GG, _O_END)], axis=1).astype(BF16)
    w_small = jnp.concatenate(
        [cols(_O_DC, _O_IQ), cols(_O_IK, _O_IW), cols(_O_GA, _O_DQ), cols(_O_IW, _O_GG),
         jnp.zeros((d, SMALL_COLS - 416), w_in.dtype)], axis=1).astype(BF16)

    pre = _row(mix_pre_norm)
    proj = _norm_matmul(h, pre, w_main, BF16, 1024, 1024)
    clat, kidx, misc = _small_proj(h, pre, w_small, _row(dsa_latent_norm), _row(idx_k_norm_w),
                                   _row(idx_k_norm_b), batch, seq)

    y_gla = _gla(proj, misc, gla_w_a2.astype(BF16), _row(gla_b_a), _row(gla_out_norm), batch, seq)
    y_dsa = _dsa(proj, misc, kidx, clat, dsa_w_uk.astype(BF16), dsa_w_uv.astype(BF16), band, batch, seq)

    merged = _merge(y_gla, y_dsa, proj, w_gla_branch.astype(BF16), w_dsa_branch.astype(BF16), 1024, 512)
    h = _proj_norm_res(merged, w_mix_out.astype(BF16), _row(mix_post_norm), h, 512)

    n_mem = mem.shape[1]
    kv = _norm_matmul(mem.reshape(batch * n_mem, d), _row(xa_mem_norm), w_xa_kv.astype(BF16), BF16,
                      batch * n_mem, 512).reshape(batch, n_mem, 2 * d)
    q = _norm_matmul(h, _row(xa_pre_norm), w_xa_q.astype(BF16), BF16, 1024, 1024)
    o = _xattn(q, kv, batch, seq, 512)
    h = _proj_norm_res(o, w_xa_o.astype(BF16), _row(xa_post_norm), h, 512)

    h = _ffn(h, _row(ffn_pre_norm), w_ffn_gate.astype(BF16), w_ffn_up.astype(BF16),
             w_ffn_down.astype(BF16), _row(ffn_post_norm), 1024, 256)
    return h


def kernel(x, mem, w_in, gla_w_a2, gla_b_a, gla_out_norm, dsa_w_uk, dsa_w_uv, dsa_latent_norm,
           idx_k_norm_w, idx_k_norm_b, rel_bias, w_gla_branch, w_dsa_branch, w_mix_out,
           mix_pre_norm, mix_post_norm, xa_pre_norm, xa_post_norm, xa_mem_norm,
           w_xa_q, w_xa_kv, w_xa_o, ffn_pre_norm, ffn_post_norm, w_ffn_gate, w_ffn_up, w_ffn_down):
    batch, seq, d = x.shape
    depth = w_in.shape[0]
    band = _bias_band(rel_bias.astype(F32))
    h = x.reshape(batch * seq, d)
    for l in range(depth):
        h = _layer(h, mem, w_in[l], gla_w_a2[l], gla_b_a[l], gla_out_norm[l], dsa_w_uk[l], dsa_w_uv[l],
                   dsa_latent_norm[l], idx_k_norm_w[l], idx_k_norm_b[l], band, w_gla_branch[l],
                   w_dsa_branch[l], w_mix_out[l], mix_pre_norm[l], mix_post_norm[l], xa_pre_norm[l],
                   xa_post_norm[l], xa_mem_norm[l], w_xa_q[l], w_xa_kv[l], w_xa_o[l], ffn_pre_norm[l],
                   ffn_post_norm[l], w_ffn_gate[l], w_ffn_up[l], w_ffn_down[l], batch, seq)
    return h.reshape(batch, seq, d)
```

```python
import functools
import math

import jax
import jax.numpy as jnp
import numpy as np
from jax import lax
from jax.experimental import pallas as pl
from jax.experimental.pallas import tpu as pltpu

F32 = jnp.float32
BF16 = jnp.bfloat16
I32 = jnp.int32

D_MODEL = 2048
CHUNK = 64
EPS = 1e-6

GLA_HEADS = 4
GLA_DK = 256
GLA_DV = 512
GLA_RANK = 16
GLA_TAU = 16.0
GLA_SUB = 1

DSA_HEADS = 16
DSA_DH = 128
DSA_DV = 128
DSA_LATENT = 256
IDX_HEADS = 16
IDX_DIM = 128
INDEX_TOPK = 256

REL_BUCKETS = 32
REL_MAX_DIST = 128

XA_HEADS = 4
XA_DH = 512

_SPLITS = (1024, 1024, 2048, 2048, 16, 2048, 256, 2048, 128, 16, 2048, 2048)
_OFFS = tuple(int(v) for v in np.cumsum((0,) + _SPLITS))
(_O_GQ, _O_GK, _O_GV, _O_GR, _O_GA, _O_DQ, _O_DC, _O_IQ, _O_IK, _O_IW, _O_GG, _O_GD, _O_END) = _OFFS

MAIN_COLS = 14336
SMALL_COLS = 512

QBLK = 128
KCH = 512
GROUP_KEYS = 32 * 128
PAD_FRONT = KCH
INT_MIN = -2 ** 31
NEG = -1e30
LOG2E = 1.4426950408889634

VMEM_LIMIT = 58 * 1024 * 1024


def _cparams(sem):
    return pltpu.CompilerParams(dimension_semantics=sem, vmem_limit_bytes=VMEM_LIMIT)


def _rms(x, gain):
    ms = jnp.mean(x * x, axis=-1, keepdims=True)
    return x * lax.rsqrt(ms + EPS) * gain


def _dot(a, b):
    return jnp.dot(a, b, preferred_element_type=F32)


def _dot_nt(a, b):
    return lax.dot_general(a, b, (((1,), (1,)), ((), ())), preferred_element_type=F32)


def _norm_matmul_kernel(x_ref, g_ref, w_ref, o_ref, u_ref):
    @pl.when(pl.program_id(1) == 0)
    def _():
        u_ref[...] = _rms(x_ref[...], g_ref[...]).astype(BF16)

    o_ref[...] = _dot(u_ref[...], w_ref[...]).astype(o_ref.dtype)


def _norm_matmul(x, gain, w, out_dtype, tm, tn):
    t, d = x.shape
    n = w.shape[1]
    return pl.pallas_call(
        _norm_matmul_kernel,
        grid=(t // tm, n // tn),
        in_specs=[pl.BlockSpec((tm, d), lambda i, j: (i, 0)),
                  pl.BlockSpec((1, d), lambda i, j: (0, 0)),
                  pl.BlockSpec((d, tn), lambda i, j: (0, j))],
        out_specs=pl.BlockSpec((tm, tn), lambda i, j: (i, j)),
        out_shape=jax.ShapeDtypeStruct((t, n), out_dtype),
        scratch_shapes=[pltpu.VMEM((tm, d), BF16)],
        compiler_params=_cparams(("parallel", "arbitrary")),
        name="norm_matmul",
    )(x, gain, w)


def _small_proj(x, gain, w_small, lat_g, ik_w, ik_b, batch, seq):
    tm = PAD_FRONT
    assert seq % tm == 0
    nblk = seq // tm
    npad = PAD_FRONT + seq
    grid = (batch, 1 + nblk)

    def x_map(b, i):
        return (b * nblk + jnp.maximum(i - 1, 0), 0)

    def kernel(x_ref, g_ref, w_ref, lat_g_ref, ik_w_ref, ik_b_ref, c_ref, k_ref, misc_ref):
        is_pad = pl.program_id(1) == 0

        @pl.when(is_pad)
        def _():
            c_ref[...] = jnp.zeros_like(c_ref)
            k_ref[...] = jnp.zeros_like(k_ref)
            misc_ref[...] = jnp.zeros_like(misc_ref)

        @pl.when(jnp.logical_not(is_pad))
        def _():
            u = _rms(x_ref[...], g_ref[...]).astype(BF16)
            p = _dot(u, w_ref[...])
            c_ref[0] = _rms(p[:, 0:256], lat_g_ref[...]).astype(BF16)
            ik = p[:, 256:384]
            mu = jnp.mean(ik, axis=-1, keepdims=True)
            xc = ik - mu
            var = jnp.mean(xc * xc, axis=-1, keepdims=True)
            k_ref[0] = (xc * lax.rsqrt(var + EPS) * ik_w_ref[...] + ik_b_ref[...]).astype(BF16)
            misc_ref[0] = p[:, 384:512]

    d = x.shape[1]
    const = lambda b, i: (0, 0)
    return pl.pallas_call(
        kernel,
        grid=grid,
        in_specs=[pl.BlockSpec((tm, d), x_map),
                  pl.BlockSpec((1, d), const),
                  pl.BlockSpec((d, SMALL_COLS), const),
                  pl.BlockSpec((1, DSA_LATENT), const),
                  pl.BlockSpec((1, IDX_DIM), const),
                  pl.BlockSpec((1, IDX_DIM), const)],
        out_specs=[pl.BlockSpec((1, tm, DSA_LATENT), lambda b, i: (b, i, 0)),
                   pl.BlockSpec((1, tm, IDX_DIM), lambda b, i: (b, i, 0)),
                   pl.BlockSpec((1, tm, 128), lambda b, i: (b, i, 0))],
        out_shape=[jax.ShapeDtypeStruct((batch, npad, DSA_LATENT), BF16),
                   jax.ShapeDtypeStruct((batch, npad, IDX_DIM), BF16),
                   jax.ShapeDtypeStruct((batch, npad, 128), F32)],
        compiler_params=_cparams(("parallel", "arbitrary")),
        name="small_proj",
    )(x, gain, w_small, lat_g, ik_w, ik_b)


def _log_sigmoid(z):
    return jnp.minimum(z, 0.0) - jnp.log1p(jnp.exp(-jnp.abs(z)))


def _split3(x):
    h = x.astype(BF16)
    r = x - h.astype(F32)
    m = r.astype(BF16)
    l = (r - m.astype(F32)).astype(BF16)
    return h, m, l


def _gla_kernel(nb, q_ref, k_ref, v_ref, r_ref, misc_ref, wa2_ref, ba_ref, on_ref, o_ref, state_ref):
    @pl.when(pl.program_id(0) == 0)
    def _():
        state_ref[...] = jnp.zeros_like(state_ref)

    rows = nb * GLA_SUB * CHUNK
    stack = lambda ref, cols: jnp.concatenate([ref[i][:, cols] for i in range(nb)], axis=0)
    row = lax.broadcasted_iota(I32, (rows, rows), 0)
    col = lax.broadcasted_iota(I32, (rows, rows), 1)
    same = (row // CHUNK) == (col // CHUNK)
    lower = jnp.logical_and(same, col <= row)
    upper = jnp.logical_and(same, col > row)
    tril = jnp.where(lower, 1.0, 0.0).astype(BF16)

    a_low = stack(misc_ref, slice(0, GLA_RANK)).astype(BF16)
    z = _dot(a_low, wa2_ref[...]) + ba_ref[...]
    la = _log_sigmoid(z) * (1.0 / GLA_TAU)
    l_h, l_m, l_l = _split3(la)
    b_all = _dot(tril, l_h) + _dot(tril, l_m) + _dot(tril, l_l)

    for h in range(GLA_HEADS):
        ks = slice(h * GLA_DK, (h + 1) * GLA_DK)
        vs = slice(h * GLA_DV, (h + 1) * GLA_DV)
        b = b_all[:, ks]
        eb = jnp.exp(b)
        ebi = jnp.exp(-b)
        q = stack(q_ref, ks).astype(F32) * (GLA_DK ** -0.5)
        k = stack(k_ref, ks).astype(F32)
        v = stack(v_ref, vs)
        q_fwd = (q * eb).astype(BF16)
        a_lo = _dot_nt(q_fwd, (k * ebi).astype(BF16))
        a_up = _dot_nt((q * ebi).astype(BF16), (k * eb).astype(BF16))
        scores = jnp.where(lower, a_lo, jnp.where(upper, a_up, 0.0)).astype(BF16)
        o_intra = _dot(scores, v)
        outs = []
        for i in range(nb):
            st = state_ref[i * GLA_HEADS + h]
            for sub in range(GLA_SUB):
                rs = slice((i * GLA_SUB + sub) * CHUNK, (i * GLA_SUB + sub + 1) * CHUNK)
                b_i = b[rs]
                b_last = b_i[CHUNK - 1:CHUNK, :]
                outs.append(o_intra[rs] + _dot_nt(q_fwd[rs], st.astype(BF16)))
                k_dec = (k[rs] * jnp.exp(b_last - b_i)).astype(BF16)
                v_t = v[rs].astype(F32).T.astype(BF16)
                st = st * jnp.exp(b_last) + _dot(v_t, k_dec)
            state_ref[i * GLA_HEADS + h] = st
        o = _rms(jnp.concatenate(outs, axis=0), on_ref[...])
        r = stack(r_ref, vs).astype(F32)
        y = (o * (r * jax.nn.sigmoid(r))).astype(o_ref.dtype)
        blk = GLA_SUB * CHUNK
        for i in range(nb):
            o_ref[i, :, vs] = y[i * blk:(i + 1) * blk]


def _gla(proj, misc, w_a2, b_a, out_norm, batch, seq):
    blk = GLA_SUB * CHUNK
    assert seq % blk == 0 and PAD_FRONT % blk == 0
    proj3 = proj.reshape(batch, seq, proj.shape[1])
    y = pl.pallas_call(
        functools.partial(_gla_kernel, batch),
        grid=(seq // blk,),
        in_specs=[pl.BlockSpec((batch, blk, 1024), lambda c: (0, c, 0)),
                  pl.BlockSpec((batch, blk, 1024), lambda c: (0, c, 1)),
                  pl.BlockSpec((batch, blk, 2048), lambda c: (0, c, 1)),
                  pl.BlockSpec((batch, blk, 2048), lambda c: (0, c, 2)),
                  pl.BlockSpec((batch, blk, 128), lambda c: (0, c + PAD_FRONT // blk, 0)),
                  pl.BlockSpec((GLA_RANK, GLA_HEADS * GLA_DK), lambda c: (0, 0)),
                  pl.BlockSpec((1, GLA_HEADS * GLA_DK), lambda c: (0, 0)),
                  pl.BlockSpec((1, GLA_DV), lambda c: (0, 0))],
        out_specs=pl.BlockSpec((batch, blk, GLA_HEADS * GLA_DV), lambda c: (0, c, 0)),
        out_shape=jax.ShapeDtypeStruct((batch, seq, GLA_HEADS * GLA_DV), BF16),
        scratch_shapes=[pltpu.VMEM((batch * GLA_HEADS, GLA_DV, GLA_DK), F32)],
        compiler_params=_cparams(("arbitrary",)),
        name="gla",
    )(proj3, proj3, proj3, proj3, misc, w_a2, b_a, out_norm)
    return y.reshape(batch * seq, GLA_HEADS * GLA_DV)


def _t5_bucket(rel):
    half = REL_BUCKETS // 2
    max_exact = half // 2
    ret = jnp.where(rel > 0, half, 0)
    n = jnp.abs(rel)
    nf = jnp.maximum(n, 1).astype(jnp.float32)
    large = max_exact + (jnp.log(nf / max_exact) / math.log(REL_MAX_DIST / max_exact)
                         * (half - max_exact)).astype(jnp.int32)
    large = jnp.minimum(large, half - 1)
    return ret + jnp.where(n < max_exact, n, large)


def _bias_band_kernel(bucket_ref, rb_ref, o_ref):
    far = REL_BUCKETS // 2 - 1
    bucket = bucket_ref[...]
    for h in range(DSA_HEADS):
        acc = jnp.zeros(bucket.shape, F32)
        for b in range(REL_BUCKETS):
            acc = jnp.where(bucket == b, rb_ref[b, h], acc)
        o_ref[h] = (acc - rb_ref[far, h]) * LOG2E


def _bias_band(rel_bias):
    t = jnp.arange(QBLK, dtype=jnp.int32)[:, None]
    j = jnp.arange(2 * QBLK, dtype=jnp.int32)[None, :]
    bucket = _t5_bucket(j - QBLK - t).astype(jnp.int32)
    return pl.pallas_call(
        _bias_band_kernel,
        in_specs=[pl.BlockSpec(memory_space=pltpu.VMEM), pl.BlockSpec(memory_space=pltpu.SMEM)],
        out_specs=pl.BlockSpec(memory_space=pltpu.VMEM),
        out_shape=jax.ShapeDtypeStruct((DSA_HEADS, QBLK, 2 * QBLK), F32),
        name="bias_band",
    )(bucket, rel_bias)


def _transpose_bits32(words):
    a = list(words)
    j, m = 16, 0x0000FFFF
    while j:
        mask = jnp.int32(m - (1 << 32) if m >= (1 << 31) else m)
        k = 0
        while k < 32:
            t = (a[k] ^ lax.shift_right_logical(a[k + j], jnp.int32(j))) & mask
            a[k] = a[k] ^ t
            a[k + j] = a[k + j] ^ (t << j)
            k = (k + j + 1) & ~j
        j >>= 1
        m = (m ^ (m << j)) & 0xFFFFFFFF
    return a


def _sortable(x):
    i = pltpu.bitcast(x, I32)
    return jnp.where(i < 0, i ^ jnp.int32(0x7FFFFFFF), i)


def _dsa_kernel(top_k, dq_ref, iq_ref, misc_ref, kidx_ref, clat_ref, wuk_ref, wuv_ref, band_ref,
                o_ref, keys_ref, planes_ref, eq_ref, iqs_ref, wb_ref, qlat_ref, madd_ref, s_ref, s2_ref, p_ref, alpha_ref,
                m_ref, l_ref, acc_ref):
    qb = pl.program_id(1)
    start = qb * QBLK
    hrows = lambda h: slice(h * QBLK, (h + 1) * QBLK)

    w_scale = IDX_HEADS ** -0.5 * IDX_DIM ** -0.5
    wq = misc_ref[0][:, GLA_RANK:GLA_RANK + IDX_HEADS] * w_scale
    for h in range(IDX_HEADS):
        wb_ref[hrows(h), :] = jnp.broadcast_to(wq[:, h:h + 1], (QBLK, 128))
        iqs_ref[hrows(h), :] = iq_ref[:, h * IDX_DIM:(h + 1) * IDX_DIM]

    row = lax.broadcasted_iota(I32, (QBLK, KCH), 0)
    lane = lax.broadcasted_iota(I32, (QBLK, KCH), 1)
    p_lim = start + (row // CHUNK + 1) * CHUNK + PAD_FRONT

    n_chunks = (start + PAD_FRONT + QBLK + KCH - 1) // KCH
    keys_ref[:, 0:KCH] = jnp.full((QBLK, KCH), INT_MIN, I32)

    def index_chunk(c, dots_ref):
        off = pl.multiple_of(c * KCH, KCH)
        kc = kidx_ref[0, pl.ds(off, KCH), :]
        dots_ref[...] = _dot_nt(iqs_ref[...], kc)
        acc = jnp.zeros((QBLK, KCH), F32)
        for h in range(IDX_HEADS):
            wbh = wb_ref[hrows(h), :]
            acc = acc + jnp.concatenate([wbh] * (KCH // 128), axis=1) * jnp.maximum(dots_ref[hrows(h), :], 0.0)
        keys_ref[:, pl.ds(off, KCH)] = jnp.where(lane + off < p_lim, _sortable(acc), INT_MIN)

    n_odd = (n_chunks - 1) % 2

    @pl.when(n_odd == 1)
    def _():
        index_chunk(1, s_ref)

    def idx_body(i, carry):
        c = 1 + n_odd + 2 * i
        index_chunk(c, s_ref)
        index_chunk(c + 1, s2_ref)
        return carry

    lax.fori_loop(0, (n_chunks - 1) // 2, idx_body, 0)

    n_groups = (n_chunks * KCH + GROUP_KEYS - 1) // GROUP_KEYS
    ngrp_max = eq_ref.shape[1] // 128

    def pad_body(c, carry):
        keys_ref[:, pl.ds(pl.multiple_of(c * KCH, KCH), KCH)] = jnp.full((QBLK, KCH), INT_MIN, I32)
        return carry

    lax.fori_loop(n_chunks, n_groups * (GROUP_KEYS // KCH), pad_body, 0)

    def plane_body(idx, carry):
        g = idx // (QBLK // 16)
        base = g * GROUP_KEYS
        for half in range(2):
            r0 = pl.multiple_of((idx % (QBLK // 16)) * 16 + half * 8, 8)
            words = [keys_ref[pl.ds(r0, 8), pl.ds(pl.multiple_of(base + j * 128, 128), 128)]
                     for j in range(32)]
            words = _transpose_bits32(words)
            words[0] = ~words[0]
            for i in range(32):
                planes_ref[31 - i, pl.ds(r0, 8), pl.ds(pl.multiple_of(g * 128, 128), 128)] = words[i]
        return carry

    lax.fori_loop(0, n_groups * (QBLK // 16), plane_body, 0)

    for g in range(ngrp_max):
        eq_ref[:, g * 128:(g + 1) * 128] = jnp.broadcast_to(jnp.where(g < n_groups, -1, 0), (QBLK, 128))

    def row_count(t):
        pc = lax.population_count(t)
        tot = pc[:, 0:128]
        for g in range(1, ngrp_max):
            tot = tot + pc[:, g * 128:(g + 1) * 128]
        return jnp.broadcast_to(jnp.sum(tot.astype(F32), axis=1, keepdims=True), (QBLK, 128))

    def pair_body(i, carry):
        prefix, above = carry
        b0 = 30 - 2 * i
        p1 = planes_ref[b0 + 1]
        p0 = planes_ref[b0]
        eq = eq_ref[...]
        e1 = eq & p1
        e0 = eq & ~p1
        t11 = e1 & p0
        t10 = e1 & ~p0
        t01 = e0 & p0
        t00 = e0 & ~p0
        s3 = above + row_count(t11)
        s2 = s3 + row_count(t10)
        s1 = s2 + row_count(t01)
        is3 = s3 >= top_k
        is2 = s2 >= top_k
        is1 = s1 >= top_k
        for g in range(ngrp_max):
            gs = slice(g * 128, (g + 1) * 128)
            eq_ref[:, gs] = jnp.where(is3, t11[:, gs], jnp.where(is2, t10[:, gs], jnp.where(is1, t01[:, gs], t00[:, gs])))
        above = jnp.where(is3, above, jnp.where(is2, s3, jnp.where(is1, s2, s1)))
        digit = jnp.where(is3, 3, jnp.where(is2, 2, jnp.where(is1, 1, 0)))
        return prefix | (digit << b0), above

    end = start + PAD_FRONT + QBLK
    n_att = (start + QBLK + KCH - 1) // KCH

    def chunk_off(j):
        return pl.multiple_of(end - KCH * (j + 1), 128)

    def absorbed_query(h):
        ql = _dot(dq_ref[:, h * DSA_DH:(h + 1) * DSA_DH], wuk_ref[h]) * (LOG2E * DSA_DH ** -0.5)
        qlat_ref[hrows(h), :] = ql.astype(BF16)

    def first_scores(h):
        s_ref[hrows(h), :] = _dot_nt(qlat_ref[hrows(h), :], clat_ref[0, pl.ds(chunk_off(0), KCH), :])

    carry = (jnp.zeros((QBLK, 128), I32), jnp.zeros((QBLK, 128), F32))
    for i in range(16):
        carry = pair_body(i, carry)
        for h in range(i * DSA_HEADS // 16, (i + 1) * DSA_HEADS // 16):
            absorbed_query(h)
            if h > 0:
                first_scores(h - 1)
    first_scores(DSA_HEADS - 1)
    thr = carry[0] ^ INT_MIN
    thr = jnp.maximum(thr, INT_MIN + 1)

    m_ref[...] = jnp.full(m_ref.shape, NEG, F32)
    l_ref[...] = jnp.zeros(l_ref.shape, F32)
    acc_ref[...] = jnp.zeros(acc_ref.shape, F32)

    def scores(j, dst_ref):
        cc = clat_ref[0, pl.ds(chunk_off(j), KCH), :]
        dst_ref[...] = _dot_nt(qlat_ref[...], cc)

    def softmax_update(j, src_ref, near):
        off = chunk_off(j)
        kk = keys_ref[:, pl.ds(off, KCH)]
        madd_ref[...] = jnp.where(kk >= jnp.concatenate([thr] * (KCH // 128), axis=1), 0.0, NEG)
        cc = clat_ref[0, pl.ds(off, KCH), :]
        for h in range(DSA_HEADS):
            s = src_ref[hrows(h), :] + madd_ref[...]
            if near:
                s = jnp.concatenate([s[:, :KCH - 2 * QBLK], s[:, KCH - 2 * QBLK:] + band_ref[h]], axis=1)
            m_old = m_ref[hrows(h), :]
            m_new = jnp.maximum(m_old, jnp.broadcast_to(jnp.max(s, axis=1, keepdims=True), (QBLK, 128)))
            alpha = jnp.exp2(m_old - m_new)
            p = jnp.exp2(s - jnp.concatenate([m_new] * (KCH // 128), axis=1))
            l_ref[hrows(h), :] = (alpha * l_ref[hrows(h), :]
                                  + jnp.broadcast_to(jnp.sum(p, axis=1, keepdims=True), (QBLK, 128)))
            m_ref[hrows(h), :] = m_new
            alpha_ref[hrows(h), :] = alpha
            p_ref[hrows(h), :] = p.astype(BF16)
        al = alpha_ref[...]
        acc_ref[...] = (jnp.concatenate([al] * (DSA_LATENT // 128), axis=1) * acc_ref[...]
                        + _dot(p_ref[...], cc))

    def step(j, cur_ref, nxt_ref, near):
        scores(jnp.minimum(j + 1, n_att - 1), nxt_ref)
        softmax_update(j, cur_ref, near)

    step(0, s_ref, s2_ref, True)

    def att_body(i, carry):
        j = 2 * i + 1
        step(j, s2_ref, s_ref, False)
        step(j + 1, s_ref, s2_ref, False)
        return carry

    lax.fori_loop(0, (n_att - 1) // 2, att_body, 0)

    @pl.when((n_att - 1) % 2 == 1)
    def _():
        step(n_att - 1, s2_ref, s_ref, False)

    for h in range(DSA_HEADS):
        inv = 1.0 / l_ref[hrows(h), :]
        o_lat = acc_ref[hrows(h), :] * jnp.concatenate([inv] * (DSA_LATENT // 128), axis=1)
        o_ref[:, h * DSA_DV:(h + 1) * DSA_DV] = _dot(o_lat.astype(BF16), wuv_ref[h]).astype(o_ref.dtype)


def _dsa(proj, misc, kidx, clat, w_uk, w_uv, band, batch, seq):
    nqb = seq // QBLK
    t = batch * seq
    npad = clat.shape[1]
    top_k = min(INDEX_TOPK, seq // 4)
    hq = DSA_HEADS * QBLK
    tokb = lambda b, i: b * nqb + i
    const3 = lambda b, i: (0, 0, 0)
    ngrp = (npad + GROUP_KEYS - 1) // GROUP_KEYS
    once = pl.Buffered(1)
    return pl.pallas_call(
        functools.partial(_dsa_kernel, top_k),
        grid=(batch, nqb),
        in_specs=[pl.BlockSpec((QBLK, 2048), lambda b, i: (tokb(b, i), 3)),
                  pl.BlockSpec((QBLK, 2048), lambda b, i: (tokb(b, i), 4)),
                  pl.BlockSpec((1, QBLK, 128), lambda b, i: (b, i + PAD_FRONT // QBLK, 0)),
                  pl.BlockSpec((1, npad, IDX_DIM), lambda b, i: (b, 0, 0), pipeline_mode=once),
                  pl.BlockSpec((1, npad, DSA_LATENT), lambda b, i: (b, 0, 0), pipeline_mode=once),
                  pl.BlockSpec((DSA_HEADS, DSA_DH, DSA_LATENT), const3, pipeline_mode=once),
                  pl.BlockSpec((DSA_HEADS, DSA_LATENT, DSA_DV), const3, pipeline_mode=once),
                  pl.BlockSpec((DSA_HEADS, QBLK, 2 * QBLK), const3, pipeline_mode=once)],
        out_specs=pl.BlockSpec((QBLK, DSA_HEADS * DSA_DV), lambda b, i: (tokb(b, i), 0)),
        out_shape=jax.ShapeDtypeStruct((t, DSA_HEADS * DSA_DV), BF16),
        scratch_shapes=[pltpu.VMEM((QBLK, ngrp * GROUP_KEYS), I32),
                        pltpu.VMEM((32, QBLK, ngrp * 128), I32),
                        pltpu.VMEM((QBLK, ngrp * 128), I32),
                        pltpu.VMEM((hq, IDX_DIM), BF16),
                        pltpu.VMEM((hq, 128), F32),
                        pltpu.VMEM((hq, DSA_LATENT), BF16),
                        pltpu.VMEM((QBLK, KCH), F32),
                        pltpu.VMEM((hq, KCH), F32),
                        pltpu.VMEM((hq, KCH), F32),
                        pltpu.VMEM((hq, KCH), BF16),
                        pltpu.VMEM((hq, 128), F32),
                        pltpu.VMEM((hq, 128), F32),
                        pltpu.VMEM((hq, 128), F32),
                        pltpu.VMEM((hq, DSA_LATENT), F32)],
        compiler_params=_cparams(("parallel", "arbitrary")),
        name="dsa",
    )(proj, proj, misc, kidx, clat, w_uk, w_uv, band)


def _mix_kernel(yg_ref, yd_ref, gg_ref, gd_ref, wg_ref, wd_ref, wm_ref, g_ref, h_ref, o_ref):
    a = _dot(yg_ref[...], wg_ref[...])
    b = _dot(yd_ref[...], wd_ref[...])
    gg = jax.nn.sigmoid(gg_ref[...].astype(F32))
    gd = jax.nn.sigmoid(gd_ref[...].astype(F32))
    merged = (gg * a + gd * b).astype(BF16)
    o_ref[...] = h_ref[...] + _rms(_dot(merged, wm_ref[...]), g_ref[...])


def _mix(y_gla, y_dsa, proj, w_g, w_d, w_m, gain, h, tm):
    t, d = y_gla.shape
    once = pl.Buffered(1)
    row = lambda i: (i, 0)
    fixed = lambda i: (0, 0)
    return pl.pallas_call(
        _mix_kernel,
        grid=(t // tm,),
        in_specs=[pl.BlockSpec((tm, d), row),
                  pl.BlockSpec((tm, d), row),
                  pl.BlockSpec((tm, d), lambda i: (i, 5)),
                  pl.BlockSpec((tm, d), lambda i: (i, 6)),
                  pl.BlockSpec((d, d), fixed, pipeline_mode=once),
                  pl.BlockSpec((d, d), fixed, pipeline_mode=once),
                  pl.BlockSpec((d, d), fixed, pipeline_mode=once),
                  pl.BlockSpec((1, d), fixed),
                  pl.BlockSpec((tm, d), row)],
        out_specs=pl.BlockSpec((tm, d), row),
        out_shape=jax.ShapeDtypeStruct((t, d), F32),
        compiler_params=_cparams(("parallel",)),
        name="mix",
    )(y_gla, y_dsa, proj, proj, w_g, w_d, w_m, gain, h)


def _xa_kernel(h_ref, g_ref, wq_ref, k_ref, v_ref, wo_ref, pg_ref, o_ref):
    h = h_ref[...]
    q = _dot(_rms(h, g_ref[...]).astype(BF16), wq_ref[...]).astype(BF16)
    outs = []
    for hd in range(XA_HEADS):
        hs = slice(hd * XA_DH, (hd + 1) * XA_DH)
        s = _dot_nt(q[:, hs], k_ref[0][:, hs]) * (XA_DH ** -0.5)
        m = jnp.max(s, axis=-1, keepdims=True)
        p = jnp.exp(s - m)
        p = p / jnp.sum(p, axis=-1, keepdims=True)
        outs.append(_dot(p.astype(BF16), v_ref[0][:, hs]).astype(BF16))
    y = _dot(jnp.concatenate(outs, axis=1), wo_ref[...])
    o_ref[...] = h + _rms(y, pg_ref[...])


def _xa(h, pre_gain, w_q, kv, w_o, post_gain, batch, seq, tm):
    t, d = h.shape
    n_mem = kv.shape[1]
    nb = seq // tm
    once = pl.Buffered(1)
    row = lambda b, i: (b * nb + i, 0)
    fixed = lambda b, i: (0, 0)
    return pl.pallas_call(
        _xa_kernel,
        grid=(batch, nb),
        in_specs=[pl.BlockSpec((tm, d), row),
                  pl.BlockSpec((1, d), fixed),
                  pl.BlockSpec((d, d), fixed, pipeline_mode=once),
                  pl.BlockSpec((1, n_mem, d), lambda b, i: (b, 0, 0)),
                  pl.BlockSpec((1, n_mem, d), lambda b, i: (b, 0, 1)),
                  pl.BlockSpec((d, d), fixed, pipeline_mode=once),
                  pl.BlockSpec((1, d), fixed)],
        out_specs=pl.BlockSpec((tm, d), row),
        out_shape=jax.ShapeDtypeStruct((t, d), F32),
        compiler_params=_cparams(("parallel", "parallel")),
        name="xa",
    )(h, pre_gain, w_q, kv, kv, w_o, post_gain)


def _ffn_kernel(h_ref, g_ref, wg_ref, wu_ref, wd_ref, pg_ref, o_ref, u_ref):
    f = pl.program_id(1)

    @pl.when(f == 0)
    def _():
        u_ref[...] = _rms(h_ref[...], g_ref[...]).astype(BF16)
        o_ref[...] = jnp.zeros_like(o_ref)

    u = u_ref[...]
    n_sub = max(wg_ref.shape[1] // 256, 1)
    half = wg_ref.shape[1] // n_sub
    down = None
    for s in range(n_sub):
        cs = slice(s * half, (s + 1) * half)
        a = _dot(u, wg_ref[:, cs])
        b = _dot(u, wu_ref[:, cs])
        act = (a * jax.nn.sigmoid(a) * b).astype(BF16)
        d = _dot(act, wd_ref[cs, :])
        down = d if down is None else down + d
    o_ref[...] += down

    @pl.when(f == pl.num_programs(1) - 1)
    def _():
        o_ref[...] = h_ref[...] + _rms(o_ref[...], pg_ref[...])


def _ffn(h, pre_gain, w_gate, w_up, w_down, post_gain, tm, tf):
    t, d = h.shape
    ff = w_gate.shape[1]
    return pl.pallas_call(
        _ffn_kernel,
        grid=(t // tm, ff // tf),
        in_specs=[pl.BlockSpec((tm, d), lambda i, f: (i, 0)),
                  pl.BlockSpec((1, d), lambda i, f: (0, 0)),
                  pl.BlockSpec((d, tf), lambda i, f: (0, f)),
                  pl.BlockSpec((d, tf), lambda i, f: (0, f)),
                  pl.BlockSpec((tf, d), lambda i, f: (f, 0)),
                  pl.BlockSpec((1, d), lambda i, f: (0, 0))],
        out_specs=pl.BlockSpec((tm, d), lambda i, f: (i, 0)),
        out_shape=jax.ShapeDtypeStruct((t, d), F32),
        scratch_shapes=[pltpu.VMEM((tm, d), BF16)],
        compiler_params=_cparams(("parallel", "arbitrary")),
        name="ffn",
    )(h, pre_gain, w_gate, w_up, w_down, post_gain)


def _row(v):
    return v.reshape(1, -1).astype(F32)


def _layer(h, mem, w_in, gla_w_a2, gla_b_a, gla_out_norm, dsa_w_uk, dsa_w_uv, dsa_latent_norm,
           idx_k_norm_w, idx_k_norm_b, band, w_gla_branch, w_dsa_branch, w_mix_out,
           mix_pre_norm, mix_post_norm, xa_pre_norm, xa_post_norm, xa_mem_norm,
           w_xa_q, w_xa_kv, w_xa_o, ffn_pre_norm, ffn_post_norm, w_ffn_gate, w_ffn_up, w_ffn_down,
           batch, seq):
    d = D_MODEL
    t = batch * seq
    cols = lambda a, b: w_in[:, a:b]
    w_main = jnp.concatenate(
        [cols(_O_GQ, _O_GA), cols(_O_DQ, _O_DC), cols(_O_IQ, _O_IK), cols(_O_GG, _O_END)], axis=1).astype(BF16)
    w_small = jnp.concatenate(
        [cols(_O_DC, _O_IQ), cols(_O_IK, _O_IW), cols(_O_GA, _O_DQ), cols(_O_IW, _O_GG),
         jnp.zeros((d, SMALL_COLS - 416), w_in.dtype)], axis=1).astype(BF16)

    pre = _row(mix_pre_norm)
    proj = _norm_matmul(h, pre, w_main, BF16, 1024, 1024)
    clat, kidx, misc = _small_proj(h, pre, w_small, _row(dsa_latent_norm), _row(idx_k_norm_w),
                                   _row(idx_k_norm_b), batch, seq)

    y_gla = _gla(proj, misc, gla_w_a2.astype(BF16), _row(gla_b_a), _row(gla_out_norm), batch, seq)
    y_dsa = _dsa(proj, misc, kidx, clat, dsa_w_uk.astype(BF16), dsa_w_uv.astype(BF16), band, batch, seq)

    h = _mix(y_gla, y_dsa, proj, w_gla_branch.astype(BF16), w_dsa_branch.astype(BF16),
             w_mix_out.astype(BF16), _row(mix_post_norm), h, 256)

    n_mem = mem.shape[1]
    kv = _norm_matmul(mem.reshape(batch * n_mem, d), _row(xa_mem_norm), w_xa_kv.astype(BF16), BF16,
                      batch * n_mem, 512).reshape(batch, n_mem, 2 * d)
    h = _xa(h, _row(xa_pre_norm), w_xa_q.astype(BF16), kv, w_xa_o.astype(BF16), _row(xa_post_norm),
            batch, seq, 512)

    h = _ffn(h, _row(ffn_pre_norm), w_ffn_gate.astype(BF16), w_ffn_up.astype(BF16),
             w_ffn_down.astype(BF16), _row(ffn_post_norm), 1024, 256)
    return h


def kernel(x, mem, w_in, gla_w_a2, gla_b_a, gla_out_norm, dsa_w_uk, dsa_w_uv, dsa_latent_norm,
           idx_k_norm_w, idx_k_norm_b, rel_bias, w_gla_branch, w_dsa_branch, w_mix_out,
           mix_pre_norm, mix_post_norm, xa_pre_norm, xa_post_norm, xa_mem_norm,
           w_xa_q, w_xa_kv, w_xa_o, ffn_pre_norm, ffn_post_norm, w_ffn_gate, w_ffn_up, w_ffn_down):
    batch, seq, d = x.shape
    depth = w_in.shape[0]
    band = _bias_band(rel_bias.astype(F32))
    h = x.reshape(batch * seq, d)
    for l in range(depth):
        h = _layer(h, mem, w_in[l], gla_w_a2[l], gla_b_a[l], gla_out_norm[l], dsa_w_uk[l], dsa_w_uv[l],
                   dsa_latent_norm[l], idx_k_norm_w[l], idx_k_norm_b[l], band, w_gla_branch[l],
                   w_dsa_branch[l], w_mix_out[l], mix_pre_norm[l], mix_post_norm[l], xa_pre_norm[l],
                   xa_post_norm[l], xa_mem_norm[l], w_xa_q[l], w_xa_kv[l], w_xa_o[l], ffn_pre_norm[l],
                   ffn_post_norm[l], w_ffn_gate[l], w_ffn_up[l], w_ffn_down[l], batch, seq)
    return h.reshape(batch, seq, d)
```

```python
import functools
import math

import jax
import jax.numpy as jnp
import numpy as np
from jax import lax
from jax.experimental import pallas as pl
from jax.experimental.pallas import tpu as pltpu

F32 = jnp.float32
BF16 = jnp.bfloat16
I32 = jnp.int32

D_MODEL = 2048
CHUNK = 64
EPS = 1e-6

GLA_HEADS = 4
GLA_DK = 256
GLA_DV = 512
GLA_RANK = 16
GLA_TAU = 16.0
GLA_SUB = 1

DSA_HEADS = 16
DSA_DH = 128
DSA_DV = 128
DSA_LATENT = 256
IDX_HEADS = 16
IDX_DIM = 128
INDEX_TOPK = 256

REL_BUCKETS = 32
REL_MAX_DIST = 128

XA_HEADS = 4
XA_DH = 512

_SPLITS = (1024, 1024, 2048, 2048, 16, 2048, 256, 2048, 128, 16, 2048, 2048)
_OFFS = tuple(int(v) for v in np.cumsum((0,) + _SPLITS))
(_O_GQ, _O_GK, _O_GV, _O_GR, _O_GA, _O_DQ, _O_DC, _O_IQ, _O_IK, _O_IW, _O_GG, _O_GD, _O_END) = _OFFS

MAIN_COLS = 14336
SMALL_COLS = 512

QBLK = 128
KCH = 512
GROUP_KEYS = 32 * 128
PAD_FRONT = KCH
INT_MIN = -2 ** 31
NEG = -1e30
LOG2E = 1.4426950408889634

VMEM_LIMIT = 58 * 1024 * 1024


def _cparams(sem):
    return pltpu.CompilerParams(dimension_semantics=sem, vmem_limit_bytes=VMEM_LIMIT)


def _rms(x, gain):
    ms = jnp.mean(x * x, axis=-1, keepdims=True)
    return x * lax.rsqrt(ms + EPS) * gain


def _dot(a, b):
    return jnp.dot(a, b, preferred_element_type=F32)


def _dot_nt(a, b):
    return lax.dot_general(a, b, (((1,), (1,)), ((), ())), preferred_element_type=F32)


def _norm_matmul_kernel(x_ref, g_ref, w_ref, o_ref, u_ref):
    @pl.when(pl.program_id(1) == 0)
    def _():
        u_ref[...] = _rms(x_ref[...], g_ref[...]).astype(BF16)

    o_ref[...] = _dot(u_ref[...], w_ref[0]).astype(o_ref.dtype)


def _col_tiles(w, tn):
    d, n = w.shape
    return w.reshape(d, n // tn, tn).transpose(1, 0, 2)


def _norm_matmul(x, gain, w, out_dtype, tm, tn):
    t, d = x.shape
    n = w.shape[1]
    return pl.pallas_call(
        _norm_matmul_kernel,
        grid=(t // tm, n // tn),
        in_specs=[pl.BlockSpec((tm, d), lambda i, j: (i, 0)),
                  pl.BlockSpec((1, d), lambda i, j: (0, 0)),
                  pl.BlockSpec((1, d, tn), lambda i, j: (j, 0, 0))],
        out_specs=pl.BlockSpec((tm, tn), lambda i, j: (i, j)),
        out_shape=jax.ShapeDtypeStruct((t, n), out_dtype),
        scratch_shapes=[pltpu.VMEM((tm, d), BF16)],
        compiler_params=_cparams(("parallel", "arbitrary")),
        name="norm_matmul",
    )(x, gain, _col_tiles(w, tn))


def _small_proj(x, gain, w_small, lat_g, ik_w, ik_b, batch, seq):
    tm = PAD_FRONT
    assert seq % tm == 0
    nblk = seq // tm
    npad = PAD_FRONT + seq
    grid = (batch, 1 + nblk)

    def x_map(b, i):
        return (b * nblk + jnp.maximum(i - 1, 0), 0)

    def kernel(x_ref, g_ref, w_ref, lat_g_ref, ik_w_ref, ik_b_ref, c_ref, k_ref, misc_ref):
        is_pad = pl.program_id(1) == 0

        @pl.when(is_pad)
        def _():
            c_ref[...] = jnp.zeros_like(c_ref)
            k_ref[...] = jnp.zeros_like(k_ref)
            misc_ref[...] = jnp.zeros_like(misc_ref)

        @pl.when(jnp.logical_not(is_pad))
        def _():
            u = _rms(x_ref[...], g_ref[...]).astype(BF16)
            p = _dot(u, w_ref[...])
            c_ref[0] = _rms(p[:, 0:256], lat_g_ref[...]).astype(BF16)
            ik = p[:, 256:384]
            mu = jnp.mean(ik, axis=-1, keepdims=True)
            xc = ik - mu
            var = jnp.mean(xc * xc, axis=-1, keepdims=True)
            k_ref[0] = (xc * lax.rsqrt(var + EPS) * ik_w_ref[...] + ik_b_ref[...]).astype(BF16)
            misc_ref[0] = p[:, 384:512]

    d = x.shape[1]
    const = lambda b, i: (0, 0)
    return pl.pallas_call(
        kernel,
        grid=grid,
        in_specs=[pl.BlockSpec((tm, d), x_map),
                  pl.BlockSpec((1, d), const),
                  pl.BlockSpec((d, SMALL_COLS), const),
                  pl.BlockSpec((1, DSA_LATENT), const),
                  pl.BlockSpec((1, IDX_DIM), const),
                  pl.BlockSpec((1, IDX_DIM), const)],
        out_specs=[pl.BlockSpec((1, tm, DSA_LATENT), lambda b, i: (b, i, 0)),
                   pl.BlockSpec((1, tm, IDX_DIM), lambda b, i: (b, i, 0)),
                   pl.BlockSpec((1, tm, 128), lambda b, i: (b, i, 0))],
        out_shape=[jax.ShapeDtypeStruct((batch, npad, DSA_LATENT), BF16),
                   jax.ShapeDtypeStruct((batch, npad, IDX_DIM), BF16),
                   jax.ShapeDtypeStruct((batch, npad, 128), F32)],
        compiler_params=_cparams(("parallel", "arbitrary")),
        name="small_proj",
    )(x, gain, w_small, lat_g, ik_w, ik_b)


def _log_sigmoid(z):
    return jnp.minimum(z, 0.0) - jnp.log1p(jnp.exp(-jnp.abs(z)))


def _split3(x):
    h = x.astype(BF16)
    r = x - h.astype(F32)
    m = r.astype(BF16)
    l = (r - m.astype(F32)).astype(BF16)
    return h, m, l


def _gla_kernel(nb, q_ref, k_ref, v_ref, r_ref, misc_ref, wa2_ref, ba_ref, on_ref, o_ref, state_ref):
    @pl.when(pl.program_id(0) == 0)
    def _():
        state_ref[...] = jnp.zeros_like(state_ref)

    rows = nb * GLA_SUB * CHUNK
    stack = lambda ref, cols: jnp.concatenate([ref[i][:, cols] for i in range(nb)], axis=0)
    row = lax.broadcasted_iota(I32, (rows, rows), 0)
    col = lax.broadcasted_iota(I32, (rows, rows), 1)
    same = (row // CHUNK) == (col // CHUNK)
    lower = jnp.logical_and(same, col <= row)
    upper = jnp.logical_and(same, col > row)
    tril = jnp.where(lower, 1.0, 0.0).astype(BF16)

    a_low = stack(misc_ref, slice(0, GLA_RANK)).astype(BF16)
    z = _dot(a_low, wa2_ref[...]) + ba_ref[...]
    la = _log_sigmoid(z) * (1.0 / GLA_TAU)
    l_h, l_m, l_l = _split3(la)
    b_all = _dot(tril, l_h) + _dot(tril, l_m) + _dot(tril, l_l)

    for h in range(GLA_HEADS):
        ks = slice(h * GLA_DK, (h + 1) * GLA_DK)
        vs = slice(h * GLA_DV, (h + 1) * GLA_DV)
        b = b_all[:, ks]
        eb = jnp.exp(b)
        ebi = jnp.exp(-b)
        q = stack(q_ref, ks).astype(F32) * (GLA_DK ** -0.5)
        k = stack(k_ref, ks).astype(F32)
        v = stack(v_ref, vs)
        q_fwd = (q * eb).astype(BF16)
        a_lo = _dot_nt(q_fwd, (k * ebi).astype(BF16))
        a_up = _dot_nt((q * ebi).astype(BF16), (k * eb).astype(BF16))
        scores = jnp.where(lower, a_lo, jnp.where(upper, a_up, 0.0)).astype(BF16)
        o_intra = _dot(scores, v)
        outs = []
        for i in range(nb):
            st = state_ref[i * GLA_HEADS + h]
            for sub in range(GLA_SUB):
                rs = slice((i * GLA_SUB + sub) * CHUNK, (i * GLA_SUB + sub + 1) * CHUNK)
                b_i = b[rs]
                b_last = b_i[CHUNK - 1:CHUNK, :]
                outs.append(o_intra[rs] + _dot_nt(q_fwd[rs], st.astype(BF16)))
                k_dec = (k[rs] * jnp.exp(b_last - b_i)).astype(BF16)
                v_t = v[rs].astype(F32).T.astype(BF16)
                st = st * jnp.exp(b_last) + _dot(v_t, k_dec)
            state_ref[i * GLA_HEADS + h] = st
        o = _rms(jnp.concatenate(outs, axis=0), on_ref[...])
        r = stack(r_ref, vs).astype(F32)
        y = (o * (r * jax.nn.sigmoid(r))).astype(o_ref.dtype)
        blk = GLA_SUB * CHUNK
        for i in range(nb):
            o_ref[i, :, vs] = y[i * blk:(i + 1) * blk]


def _gla(proj, misc, w_a2, b_a, out_norm, batch, seq):
    blk = GLA_SUB * CHUNK
    assert seq % blk == 0 and PAD_FRONT % blk == 0
    proj3 = proj.reshape(batch, seq, proj.shape[1])
    y = pl.pallas_call(
        functools.partial(_gla_kernel, batch),
        grid=(seq // blk,),
        in_specs=[pl.BlockSpec((batch, blk, 1024), lambda c: (0, c, 0)),
                  pl.BlockSpec((batch, blk, 1024), lambda c: (0, c, 1)),
                  pl.BlockSpec((batch, blk, 2048), lambda c: (0, c, 1)),
                  pl.BlockSpec((batch, blk, 2048), lambda c: (0, c, 2)),
                  pl.BlockSpec((batch, blk, 128), lambda c: (0, c + PAD_FRONT // blk, 0)),
                  pl.BlockSpec((GLA_RANK, GLA_HEADS * GLA_DK), lambda c: (0, 0)),
                  pl.BlockSpec((1, GLA_HEADS * GLA_DK), lambda c: (0, 0)),
                  pl.BlockSpec((1, GLA_DV), lambda c: (0, 0))],
        out_specs=pl.BlockSpec((batch, blk, GLA_HEADS * GLA_DV), lambda c: (0, c, 0)),
        out_shape=jax.ShapeDtypeStruct((batch, seq, GLA_HEADS * GLA_DV), BF16),
        scratch_shapes=[pltpu.VMEM((batch * GLA_HEADS, GLA_DV, GLA_DK), F32)],
        compiler_params=_cparams(("arbitrary",)),
        name="gla",
    )(proj3, proj3, proj3, proj3, misc, w_a2, b_a, out_norm)
    return y.reshape(batch * seq, GLA_HEADS * GLA_DV)


def _t5_bucket(rel):
    half = REL_BUCKETS // 2
    max_exact = half // 2
    ret = jnp.where(rel > 0, half, 0)
    n = jnp.abs(rel)
    nf = jnp.maximum(n, 1).astype(jnp.float32)
    large = max_exact + (jnp.log(nf / max_exact) / math.log(REL_MAX_DIST / max_exact)
                         * (half - max_exact)).astype(jnp.int32)
    large = jnp.minimum(large, half - 1)
    return ret + jnp.where(n < max_exact, n, large)


def _bias_band_kernel(bucket_ref, rb_ref, o_ref):
    far = REL_BUCKETS // 2 - 1
    bucket = bucket_ref[...]
    for h in range(DSA_HEADS):
        acc = jnp.zeros(bucket.shape, F32)
        for b in range(REL_BUCKETS):
            acc = jnp.where(bucket == b, rb_ref[b, h], acc)
        o_ref[h] = (acc - rb_ref[far, h]) * LOG2E


def _bias_band(rel_bias):
    t = jnp.arange(QBLK, dtype=jnp.int32)[:, None]
    j = jnp.arange(2 * QBLK, dtype=jnp.int32)[None, :]
    bucket = _t5_bucket(j - QBLK - t).astype(jnp.int32)
    return pl.pallas_call(
        _bias_band_kernel,
        in_specs=[pl.BlockSpec(memory_space=pltpu.VMEM), pl.BlockSpec(memory_space=pltpu.SMEM)],
        out_specs=pl.BlockSpec(memory_space=pltpu.VMEM),
        out_shape=jax.ShapeDtypeStruct((DSA_HEADS, QBLK, 2 * QBLK), F32),
        name="bias_band",
    )(bucket, rel_bias)


def _transpose_bits32(words):
    a = list(words)
    j, m = 16, 0x0000FFFF
    while j:
        mask = jnp.int32(m - (1 << 32) if m >= (1 << 31) else m)
        k = 0
        while k < 32:
            t = (a[k] ^ lax.shift_right_logical(a[k + j], jnp.int32(j))) & mask
            a[k] = a[k] ^ t
            a[k + j] = a[k + j] ^ (t << j)
            k = (k + j + 1) & ~j
        j >>= 1
        m = (m ^ (m << j)) & 0xFFFFFFFF
    return a


def _sortable(x):
    i = pltpu.bitcast(x, I32)
    return jnp.where(i < 0, i ^ jnp.int32(0x7FFFFFFF), i)


def _dsa_kernel(top_k, dq_ref, iq_ref, misc_ref, kidx_ref, clat_ref, wuk_ref, wuv_ref, band_ref,
                o_ref, keys_ref, planes_ref, eq_ref, iqs_ref, wb_ref, qlat_ref, madd_ref, s_ref, s2_ref, p_ref, alpha_ref,
                m_ref, l_ref, acc_ref):
    qb = pl.program_id(1)
    start = qb * QBLK
    hrows = lambda h: slice(h * QBLK, (h + 1) * QBLK)

    w_scale = IDX_HEADS ** -0.5 * IDX_DIM ** -0.5
    wq = misc_ref[0][:, GLA_RANK:GLA_RANK + IDX_HEADS] * w_scale
    for h in range(IDX_HEADS):
        wb_ref[hrows(h), :] = jnp.broadcast_to(wq[:, h:h + 1], (QBLK, 128))
        iqs_ref[hrows(h), :] = iq_ref[:, h * IDX_DIM:(h + 1) * IDX_DIM]

    row = lax.broadcasted_iota(I32, (QBLK, KCH), 0)
    lane = lax.broadcasted_iota(I32, (QBLK, KCH), 1)
    p_lim = start + (row // CHUNK + 1) * CHUNK + PAD_FRONT

    n_chunks = (start + PAD_FRONT + QBLK + KCH - 1) // KCH
    keys_ref[:, 0:KCH] = jnp.full((QBLK, KCH), INT_MIN, I32)

    def index_chunk(c, dots_ref):
        off = pl.multiple_of(c * KCH, KCH)
        kc = kidx_ref[0, pl.ds(off, KCH), :]
        dots_ref[...] = _dot_nt(iqs_ref[...], kc)
        acc = jnp.zeros((QBLK, KCH), F32)
        for h in range(IDX_HEADS):
            wbh = wb_ref[hrows(h), :]
            acc = acc + jnp.concatenate([wbh] * (KCH // 128), axis=1) * jnp.maximum(dots_ref[hrows(h), :], 0.0)
        keys_ref[:, pl.ds(off, KCH)] = jnp.where(lane + off < p_lim, _sortable(acc), INT_MIN)

    n_odd = (n_chunks - 1) % 2

    @pl.when(n_odd == 1)
    def _():
        index_chunk(1, s_ref)

    def idx_body(i, carry):
        c = 1 + n_odd + 2 * i
        index_chunk(c, s_ref)
        index_chunk(c + 1, s2_ref)
        return carry

    lax.fori_loop(0, (n_chunks - 1) // 2, idx_body, 0)

    n_groups = (n_chunks * KCH + GROUP_KEYS - 1) // GROUP_KEYS
    ngrp_max = eq_ref.shape[1] // 128

    def pad_body(c, carry):
        keys_ref[:, pl.ds(pl.multiple_of(c * KCH, KCH), KCH)] = jnp.full((QBLK, KCH), INT_MIN, I32)
        return carry

    lax.fori_loop(n_chunks, n_groups * (GROUP_KEYS // KCH), pad_body, 0)

    def plane_body(idx, carry):
        g = idx // (QBLK // 16)
        base = g * GROUP_KEYS
        for half in range(2):
            r0 = pl.multiple_of((idx % (QBLK // 16)) * 16 + half * 8, 8)
            words = [keys_ref[pl.ds(r0, 8), pl.ds(pl.multiple_of(base + j * 128, 128), 128)]
                     for j in range(32)]
            words = _transpose_bits32(words)
            words[0] = ~words[0]
            for i in range(32):
                planes_ref[31 - i, pl.ds(r0, 8), pl.ds(pl.multiple_of(g * 128, 128), 128)] = words[i]
        return carry

    lax.fori_loop(0, n_groups * (QBLK // 16), plane_body, 0)

    for g in range(ngrp_max):
        eq_ref[:, g * 128:(g + 1) * 128] = jnp.broadcast_to(jnp.where(g < n_groups, -1, 0), (QBLK, 128))

    def row_count(t):
        pc = lax.population_count(t)
        tot = pc[:, 0:128]
        for g in range(1, ngrp_max):
            tot = tot + pc[:, g * 128:(g + 1) * 128]
        return jnp.broadcast_to(jnp.sum(tot.astype(F32), axis=1, keepdims=True), (QBLK, 128))

    def pair_body(i, carry):
        prefix, above = carry
        b0 = 30 - 2 * i
        p1 = planes_ref[b0 + 1]
        p0 = planes_ref[b0]
        eq = eq_ref[...]
        e1 = eq & p1
        e0 = eq & ~p1
        t11 = e1 & p0
        t10 = e1 & ~p0
        t01 = e0 & p0
        t00 = e0 & ~p0
        s3 = above + row_count(t11)
        s2 = s3 + row_count(t10)
        s1 = s2 + row_count(t01)
        is3 = s3 >= top_k
        is2 = s2 >= top_k
        is1 = s1 >= top_k
        for g in range(ngrp_max):
            gs = slice(g * 128, (g + 1) * 128)
            eq_ref[:, gs] = jnp.where(is3, t11[:, gs], jnp.where(is2, t10[:, gs], jnp.where(is1, t01[:, gs], t00[:, gs])))
        above = jnp.where(is3, above, jnp.where(is2, s3, jnp.where(is1, s2, s1)))
        digit = jnp.where(is3, 3, jnp.where(is2, 2, jnp.where(is1, 1, 0)))
        return prefix | (digit << b0), above

    end = start + PAD_FRONT + QBLK
    n_att = (start + QBLK + KCH - 1) // KCH

    def chunk_off(j):
        return pl.multiple_of(end - KCH * (j + 1), 128)

    def absorbed_query(h):
        ql = _dot(dq_ref[:, h * DSA_DH:(h + 1) * DSA_DH], wuk_ref[h]) * (LOG2E * DSA_DH ** -0.5)
        qlat_ref[hrows(h), :] = ql.astype(BF16)

    def first_scores(h):
        s_ref[hrows(h), :] = _dot_nt(qlat_ref[hrows(h), :], clat_ref[0, pl.ds(chunk_off(0), KCH), :])

    carry = (jnp.zeros((QBLK, 128), I32), jnp.zeros((QBLK, 128), F32))
    for i in range(16):
        carry = pair_body(i, carry)
        for h in range(i * DSA_HEADS // 16, (i + 1) * DSA_HEADS // 16):
            absorbed_query(h)
            if h > 0:
                first_scores(h - 1)
    first_scores(DSA_HEADS - 1)
    thr = carry[0] ^ INT_MIN
    thr = jnp.maximum(thr, INT_MIN + 1)

    m_ref[...] = jnp.full(m_ref.shape, NEG, F32)
    l_ref[...] = jnp.zeros(l_ref.shape, F32)
    acc_ref[...] = jnp.zeros(acc_ref.shape, F32)

    def scores(j, dst_ref):
        cc = clat_ref[0, pl.ds(chunk_off(j), KCH), :]
        dst_ref[...] = _dot_nt(qlat_ref[...], cc)

    def softmax_update(j, src_ref, near):
        off = chunk_off(j)
        kk = keys_ref[:, pl.ds(off, KCH)]
        madd_ref[...] = jnp.where(kk >= jnp.concatenate([thr] * (KCH // 128), axis=1), 0.0, NEG)
        cc = clat_ref[0, pl.ds(off, KCH), :]
        for h in range(DSA_HEADS):
            s = src_ref[hrows(h), :] + madd_ref[...]
            if near:
                s = jnp.concatenate([s[:, :KCH - 2 * QBLK], s[:, KCH - 2 * QBLK:] + band_ref[h]], axis=1)
            m_old = m_ref[hrows(h), :]
            m_new = jnp.maximum(m_old, jnp.broadcast_to(jnp.max(s, axis=1, keepdims=True), (QBLK, 128)))
            alpha = jnp.exp2(m_old - m_new)
            p = jnp.exp2(s - jnp.concatenate([m_new] * (KCH // 128), axis=1))
            l_ref[hrows(h), :] = (alpha * l_ref[hrows(h), :]
                                  + jnp.broadcast_to(jnp.sum(p, axis=1, keepdims=True), (QBLK, 128)))
            m_ref[hrows(h), :] = m_new
            alpha_ref[hrows(h), :] = alpha
            p_ref[hrows(h), :] = p.astype(BF16)
        al = alpha_ref[...]
        acc_ref[...] = (jnp.concatenate([al] * (DSA_LATENT // 128), axis=1) * acc_ref[...]
                        + _dot(p_ref[...], cc))

    def step(j, cur_ref, nxt_ref, near):
        scores(jnp.minimum(j + 1, n_att - 1), nxt_ref)
        softmax_update(j, cur_ref, near)

    step(0, s_ref, s2_ref, True)

    def att_body(i, carry):
        j = 2 * i + 1
        step(j, s2_ref, s_ref, False)
        step(j + 1, s_ref, s2_ref, False)
        return carry

    lax.fori_loop(0, (n_att - 1) // 2, att_body, 0)

    @pl.when((n_att - 1) % 2 == 1)
    def _():
        step(n_att - 1, s2_ref, s_ref, False)

    for h in range(DSA_HEADS):
        inv = 1.0 / l_ref[hrows(h), :]
        o_lat = acc_ref[hrows(h), :] * jnp.concatenate([inv] * (DSA_LATENT // 128), axis=1)
        o_ref[:, h * DSA_DV:(h + 1) * DSA_DV] = _dot(o_lat.astype(BF16), wuv_ref[h]).astype(o_ref.dtype)


def _dsa(proj, misc, kidx, clat, w_uk, w_uv, band, batch, seq):
    nqb = seq // QBLK
    t = batch * seq
    npad = clat.shape[1]
    top_k = min(INDEX_TOPK, seq // 4)
    hq = DSA_HEADS * QBLK
    tokb = lambda b, i: b * nqb + i
    const3 = lambda b, i: (0, 0, 0)
    ngrp = (npad + GROUP_KEYS - 1) // GROUP_KEYS
    once = pl.Buffered(1)
    return pl.pallas_call(
        functools.partial(_dsa_kernel, top_k),
        grid=(batch, nqb),
        in_specs=[pl.BlockSpec((QBLK, 2048), lambda b, i: (tokb(b, i), 3)),
                  pl.BlockSpec((QBLK, 2048), lambda b, i: (tokb(b, i), 4)),
                  pl.BlockSpec((1, QBLK, 128), lambda b, i: (b, i + PAD_FRONT // QBLK, 0)),
                  pl.BlockSpec((1, npad, IDX_DIM), lambda b, i: (b, 0, 0), pipeline_mode=once),
                  pl.BlockSpec((1, npad, DSA_LATENT), lambda b, i: (b, 0, 0), pipeline_mode=once),
                  pl.BlockSpec((DSA_HEADS, DSA_DH, DSA_LATENT), const3, pipeline_mode=once),
                  pl.BlockSpec((DSA_HEADS, DSA_LATENT, DSA_DV), const3, pipeline_mode=once),
                  pl.BlockSpec((DSA_HEADS, QBLK, 2 * QBLK), const3, pipeline_mode=once)],
        out_specs=pl.BlockSpec((QBLK, DSA_HEADS * DSA_DV), lambda b, i: (tokb(b, i), 0)),
        out_shape=jax.ShapeDtypeStruct((t, DSA_HEADS * DSA_DV), BF16),
        scratch_shapes=[pltpu.VMEM((QBLK, ngrp * GROUP_KEYS), I32),
                        pltpu.VMEM((32, QBLK, ngrp * 128), I32),
                        pltpu.VMEM((QBLK, ngrp * 128), I32),
                        pltpu.VMEM((hq, IDX_DIM), BF16),
                        pltpu.VMEM((hq, 128), F32),
                        pltpu.VMEM((hq, DSA_LATENT), BF16),
                        pltpu.VMEM((QBLK, KCH), F32),
                        pltpu.VMEM((hq, KCH), F32),
                        pltpu.VMEM((hq, KCH), F32),
                        pltpu.VMEM((hq, KCH), BF16),
                        pltpu.VMEM((hq, 128), F32),
                        pltpu.VMEM((hq, 128), F32),
                        pltpu.VMEM((hq, 128), F32),
                        pltpu.VMEM((hq, DSA_LATENT), F32)],
        compiler_params=_cparams(("parallel", "arbitrary")),
        name="dsa",
    )(proj, proj, misc, kidx, clat, w_uk, w_uv, band)


def _mix_kernel(yg_ref, yd_ref, gg_ref, gd_ref, wg_ref, wd_ref, wm_ref, g_ref, h_ref, o_ref):
    a = _dot(yg_ref[...], wg_ref[...])
    b = _dot(yd_ref[...], wd_ref[...])
    gg = jax.nn.sigmoid(gg_ref[...].astype(F32))
    gd = jax.nn.sigmoid(gd_ref[...].astype(F32))
    merged = (gg * a + gd * b).astype(BF16)
    o_ref[...] = h_ref[...] + _rms(_dot(merged, wm_ref[...]), g_ref[...])


def _mix(y_gla, y_dsa, proj, w_g, w_d, w_m, gain, h, tm):
    t, d = y_gla.shape
    once = pl.Buffered(1)
    row = lambda i: (i, 0)
    fixed = lambda i: (0, 0)
    return pl.pallas_call(
        _mix_kernel,
        grid=(t // tm,),
        in_specs=[pl.BlockSpec((tm, d), row),
                  pl.BlockSpec((tm, d), row),
                  pl.BlockSpec((tm, d), lambda i: (i, 5)),
                  pl.BlockSpec((tm, d), lambda i: (i, 6)),
                  pl.BlockSpec((d, d), fixed, pipeline_mode=once),
                  pl.BlockSpec((d, d), fixed, pipeline_mode=once),
                  pl.BlockSpec((d, d), fixed, pipeline_mode=once),
                  pl.BlockSpec((1, d), fixed),
                  pl.BlockSpec((tm, d), row)],
        out_specs=pl.BlockSpec((tm, d), row),
        out_shape=jax.ShapeDtypeStruct((t, d), F32),
        compiler_params=_cparams(("parallel",)),
        name="mix",
    )(y_gla, y_dsa, proj, proj, w_g, w_d, w_m, gain, h)


def _xa_kernel(h_ref, g_ref, wq_ref, k_ref, v_ref, wo_ref, pg_ref, o_ref):
    h = h_ref[...]
    q = _dot(_rms(h, g_ref[...]).astype(BF16), wq_ref[...]).astype(BF16)
    outs = []
    for hd in range(XA_HEADS):
        hs = slice(hd * XA_DH, (hd + 1) * XA_DH)
        s = _dot_nt(q[:, hs], k_ref[0][:, hs]) * (XA_DH ** -0.5)
        m = jnp.max(s, axis=-1, keepdims=True)
        p = jnp.exp(s - m)
        p = p / jnp.sum(p, axis=-1, keepdims=True)
        outs.append(_dot(p.astype(BF16), v_ref[0][:, hs]).astype(BF16))
    y = _dot(jnp.concatenate(outs, axis=1), wo_ref[...])
    o_ref[...] = h + _rms(y, pg_ref[...])


def _xa(h, pre_gain, w_q, kv, w_o, post_gain, batch, seq, tm):
    t, d = h.shape
    n_mem = kv.shape[1]
    nb = seq // tm
    once = pl.Buffered(1)
    row = lambda b, i: (b * nb + i, 0)
    fixed = lambda b, i: (0, 0)
    return pl.pallas_call(
        _xa_kernel,
        grid=(batch, nb),
        in_specs=[pl.BlockSpec((tm, d), row),
                  pl.BlockSpec((1, d), fixed),
                  pl.BlockSpec((d, d), fixed, pipeline_mode=once),
                  pl.BlockSpec((1, n_mem, d), lambda b, i: (b, 0, 0)),
                  pl.BlockSpec((1, n_mem, d), lambda b, i: (b, 0, 1)),
                  pl.BlockSpec((d, d), fixed, pipeline_mode=once),
                  pl.BlockSpec((1, d), fixed)],
        out_specs=pl.BlockSpec((tm, d), row),
        out_shape=jax.ShapeDtypeStruct((t, d), F32),
        compiler_params=_cparams(("parallel", "parallel")),
        name="xa",
    )(h, pre_gain, w_q, kv, kv, w_o, post_gain)


def _ffn_kernel(h_ref, g_ref, wg_ref, wu_ref, wd_ref, pg_ref, o_ref, u_ref):
    f = pl.program_id(1)

    @pl.when(f == 0)
    def _():
        u_ref[...] = _rms(h_ref[...], g_ref[...]).astype(BF16)
        o_ref[...] = jnp.zeros_like(o_ref)

    u = u_ref[...]
    n_sub = max(wg_ref.shape[2] // 256, 1)
    half = wg_ref.shape[2] // n_sub
    down = None
    for s in range(n_sub):
        cs = slice(s * half, (s + 1) * half)
        a = _dot(u, wg_ref[0][:, cs])
        b = _dot(u, wu_ref[0][:, cs])
        act = (a * jax.nn.sigmoid(a) * b).astype(BF16)
        d = _dot(act, wd_ref[cs, :])
        down = d if down is None else down + d
    o_ref[...] += down

    @pl.when(f == pl.num_programs(1) - 1)
    def _():
        o_ref[...] = h_ref[...] + _rms(o_ref[...], pg_ref[...])


def _ffn(h, pre_gain, w_gate, w_up, w_down, post_gain, tm, tf):
    t, d = h.shape
    ff = w_gate.shape[1]
    return pl.pallas_call(
        _ffn_kernel,
        grid=(t // tm, ff // tf),
        in_specs=[pl.BlockSpec((tm, d), lambda i, f: (i, 0)),
                  pl.BlockSpec((1, d), lambda i, f: (0, 0)),
                  pl.BlockSpec((1, d, tf), lambda i, f: (f, 0, 0)),
                  pl.BlockSpec((1, d, tf), lambda i, f: (f, 0, 0)),
                  pl.BlockSpec((tf, d), lambda i, f: (f, 0)),
                  pl.BlockSpec((1, d), lambda i, f: (0, 0))],
        out_specs=pl.BlockSpec((tm, d), lambda i, f: (i, 0)),
        out_shape=jax.ShapeDtypeStruct((t, d), F32),
        scratch_shapes=[pltpu.VMEM((tm, d), BF16)],
        compiler_params=_cparams(("parallel", "arbitrary")),
        name="ffn",
    )(h, pre_gain, _col_tiles(w_gate, tf), _col_tiles(w_up, tf), w_down, post_gain)


def _row(v):
    return v.reshape(1, -1).astype(F32)


def _layer(h, mem, w_in, gla_w_a2, gla_b_a, gla_out_norm, dsa_w_uk, dsa_w_uv, dsa_latent_norm,
           idx_k_norm_w, idx_k_norm_b, band, w_gla_branch, w_dsa_branch, w_mix_out,
           mix_pre_norm, mix_post_norm, xa_pre_norm, xa_post_norm, xa_mem_norm,
           w_xa_q, w_xa_kv, w_xa_o, ffn_pre_norm, ffn_post_norm, w_ffn_gate, w_ffn_up, w_ffn_down,
           batch, seq):
    d = D_MODEL
    t = batch * seq
    cols = lambda a, b: w_in[:, a:b]
    w_main = jnp.concatenate(
        [cols(_O_GQ, _O_GA), cols(_O_DQ, _O_DC), cols(_O_IQ, _O_IK), cols(_O_GG, _O_END)], axis=1).astype(BF16)
    w_small = jnp.concatenate(
        [cols(_O_DC, _O_IQ), cols(_O_IK, _O_IW), cols(_O_GA, _O_DQ), cols(_O_IW, _O_GG),
         jnp.zeros((d, SMALL_COLS - 416), w_in.dtype)], axis=1).astype(BF16)

    pre = _row(mix_pre_norm)
    proj = _norm_matmul(h, pre, w_main, BF16, 1024, 1024)
    clat, kidx, misc = _small_proj(h, pre, w_small, _row(dsa_latent_norm), _row(idx_k_norm_w),
                                   _row(idx_k_norm_b), batch, seq)

    y_gla = _gla(proj, misc, gla_w_a2.astype(BF16), _row(gla_b_a), _row(gla_out_norm), batch, seq)
    y_dsa = _dsa(proj, misc, kidx, clat, dsa_w_uk.astype(BF16), dsa_w_uv.astype(BF16), band, batch, seq)

    h = _mix(y_gla, y_dsa, proj, w_gla_branch.astype(BF16), w_dsa_branch.astype(BF16),
             w_mix_out.astype(BF16), _row(mix_post_norm), h, 256)

    n_mem = mem.shape[1]
    kv = _norm_matmul(mem.reshape(batch * n_mem, d), _row(xa_mem_norm), w_xa_kv.astype(BF16), BF16,
                      batch * n_mem, 512).reshape(batch, n_mem, 2 * d)
    h = _xa(h, _row(xa_pre_norm), w_xa_q.astype(BF16), kv, w_xa_o.astype(BF16), _row(xa_post_norm),
            batch, seq, 512)

    h = _ffn(h, _row(ffn_pre_norm), w_ffn_gate.astype(BF16), w_ffn_up.astype(BF16),
             w_ffn_down.astype(BF16), _row(ffn_post_norm), 1024, 256)
    return h


def kernel(x, mem, w_in, gla_w_a2, gla_b_a, gla_out_norm, dsa_w_uk, dsa_w_uv, dsa_latent_norm,
           idx_k_norm_w, idx_k_norm_b, rel_bias, w_gla_branch, w_dsa_branch, w_mix_out,
           mix_pre_norm, mix_post_norm, xa_pre_norm, xa_post_norm, xa_mem_norm,
           w_xa_q, w_xa_kv, w_xa_o, ffn_pre_norm, ffn_post_norm, w_ffn_gate, w_ffn_up, w_ffn_down):
    batch, seq, d = x.shape
    depth = w_in.shape[0]
    band = _bias_band(rel_bias.astype(F32))
    h = x.reshape(batch * seq, d)
    for l in range(depth):
        h = _layer(h, mem, w_in[l], gla_w_a2[l], gla_b_a[l], gla_out_norm[l], dsa_w_uk[l], dsa_w_uv[l],
                   dsa_latent_norm[l], idx_k_norm_w[l], idx_k_norm_b[l], band, w_gla_branch[l],
                   w_dsa_branch[l], w_mix_out[l], mix_pre_norm[l], mix_post_norm[l], xa_pre_norm[l],
                   xa_post_norm[l], xa_mem_norm[l], w_xa_q[l], w_xa_kv[l], w_xa_o[l], ffn_pre_norm[l],
                   ffn_post_norm[l], w_ffn_gate[l], w_ffn_up[l], w_ffn_down[l], batch, seq)
    return h.reshape(batch, seq, d)
```

```python
import functools
import math

import jax
import jax.numpy as jnp
import numpy as np
from jax import lax
from jax.experimental import pallas as pl
from jax.experimental.pallas import tpu as pltpu

F32 = jnp.float32
BF16 = jnp.bfloat16
I32 = jnp.int32

D_MODEL = 2048
CHUNK = 64
EPS = 1e-6

GLA_HEADS = 4
GLA_DK = 256
GLA_DV = 512
GLA_RANK = 16
GLA_TAU = 16.0
GLA_SUB = 1

DSA_HEADS = 16
DSA_DH = 128
DSA_DV = 128
DSA_LATENT = 256
IDX_HEADS = 16
IDX_DIM = 128
INDEX_TOPK = 256

REL_BUCKETS = 32
REL_MAX_DIST = 128

XA_HEADS = 4
XA_DH = 512

_SPLITS = (1024, 1024, 2048, 2048, 16, 2048, 256, 2048, 128, 16, 2048, 2048)
_OFFS = tuple(int(v) for v in np.cumsum((0,) + _SPLITS))
(_O_GQ, _O_GK, _O_GV, _O_GR, _O_GA, _O_DQ, _O_DC, _O_IQ, _O_IK, _O_IW, _O_GG, _O_GD, _O_END) = _OFFS

MAIN_COLS = 14336
SMALL_COLS = 512

QBLK = 128
KCH = 512
GROUP_KEYS = 32 * 128
PAD_FRONT = KCH
INT_MIN = -2 ** 31
NEG = -1e30
LOG2E = 1.4426950408889634

VMEM_LIMIT = 58 * 1024 * 1024


def _cparams(sem):
    return pltpu.CompilerParams(dimension_semantics=sem, vmem_limit_bytes=VMEM_LIMIT)


def _rms(x, gain):
    ms = jnp.mean(x * x, axis=-1, keepdims=True)
    return x * lax.rsqrt(ms + EPS) * gain


def _dot(a, b):
    return jnp.dot(a, b, preferred_element_type=F32)


def _dot_nt(a, b):
    return lax.dot_general(a, b, (((1,), (1,)), ((), ())), preferred_element_type=F32)


def _norm_matmul_kernel(x_ref, g_ref, w_ref, o_ref, u_ref):
    @pl.when(pl.program_id(1) == 0)
    def _():
        u_ref[...] = _rms(x_ref[...], g_ref[...]).astype(BF16)

    o_ref[...] = _dot(u_ref[...], w_ref[...]).astype(o_ref.dtype)


def _norm_matmul(x, gain, w, out_dtype, tm, tn):
    t, d = x.shape
    n = w.shape[1]
    return pl.pallas_call(
        _norm_matmul_kernel,
        grid=(t // tm, n // tn),
        in_specs=[pl.BlockSpec((tm, d), lambda i, j: (i, 0)),
                  pl.BlockSpec((1, d), lambda i, j: (0, 0)),
                  pl.BlockSpec((d, tn), lambda i, j: (0, j))],
        out_specs=pl.BlockSpec((tm, tn), lambda i, j: (i, j)),
        out_shape=jax.ShapeDtypeStruct((t, n), out_dtype),
        scratch_shapes=[pltpu.VMEM((tm, d), BF16)],
        compiler_params=_cparams(("parallel", "arbitrary")),
        name="norm_matmul",
    )(x, gain, w)


def _small_proj(x, gain, w_small, lat_g, ik_w, ik_b, batch, seq):
    tm = PAD_FRONT
    assert seq % tm == 0
    nblk = seq // tm
    npad = PAD_FRONT + seq
    grid = (batch, 1 + nblk)

    def x_map(b, i):
        return (b * nblk + jnp.maximum(i - 1, 0), 0)

    def kernel(x_ref, g_ref, w_ref, lat_g_ref, ik_w_ref, ik_b_ref, c_ref, k_ref, misc_ref):
        is_pad = pl.program_id(1) == 0

        @pl.when(is_pad)
        def _():
            c_ref[...] = jnp.zeros_like(c_ref)
            k_ref[...] = jnp.zeros_like(k_ref)
            misc_ref[...] = jnp.zeros_like(misc_ref)

        @pl.when(jnp.logical_not(is_pad))
        def _():
            u = _rms(x_ref[...], g_ref[...]).astype(BF16)
            p = _dot(u, w_ref[...])
            c_ref[0] = _rms(p[:, 0:256], lat_g_ref[...]).astype(BF16)
            ik = p[:, 256:384]
            mu = jnp.mean(ik, axis=-1, keepdims=True)
            xc = ik - mu
            var = jnp.mean(xc * xc, axis=-1, keepdims=True)
            k_ref[0] = (xc * lax.rsqrt(var + EPS) * ik_w_ref[...] + ik_b_ref[...]).astype(BF16)
            misc_ref[0] = p[:, 384:512]

    d = x.shape[1]
    const = lambda b, i: (0, 0)
    return pl.pallas_call(
        kernel,
        grid=grid,
        in_specs=[pl.BlockSpec((tm, d), x_map),
                  pl.BlockSpec((1, d), const),
                  pl.BlockSpec((d, SMALL_COLS), const),
                  pl.BlockSpec((1, DSA_LATENT), const),
                  pl.BlockSpec((1, IDX_DIM), const),
                  pl.BlockSpec((1, IDX_DIM), const)],
        out_specs=[pl.BlockSpec((1, tm, DSA_LATENT), lambda b, i: (b, i, 0)),
                   pl.BlockSpec((1, tm, IDX_DIM), lambda b, i: (b, i, 0)),
                   pl.BlockSpec((1, tm, 128), lambda b, i: (b, i, 0))],
        out_shape=[jax.ShapeDtypeStruct((batch, npad, DSA_LATENT), BF16),
                   jax.ShapeDtypeStruct((batch, npad, IDX_DIM), BF16),
                   jax.ShapeDtypeStruct((batch, npad, 128), F32)],
        compiler_params=_cparams(("parallel", "arbitrary")),
        name="small_proj",
    )(x, gain, w_small, lat_g, ik_w, ik_b)


def _log_sigmoid(z):
    return jnp.minimum(z, 0.0) - jnp.log1p(jnp.exp(-jnp.abs(z)))


def _split3(x):
    h = x.astype(BF16)
    r = x - h.astype(F32)
    m = r.astype(BF16)
    l = (r - m.astype(F32)).astype(BF16)
    return h, m, l


def _gla_kernel(nb, q_ref, k_ref, v_ref, r_ref, misc_ref, wa2_ref, ba_ref, on_ref, o_ref, state_ref):
    @pl.when(pl.program_id(0) == 0)
    def _():
        state_ref[...] = jnp.zeros_like(state_ref)

    rows = nb * GLA_SUB * CHUNK
    stack = lambda ref, cols: jnp.concatenate([ref[i][:, cols] for i in range(nb)], axis=0)
    row = lax.broadcasted_iota(I32, (rows, rows), 0)
    col = lax.broadcasted_iota(I32, (rows, rows), 1)
    same = (row // CHUNK) == (col // CHUNK)
    lower = jnp.logical_and(same, col <= row)
    upper = jnp.logical_and(same, col > row)
    tril = jnp.where(lower, 1.0, 0.0).astype(BF16)

    a_low = stack(misc_ref, slice(0, GLA_RANK)).astype(BF16)
    z = _dot(a_low, wa2_ref[...]) + ba_ref[...]
    la = _log_sigmoid(z) * (1.0 / GLA_TAU)
    l_h, l_m, l_l = _split3(la)
    b_all = _dot(tril, l_h) + _dot(tril, l_m) + _dot(tril, l_l)

    for h in range(GLA_HEADS):
        ks = slice(h * GLA_DK, (h + 1) * GLA_DK)
        vs = slice(h * GLA_DV, (h + 1) * GLA_DV)
        b = b_all[:, ks]
        eb = jnp.exp(b)
        ebi = jnp.exp(-b)
        q = stack(q_ref, ks).astype(F32) * (GLA_DK ** -0.5)
        k = stack(k_ref, ks).astype(F32)
        v = stack(v_ref, vs)
        q_fwd = (q * eb).astype(BF16)
        a_lo = _dot_nt(q_fwd, (k * ebi).astype(BF16))
        a_up = _dot_nt((q * ebi).astype(BF16), (k * eb).astype(BF16))
        scores = jnp.where(lower, a_lo, jnp.where(upper, a_up, 0.0)).astype(BF16)
        o_intra = _dot(scores, v)
        outs = []
        for i in range(nb):
            st = state_ref[i * GLA_HEADS + h]
            for sub in range(GLA_SUB):
                rs = slice((i * GLA_SUB + sub) * CHUNK, (i * GLA_SUB + sub + 1) * CHUNK)
                b_i = b[rs]
                b_last = b_i[CHUNK - 1:CHUNK, :]
                outs.append(o_intra[rs] + _dot_nt(q_fwd[rs], st.astype(BF16)))
                k_dec = (k[rs] * jnp.exp(b_last - b_i)).astype(BF16)
                v_t = v[rs].astype(F32).T.astype(BF16)
                st = st * jnp.exp(b_last) + _dot(v_t, k_dec)
            state_ref[i * GLA_HEADS + h] = st
        o = _rms(jnp.concatenate(outs, axis=0), on_ref[...])
        r = stack(r_ref, vs).astype(F32)
        y = (o * (r * jax.nn.sigmoid(r))).astype(o_ref.dtype)
        blk = GLA_SUB * CHUNK
        for i in range(nb):
            o_ref[i, :, vs] = y[i * blk:(i + 1) * blk]


def _gla(proj, misc, w_a2, b_a, out_norm, batch, seq):
    blk = GLA_SUB * CHUNK
    assert seq % blk == 0 and PAD_FRONT % blk == 0
    proj3 = proj.reshape(batch, seq, proj.shape[1])
    y = pl.pallas_call(
        functools.partial(_gla_kernel, batch),
        grid=(seq // blk,),
        in_specs=[pl.BlockSpec((batch, blk, 1024), lambda c: (0, c, 0)),
                  pl.BlockSpec((batch, blk, 1024), lambda c: (0, c, 1)),
                  pl.BlockSpec((batch, blk, 2048), lambda c: (0, c, 1)),
                  pl.BlockSpec((batch, blk, 2048), lambda c: (0, c, 2)),
                  pl.BlockSpec((batch, blk, 128), lambda c: (0, c + PAD_FRONT // blk, 0)),
                  pl.BlockSpec((GLA_RANK, GLA_HEADS * GLA_DK), lambda c: (0, 0)),
                  pl.BlockSpec((1, GLA_HEADS * GLA_DK), lambda c: (0, 0)),
                  pl.BlockSpec((1, GLA_DV), lambda c: (0, 0))],
        out_specs=pl.BlockSpec((batch, blk, GLA_HEADS * GLA_DV), lambda c: (0, c, 0)),
        out_shape=jax.ShapeDtypeStruct((batch, seq, GLA_HEADS * GLA_DV), BF16),
        scratch_shapes=[pltpu.VMEM((batch * GLA_HEADS, GLA_DV, GLA_DK), F32)],
        compiler_params=_cparams(("arbitrary",)),
        name="gla",
    )(proj3, proj3, proj3, proj3, misc, w_a2, b_a, out_norm)
    return y.reshape(batch * seq, GLA_HEADS * GLA_DV)


def _t5_bucket(rel):
    half = REL_BUCKETS // 2
    max_exact = half // 2
    ret = jnp.where(rel > 0, half, 0)
    n = jnp.abs(rel)
    nf = jnp.maximum(n, 1).astype(jnp.float32)
    large = max_exact + (jnp.log(nf / max_exact) / math.log(REL_MAX_DIST / max_exact)
                         * (half - max_exact)).astype(jnp.int32)
    large = jnp.minimum(large, half - 1)
    return ret + jnp.where(n < max_exact, n, large)


def _bias_band_kernel(bucket_ref, rb_ref, o_ref):
    far = REL_BUCKETS // 2 - 1
    bucket = bucket_ref[...]
    for h in range(DSA_HEADS):
        acc = jnp.zeros(bucket.shape, F32)
        for b in range(REL_BUCKETS):
            acc = jnp.where(bucket == b, rb_ref[b, h], acc)
        o_ref[h] = (acc - rb_ref[far, h]) * LOG2E


def _bias_band(rel_bias):
    t = jnp.arange(QBLK, dtype=jnp.int32)[:, None]
    j = jnp.arange(2 * QBLK, dtype=jnp.int32)[None, :]
    bucket = _t5_bucket(j - QBLK - t).astype(jnp.int32)
    return pl.pallas_call(
        _bias_band_kernel,
        in_specs=[pl.BlockSpec(memory_space=pltpu.VMEM), pl.BlockSpec(memory_space=pltpu.SMEM)],
        out_specs=pl.BlockSpec(memory_space=pltpu.VMEM),
        out_shape=jax.ShapeDtypeStruct((DSA_HEADS, QBLK, 2 * QBLK), F32),
        name="bias_band",
    )(bucket, rel_bias)


_SWAP_MASK = {16: 0x0000FFFF, 8: 0x00FF00FF, 4: 0x0F0F0F0F, 2: 0x33333333, 1: 0x55555555}


def _transpose_stages(words, stages):
    a = list(words)
    for j in stages:
        for k in range(len(a)):
            if k & j == 0:
                t = (a[k] ^ lax.shift_right_logical(a[k + j], jnp.int32(j))) & jnp.int32(_SWAP_MASK[j])
                a[k], a[k + j] = a[k] ^ t, a[k + j] ^ (t << j)
    return a


def _sortable(x):
    i = pltpu.bitcast(x, I32)
    return jnp.where(i < 0, i ^ jnp.int32(0x7FFFFFFF), i)


def _dsa_kernel(top_k, dq_ref, iq_ref, misc_ref, kidx_ref, clat_ref, wuk_ref, wuv_ref, band_ref,
                o_ref, keys_ref, planes_ref, eq_ref, iqs_ref, wb_ref, qlat_ref, madd_ref, s_ref, s2_ref, p_ref, alpha_ref,
                m_ref, l_ref, acc_ref):
    qb = pl.program_id(1)
    start = qb * QBLK
    hrows = lambda h: slice(h * QBLK, (h + 1) * QBLK)

    w_scale = IDX_HEADS ** -0.5 * IDX_DIM ** -0.5
    wq = misc_ref[0][:, GLA_RANK:GLA_RANK + IDX_HEADS] * w_scale
    for h in range(IDX_HEADS):
        wb_ref[hrows(h), :] = jnp.broadcast_to(wq[:, h:h + 1], (QBLK, 128))
        iqs_ref[hrows(h), :] = iq_ref[:, h * IDX_DIM:(h + 1) * IDX_DIM]

    row = lax.broadcasted_iota(I32, (QBLK, KCH), 0)
    lane = lax.broadcasted_iota(I32, (QBLK, KCH), 1)
    p_lim = start + (row // CHUNK + 1) * CHUNK + PAD_FRONT

    n_chunks = (start + PAD_FRONT + QBLK + KCH - 1) // KCH
    tiles_per_chunk = KCH // 128
    chunks_per_group = GROUP_KEYS // KCH

    def stage_planes(c, key):
        tiles = [key[:, i * 128:(i + 1) * 128] for i in range(tiles_per_chunk)]
        tiles = _transpose_stages(tiles, (2, 1))
        g = c // chunks_per_group
        w0 = (c % chunks_per_group) * tiles_per_chunk
        for i in range(tiles_per_chunk):
            planes_ref[w0 + i, :, pl.ds(pl.multiple_of(g * 128, 128), 128)] = tiles[i]

    no_key = jnp.full((QBLK, KCH), INT_MIN, I32)
    keys_ref[:, 0:KCH] = no_key
    stage_planes(jnp.int32(0), no_key)

    def index_chunk(c, dots_ref):
        off = pl.multiple_of(c * KCH, KCH)
        kc = kidx_ref[0, pl.ds(off, KCH), :]
        dots_ref[...] = _dot_nt(iqs_ref[...], kc)
        acc = jnp.zeros((QBLK, KCH), F32)
        for h in range(IDX_HEADS):
            wbh = wb_ref[hrows(h), :]
            acc = acc + jnp.concatenate([wbh] * (KCH // 128), axis=1) * jnp.maximum(dots_ref[hrows(h), :], 0.0)
        key = jnp.where(lane + off < p_lim, _sortable(acc), INT_MIN)
        keys_ref[:, pl.ds(off, KCH)] = key
        stage_planes(c, key)

    n_odd = (n_chunks - 1) % 2

    @pl.when(n_odd == 1)
    def _():
        index_chunk(1, s_ref)

    def idx_body(i, carry):
        c = 1 + n_odd + 2 * i
        index_chunk(c, s_ref)
        index_chunk(c + 1, s2_ref)
        return carry

    lax.fori_loop(0, (n_chunks - 1) // 2, idx_body, 0)

    n_groups = (n_chunks + chunks_per_group - 1) // chunks_per_group
    ngrp_max = eq_ref.shape[1] // 128

    def pad_body(c, carry):
        stage_planes(c, no_key)
        return carry

    lax.fori_loop(n_chunks, n_groups * chunks_per_group, pad_body, 0)

    def plane_body(idx, carry):
        g = idx // (QBLK // 16)
        gl = pl.ds(pl.multiple_of(g * 128, 128), 128)
        for half in range(2):
            r0 = pl.multiple_of((idx % (QBLK // 16)) * 16 + half * 8, 8)
            words = [planes_ref[j, pl.ds(r0, 8), gl] for j in range(32)]
            words = _transpose_stages(words, (16, 8, 4))
            words[0] = ~words[0]
            for i in range(32):
                planes_ref[31 - i, pl.ds(r0, 8), gl] = words[i]
        return carry

    lax.fori_loop(0, n_groups * (QBLK // 16), plane_body, 0)

    for g in range(ngrp_max):
        eq_ref[:, g * 128:(g + 1) * 128] = jnp.broadcast_to(jnp.where(g < n_groups, -1, 0), (QBLK, 128))

    def row_count(t):
        pc = lax.population_count(t)
        tot = pc[:, 0:128]
        for g in range(1, ngrp_max):
            tot = tot + pc[:, g * 128:(g + 1) * 128]
        return jnp.broadcast_to(jnp.sum(tot.astype(F32), axis=1, keepdims=True), (QBLK, 128))

    def pair_body(i, carry):
        prefix, above = carry
        b0 = 30 - 2 * i
        p1 = planes_ref[b0 + 1]
        p0 = planes_ref[b0]
        eq = eq_ref[...]
        e1 = eq & p1
        e0 = eq & ~p1
        t11 = e1 & p0
        t10 = e1 & ~p0
        t01 = e0 & p0
        t00 = e0 & ~p0
        s3 = above + row_count(t11)
        s2 = s3 + row_count(t10)
        s1 = s2 + row_count(t01)
        is3 = s3 >= top_k
        is2 = s2 >= top_k
        is1 = s1 >= top_k
        for g in range(ngrp_max):
            gs = slice(g * 128, (g + 1) * 128)
            eq_ref[:, gs] = jnp.where(is3, t11[:, gs], jnp.where(is2, t10[:, gs], jnp.where(is1, t01[:, gs], t00[:, gs])))
        above = jnp.where(is3, above, jnp.where(is2, s3, jnp.where(is1, s2, s1)))
        digit = jnp.where(is3, 3, jnp.where(is2, 2, jnp.where(is1, 1, 0)))
        return prefix | (digit << b0), above

    end = start + PAD_FRONT + QBLK
    n_att = (start + QBLK + KCH - 1) // KCH

    def chunk_off(j):
        return pl.multiple_of(end - KCH * (j + 1), 128)

    def absorbed_query(h):
        ql = _dot(dq_ref[:, h * DSA_DH:(h + 1) * DSA_DH], wuk_ref[h]) * (LOG2E * DSA_DH ** -0.5)
        qlat_ref[hrows(h), :] = ql.astype(BF16)

    def first_scores(h):
        s_ref[hrows(h), :] = _dot_nt(qlat_ref[hrows(h), :], clat_ref[0, pl.ds(chunk_off(0), KCH), :])

    carry = (jnp.zeros((QBLK, 128), I32), jnp.zeros((QBLK, 128), F32))
    for i in range(16):
        carry = pair_body(i, carry)
        for h in range(i * DSA_HEADS // 16, (i + 1) * DSA_HEADS // 16):
            absorbed_query(h)
            if h > 0:
                first_scores(h - 1)
    first_scores(DSA_HEADS - 1)
    thr = carry[0] ^ INT_MIN
    thr = jnp.maximum(thr, INT_MIN + 1)

    m_ref[...] = jnp.full(m_ref.shape, NEG, F32)
    l_ref[...] = jnp.zeros(l_ref.shape, F32)
    acc_ref[...] = jnp.zeros(acc_ref.shape, F32)

    def scores(j, dst_ref):
        cc = clat_ref[0, pl.ds(chunk_off(j), KCH), :]
        dst_ref[...] = _dot_nt(qlat_ref[...], cc)

    def softmax_update(j, src_ref, near):
        off = chunk_off(j)
        kk = keys_ref[:, pl.ds(off, KCH)]
        madd_ref[...] = jnp.where(kk >= jnp.concatenate([thr] * (KCH // 128), axis=1), 0.0, NEG)
        cc = clat_ref[0, pl.ds(off, KCH), :]
        for h in range(DSA_HEADS):
            s = src_ref[hrows(h), :] + madd_ref[...]
            if near:
                s = jnp.concatenate([s[:, :KCH - 2 * QBLK], s[:, KCH - 2 * QBLK:] + band_ref[h]], axis=1)
            m_old = m_ref[hrows(h), :]
            m_new = jnp.maximum(m_old, jnp.broadcast_to(jnp.max(s, axis=1, keepdims=True), (QBLK, 128)))
            alpha = jnp.exp2(m_old - m_new)
            p = jnp.exp2(s - jnp.concatenate([m_new] * (KCH // 128), axis=1))
            l_ref[hrows(h), :] = (alpha * l_ref[hrows(h), :]
                                  + jnp.broadcast_to(jnp.sum(p, axis=1, keepdims=True), (QBLK, 128)))
            m_ref[hrows(h), :] = m_new
            alpha_ref[hrows(h), :] = alpha
            p_ref[hrows(h), :] = p.astype(BF16)
        al = alpha_ref[...]
        acc_ref[...] = (jnp.concatenate([al] * (DSA_LATENT // 128), axis=1) * acc_ref[...]
                        + _dot(p_ref[...], cc))

    def step(j, cur_ref, nxt_ref, near):
        scores(jnp.minimum(j + 1, n_att - 1), nxt_ref)
        softmax_update(j, cur_ref, near)

    step(0, s_ref, s2_ref, True)

    def att_body(i, carry):
        j = 2 * i + 1
        step(j, s2_ref, s_ref, False)
        step(j + 1, s_ref, s2_ref, False)
        return carry

    lax.fori_loop(0, (n_att - 1) // 2, att_body, 0)

    @pl.when((n_att - 1) % 2 == 1)
    def _():
        step(n_att - 1, s2_ref, s_ref, False)

    for h in range(DSA_HEADS):
        inv = 1.0 / l_ref[hrows(h), :]
        o_lat = acc_ref[hrows(h), :] * jnp.concatenate([inv] * (DSA_LATENT // 128), axis=1)
        o_ref[:, h * DSA_DV:(h + 1) * DSA_DV] = _dot(o_lat.astype(BF16), wuv_ref[h]).astype(o_ref.dtype)


def _dsa(proj, misc, kidx, clat, w_uk, w_uv, band, batch, seq):
    nqb = seq // QBLK
    t = batch * seq
    npad = clat.shape[1]
    top_k = min(INDEX_TOPK, seq // 4)
    hq = DSA_HEADS * QBLK
    tokb = lambda b, i: b * nqb + i
    const3 = lambda b, i: (0, 0, 0)
    ngrp = (npad + GROUP_KEYS - 1) // GROUP_KEYS
    once = pl.Buffered(1)
    return pl.pallas_call(
        functools.partial(_dsa_kernel, top_k),
        grid=(batch, nqb),
        in_specs=[pl.BlockSpec((QBLK, 2048), lambda b, i: (tokb(b, i), 3)),
                  pl.BlockSpec((QBLK, 2048), lambda b, i: (tokb(b, i), 4)),
                  pl.BlockSpec((1, QBLK, 128), lambda b, i: (b, i + PAD_FRONT // QBLK, 0)),
                  pl.BlockSpec((1, npad, IDX_DIM), lambda b, i: (b, 0, 0), pipeline_mode=once),
                  pl.BlockSpec((1, npad, DSA_LATENT), lambda b, i: (b, 0, 0), pipeline_mode=once),
                  pl.BlockSpec((DSA_HEADS, DSA_DH, DSA_LATENT), const3, pipeline_mode=once),
                  pl.BlockSpec((DSA_HEADS, DSA_LATENT, DSA_DV), const3, pipeline_mode=once),
                  pl.BlockSpec((DSA_HEADS, QBLK, 2 * QBLK), const3, pipeline_mode=once)],
        out_specs=pl.BlockSpec((QBLK, DSA_HEADS * DSA_DV), lambda b, i: (tokb(b, i), 0)),
        out_shape=jax.ShapeDtypeStruct((t, DSA_HEADS * DSA_DV), BF16),
        scratch_shapes=[pltpu.VMEM((QBLK, npad), I32),
                        pltpu.VMEM((32, QBLK, ngrp * 128), I32),
                        pltpu.VMEM((QBLK, ngrp * 128), I32),
                        pltpu.VMEM((hq, IDX_DIM), BF16),
                        pltpu.VMEM((hq, 128), F32),
                        pltpu.VMEM((hq, DSA_LATENT), BF16),
                        pltpu.VMEM((QBLK, KCH), F32),
                        pltpu.VMEM((hq, KCH), F32),
                        pltpu.VMEM((hq, KCH), F32),
                        pltpu.VMEM((hq, KCH), BF16),
                        pltpu.VMEM((hq, 128), F32),
                        pltpu.VMEM((hq, 128), F32),
                        pltpu.VMEM((hq, 128), F32),
                        pltpu.VMEM((hq, DSA_LATENT), F32)],
        compiler_params=_cparams(("parallel", "arbitrary")),
        name="dsa",
    )(proj, proj, misc, kidx, clat, w_uk, w_uv, band)


def _mix_kernel(yg_ref, yd_ref, gg_ref, gd_ref, wg_ref, wd_ref, wm_ref, g_ref, h_ref, o_ref):
    a = _dot(yg_ref[...], wg_ref[...])
    b = _dot(yd_ref[...], wd_ref[...])
    gg = jax.nn.sigmoid(gg_ref[...].astype(F32))
    gd = jax.nn.sigmoid(gd_ref[...].astype(F32))
    merged = (gg * a + gd * b).astype(BF16)
    o_ref[...] = h_ref[...] + _rms(_dot(merged, wm_ref[...]), g_ref[...])


def _mix(y_gla, y_dsa, proj, w_g, w_d, w_m, gain, h, tm):
    t, d = y_gla.shape
    once = pl.Buffered(1)
    row = lambda i: (i, 0)
    fixed = lambda i: (0, 0)
    return pl.pallas_call(
        _mix_kernel,
        grid=(t // tm,),
        in_specs=[pl.BlockSpec((tm, d), row),
                  pl.BlockSpec((tm, d), row),
                  pl.BlockSpec((tm, d), lambda i: (i, 5)),
                  pl.BlockSpec((tm, d), lambda i: (i, 6)),
                  pl.BlockSpec((d, d), fixed, pipeline_mode=once),
                  pl.BlockSpec((d, d), fixed, pipeline_mode=once),
                  pl.BlockSpec((d, d), fixed, pipeline_mode=once),
                  pl.BlockSpec((1, d), fixed),
                  pl.BlockSpec((tm, d), row)],
        out_specs=pl.BlockSpec((tm, d), row),
        out_shape=jax.ShapeDtypeStruct((t, d), F32),
        compiler_params=_cparams(("parallel",)),
        name="mix",
    )(y_gla, y_dsa, proj, proj, w_g, w_d, w_m, gain, h)


def _xa_kernel(h_ref, g_ref, wq_ref, k_ref, v_ref, wo_ref, pg_ref, o_ref):
    h = h_ref[...]
    q = _dot(_rms(h, g_ref[...]).astype(BF16), wq_ref[...]).astype(BF16)
    outs = []
    for hd in range(XA_HEADS):
        hs = slice(hd * XA_DH, (hd + 1) * XA_DH)
        s = _dot_nt(q[:, hs], k_ref[0][:, hs]) * (XA_DH ** -0.5)
        m = jnp.max(s, axis=-1, keepdims=True)
        p = jnp.exp(s - m)
        p = p / jnp.sum(p, axis=-1, keepdims=True)
        outs.append(_dot(p.astype(BF16), v_ref[0][:, hs]).astype(BF16))
    y = _dot(jnp.concatenate(outs, axis=1), wo_ref[...])
    o_ref[...] = h + _rms(y, pg_ref[...])


def _xa(h, pre_gain, w_q, kv, w_o, post_gain, batch, seq, tm):
    t, d = h.shape
    n_mem = kv.shape[1]
    nb = seq // tm
    once = pl.Buffered(1)
    row = lambda b, i: (b * nb + i, 0)
    fixed = lambda b, i: (0, 0)
    return pl.pallas_call(
        _xa_kernel,
        grid=(batch, nb),
        in_specs=[pl.BlockSpec((tm, d), row),
                  pl.BlockSpec((1, d), fixed),
                  pl.BlockSpec((d, d), fixed, pipeline_mode=once),
                  pl.BlockSpec((1, n_mem, d), lambda b, i: (b, 0, 0)),
                  pl.BlockSpec((1, n_mem, d), lambda b, i: (b, 0, 1)),
                  pl.BlockSpec((d, d), fixed, pipeline_mode=once),
                  pl.BlockSpec((1, d), fixed)],
        out_specs=pl.BlockSpec((tm, d), row),
        out_shape=jax.ShapeDtypeStruct((t, d), F32),
        compiler_params=_cparams(("parallel", "parallel")),
        name="xa",
    )(h, pre_gain, w_q, kv, kv, w_o, post_gain)


def _ffn_kernel(h_ref, g_ref, wg_ref, wu_ref, wd_ref, pg_ref, o_ref, u_ref):
    f = pl.program_id(1)

    @pl.when(f == 0)
    def _():
        u_ref[...] = _rms(h_ref[...], g_ref[...]).astype(BF16)
        o_ref[...] = jnp.zeros_like(o_ref)

    u = u_ref[...]
    n_sub = max(wg_ref.shape[1] // 256, 1)
    half = wg_ref.shape[1] // n_sub
    down = None
    for s in range(n_sub):
        cs = slice(s * half, (s + 1) * half)
        a = _dot(u, wg_ref[:, cs])
        b = _dot(u, wu_ref[:, cs])
        act = (a * jax.nn.sigmoid(a) * b).astype(BF16)
        d = _dot(act, wd_ref[cs, :])
        down = d if down is None else down + d
    o_ref[...] += down

    @pl.when(f == pl.num_programs(1) - 1)
    def _():
        o_ref[...] = h_ref[...] + _rms(o_ref[...], pg_ref[...])


def _ffn(h, pre_gain, w_gate, w_up, w_down, post_gain, tm, tf):
    t, d = h.shape
    ff = w_gate.shape[1]
    return pl.pallas_call(
        _ffn_kernel,
        grid=(t // tm, ff // tf),
        in_specs=[pl.BlockSpec((tm, d), lambda i, f: (i, 0)),
                  pl.BlockSpec((1, d), lambda i, f: (0, 0)),
                  pl.BlockSpec((d, tf), lambda i, f: (0, f)),
                  pl.BlockSpec((d, tf), lambda i, f: (0, f)),
                  pl.BlockSpec((tf, d), lambda i, f: (f, 0)),
                  pl.BlockSpec((1, d), lambda i, f: (0, 0))],
        out_specs=pl.BlockSpec((tm, d), lambda i, f: (i, 0)),
        out_shape=jax.ShapeDtypeStruct((t, d), F32),
        scratch_shapes=[pltpu.VMEM((tm, d), BF16)],
        compiler_params=_cparams(("parallel", "arbitrary")),
        name="ffn",
    )(h, pre_gain, w_gate, w_up, w_down, post_gain)


def _row(v):
    return v.reshape(1, -1).astype(F32)


def _layer(h, mem, w_in, gla_w_a2, gla_b_a, gla_out_norm, dsa_w_uk, dsa_w_uv, dsa_latent_norm,
           idx_k_norm_w, idx_k_norm_b, band, w_gla_branch, w_dsa_branch, w_mix_out,
           mix_pre_norm, mix_post_norm, xa_pre_norm, xa_post_norm, xa_mem_norm,
           w_xa_q, w_xa_kv, w_xa_o, ffn_pre_norm, ffn_post_norm, w_ffn_gate, w_ffn_up, w_ffn_down,
           batch, seq):
    d = D_MODEL
    t = batch * seq
    cols = lambda a, b: w_in[:, a:b]
    w_main = jnp.concatenate(
        [cols(_O_GQ, _O_GA), cols(_O_DQ, _O_DC), cols(_O_IQ, _O_IK), cols(_O_GG, _O_END)], axis=1).astype(BF16)
    w_small = jnp.concatenate(
        [cols(_O_DC, _O_IQ), cols(_O_IK, _O_IW), cols(_O_GA, _O_DQ), cols(_O_IW, _O_GG),
         jnp.zeros((d, SMALL_COLS - 416), w_in.dtype)], axis=1).astype(BF16)

    pre = _row(mix_pre_norm)
    proj = _norm_matmul(h, pre, w_main, BF16, 1024, 1024)
    clat, kidx, misc = _small_proj(h, pre, w_small, _row(dsa_latent_norm), _row(idx_k_norm_w),
                                   _row(idx_k_norm_b), batch, seq)

    y_gla = _gla(proj, misc, gla_w_a2.astype(BF16), _row(gla_b_a), _row(gla_out_norm), batch, seq)
    y_dsa = _dsa(proj, misc, kidx, clat, dsa_w_uk.astype(BF16), dsa_w_uv.astype(BF16), band, batch, seq)

    h = _mix(y_gla, y_dsa, proj, w_gla_branch.astype(BF16), w_dsa_branch.astype(BF16),
             w_mix_out.astype(BF16), _row(mix_post_norm), h, 256)

    n_mem = mem.shape[1]
    kv = _norm_matmul(mem.reshape(batch * n_mem, d), _row(xa_mem_norm), w_xa_kv.astype(BF16), BF16,
                      batch * n_mem, 512).reshape(batch, n_mem, 2 * d)
    h = _xa(h, _row(xa_pre_norm), w_xa_q.astype(BF16), kv, w_xa_o.astype(BF16), _row(xa_post_norm),
            batch, seq, 512)

    h = _ffn(h, _row(ffn_pre_norm), w_ffn_gate.astype(BF16), w_ffn_up.astype(BF16),
             w_ffn_down.astype(BF16), _row(ffn_post_norm), 1024, 256)
    return h


def kernel(x, mem, w_in, gla_w_a2, gla_b_a, gla_out_norm, dsa_w_uk, dsa_w_uv, dsa_latent_norm,
           idx_k_norm_w, idx_k_norm_b, rel_bias, w_gla_branch, w_dsa_branch, w_mix_out,
           mix_pre_norm, mix_post_norm, xa_pre_norm, xa_post_norm, xa_mem_norm,
           w_xa_q, w_xa_kv, w_xa_o, ffn_pre_norm, ffn_post_norm, w_ffn_gate, w_ffn_up, w_ffn_down):
    batch, seq, d = x.shape
    depth = w_in.shape[0]
    band = _bias_band(rel_bias.astype(F32))
    h = x.reshape(batch * seq, d)
    for l in range(depth):
        h = _layer(h, mem, w_in[l], gla_w_a2[l], gla_b_a[l], gla_out_norm[l], dsa_w_uk[l], dsa_w_uv[l],
                   dsa_latent_norm[l], idx_k_norm_w[l], idx_k_norm_b[l], band, w_gla_branch[l],
                   w_dsa_branch[l], w_mix_out[l], mix_pre_norm[l], mix_post_norm[l], xa_pre_norm[l],
                   xa_post_norm[l], xa_mem_norm[l], w_xa_q[l], w_xa_kv[l], w_xa_o[l], ffn_pre_norm[l],
                   ffn_post_norm[l], w_ffn_gate[l], w_ffn_up[l], w_ffn_down[l], batch, seq)
    return h.reshape(batch, seq, d)
```

```python
import functools
import math

import jax
import jax.numpy as jnp
import numpy as np
from jax import lax
from jax.experimental import pallas as pl
from jax.experimental.pallas import tpu as pltpu

F32 = jnp.float32
BF16 = jnp.bfloat16
I32 = jnp.int32

D_MODEL = 2048
CHUNK = 64
EPS = 1e-6

GLA_HEADS = 4
GLA_DK = 256
GLA_DV = 512
GLA_RANK = 16
GLA_TAU = 16.0
GLA_SUB = 1

DSA_HEADS = 16
DSA_DH = 128
DSA_DV = 128
DSA_LATENT = 256
IDX_HEADS = 16
IDX_DIM = 128
INDEX_TOPK = 256

REL_BUCKETS = 32
REL_MAX_DIST = 128

XA_HEADS = 4
XA_DH = 512

_SPLITS = (1024, 1024, 2048, 2048, 16, 2048, 256, 2048, 128, 16, 2048, 2048)
_OFFS = tuple(int(v) for v in np.cumsum((0,) + _SPLITS))
(_O_GQ, _O_GK, _O_GV, _O_GR, _O_GA, _O_DQ, _O_DC, _O_IQ, _O_IK, _O_IW, _O_GG, _O_GD, _O_END) = _OFFS

MAIN_COLS = 14336
SMALL_COLS = 512

QBLK = 128
KCH = 512
GROUP_KEYS = 32 * 128
POS_BITS = 15
PAD_FRONT = KCH
INT_MIN = -2 ** 31
NEG = -1e30
LOG2E = 1.4426950408889634

VMEM_LIMIT = 58 * 1024 * 1024


def _cparams(sem):
    return pltpu.CompilerParams(dimension_semantics=sem, vmem_limit_bytes=VMEM_LIMIT)


def _rms(x, gain):
    ms = jnp.mean(x * x, axis=-1, keepdims=True)
    return x * lax.rsqrt(ms + EPS) * gain


def _dot(a, b):
    return jnp.dot(a, b, preferred_element_type=F32)


def _dot_nt(a, b):
    return lax.dot_general(a, b, (((1,), (1,)), ((), ())), preferred_element_type=F32)


def _norm_matmul_kernel(x_ref, g_ref, w_ref, o_ref, u_ref):
    @pl.when(pl.program_id(1) == 0)
    def _():
        u_ref[...] = _rms(x_ref[...], g_ref[...]).astype(BF16)

    o_ref[...] = _dot(u_ref[...], w_ref[...]).astype(o_ref.dtype)


def _norm_matmul(x, gain, w, out_dtype, tm, tn):
    t, d = x.shape
    n = w.shape[1]
    return pl.pallas_call(
        _norm_matmul_kernel,
        grid=(t // tm, n // tn),
        in_specs=[pl.BlockSpec((tm, d), lambda i, j: (i, 0)),
                  pl.BlockSpec((1, d), lambda i, j: (0, 0)),
                  pl.BlockSpec((d, tn), lambda i, j: (0, j))],
        out_specs=pl.BlockSpec((tm, tn), lambda i, j: (i, j)),
        out_shape=jax.ShapeDtypeStruct((t, n), out_dtype),
        scratch_shapes=[pltpu.VMEM((tm, d), BF16)],
        compiler_params=_cparams(("parallel", "arbitrary")),
        name="norm_matmul",
    )(x, gain, w)


def _small_proj(x, gain, w_small, lat_g, ik_w, ik_b, batch, seq):
    tm = PAD_FRONT
    assert seq % tm == 0
    nblk = seq // tm
    npad = PAD_FRONT + seq
    grid = (batch, 1 + nblk)

    def x_map(b, i):
        return (b * nblk + jnp.maximum(i - 1, 0), 0)

    def kernel(x_ref, g_ref, w_ref, lat_g_ref, ik_w_ref, ik_b_ref, c_ref, k_ref, misc_ref):
        is_pad = pl.program_id(1) == 0

        @pl.when(is_pad)
        def _():
            c_ref[...] = jnp.zeros_like(c_ref)
            k_ref[...] = jnp.zeros_like(k_ref)
            misc_ref[...] = jnp.zeros_like(misc_ref)

        @pl.when(jnp.logical_not(is_pad))
        def _():
            u = _rms(x_ref[...], g_ref[...]).astype(BF16)
            p = _dot(u, w_ref[...])
            c_ref[0] = _rms(p[:, 0:256], lat_g_ref[...]).astype(BF16)
            ik = p[:, 256:384]
            mu = jnp.mean(ik, axis=-1, keepdims=True)
            xc = ik - mu
            var = jnp.mean(xc * xc, axis=-1, keepdims=True)
            k_ref[0] = (xc * lax.rsqrt(var + EPS) * ik_w_ref[...] + ik_b_ref[...]).astype(BF16)
            misc_ref[0] = p[:, 384:512]

    d = x.shape[1]
    const = lambda b, i: (0, 0)
    return pl.pallas_call(
        kernel,
        grid=grid,
        in_specs=[pl.BlockSpec((tm, d), x_map),
                  pl.BlockSpec((1, d), const),
                  pl.BlockSpec((d, SMALL_COLS), const),
                  pl.BlockSpec((1, DSA_LATENT), const),
                  pl.BlockSpec((1, IDX_DIM), const),
                  pl.BlockSpec((1, IDX_DIM), const)],
        out_specs=[pl.BlockSpec((1, tm, DSA_LATENT), lambda b, i: (b, i, 0)),
                   pl.BlockSpec((1, tm, IDX_DIM), lambda b, i: (b, i, 0)),
                   pl.BlockSpec((1, tm, 128), lambda b, i: (b, i, 0))],
        out_shape=[jax.ShapeDtypeStruct((batch, npad, DSA_LATENT), BF16),
                   jax.ShapeDtypeStruct((batch, npad, IDX_DIM), BF16),
                   jax.ShapeDtypeStruct((batch, npad, 128), F32)],
        compiler_params=_cparams(("parallel", "arbitrary")),
        name="small_proj",
    )(x, gain, w_small, lat_g, ik_w, ik_b)


def _log_sigmoid(z):
    return jnp.minimum(z, 0.0) - jnp.log1p(jnp.exp(-jnp.abs(z)))


def _split3(x):
    h = x.astype(BF16)
    r = x - h.astype(F32)
    m = r.astype(BF16)
    l = (r - m.astype(F32)).astype(BF16)
    return h, m, l


def _gla_kernel(nb, q_ref, k_ref, v_ref, r_ref, misc_ref, wa2_ref, ba_ref, on_ref, o_ref, state_ref):
    @pl.when(pl.program_id(0) == 0)
    def _():
        state_ref[...] = jnp.zeros_like(state_ref)

    rows = nb * GLA_SUB * CHUNK
    stack = lambda ref, cols: jnp.concatenate([ref[i][:, cols] for i in range(nb)], axis=0)
    row = lax.broadcasted_iota(I32, (rows, rows), 0)
    col = lax.broadcasted_iota(I32, (rows, rows), 1)
    same = (row // CHUNK) == (col // CHUNK)
    lower = jnp.logical_and(same, col <= row)
    upper = jnp.logical_and(same, col > row)
    tril = jnp.where(lower, 1.0, 0.0).astype(BF16)

    a_low = stack(misc_ref, slice(0, GLA_RANK)).astype(BF16)
    z = _dot(a_low, wa2_ref[...]) + ba_ref[...]
    la = _log_sigmoid(z) * (1.0 / GLA_TAU)
    l_h, l_m, l_l = _split3(la)
    b_all = _dot(tril, l_h) + _dot(tril, l_m) + _dot(tril, l_l)

    for h in range(GLA_HEADS):
        ks = slice(h * GLA_DK, (h + 1) * GLA_DK)
        vs = slice(h * GLA_DV, (h + 1) * GLA_DV)
        b = b_all[:, ks]
        eb = jnp.exp(b)
        ebi = jnp.exp(-b)
        q = stack(q_ref, ks).astype(F32) * (GLA_DK ** -0.5)
        k = stack(k_ref, ks).astype(F32)
        v = stack(v_ref, vs)
        q_fwd = (q * eb).astype(BF16)
        a_lo = _dot_nt(q_fwd, (k * ebi).astype(BF16))
        a_up = _dot_nt((q * ebi).astype(BF16), (k * eb).astype(BF16))
        scores = jnp.where(lower, a_lo, jnp.where(upper, a_up, 0.0)).astype(BF16)
        o_intra = _dot(scores, v)
        outs = []
        for i in range(nb):
            st = state_ref[i * GLA_HEADS + h]
            for sub in range(GLA_SUB):
                rs = slice((i * GLA_SUB + sub) * CHUNK, (i * GLA_SUB + sub + 1) * CHUNK)
                b_i = b[rs]
                b_last = b_i[CHUNK - 1:CHUNK, :]
                outs.append(o_intra[rs] + _dot_nt(q_fwd[rs], st.astype(BF16)))
                k_dec = (k[rs] * jnp.exp(b_last - b_i)).astype(BF16)
                v_t = v[rs].astype(F32).T.astype(BF16)
                st = st * jnp.exp(b_last) + _dot(v_t, k_dec)
            state_ref[i * GLA_HEADS + h] = st
        o = _rms(jnp.concatenate(outs, axis=0), on_ref[...])
        r = stack(r_ref, vs).astype(F32)
        y = (o * (r * jax.nn.sigmoid(r))).astype(o_ref.dtype)
        blk = GLA_SUB * CHUNK
        for i in range(nb):
            o_ref[i, :, vs] = y[i * blk:(i + 1) * blk]


def _gla(proj, misc, w_a2, b_a, out_norm, batch, seq):
    blk = GLA_SUB * CHUNK
    assert seq % blk == 0 and PAD_FRONT % blk == 0
    proj3 = proj.reshape(batch, seq, proj.shape[1])
    y = pl.pallas_call(
        functools.partial(_gla_kernel, batch),
        grid=(seq // blk,),
        in_specs=[pl.BlockSpec((batch, blk, 1024), lambda c: (0, c, 0)),
                  pl.BlockSpec((batch, blk, 1024), lambda c: (0, c, 1)),
                  pl.BlockSpec((batch, blk, 2048), lambda c: (0, c, 1)),
                  pl.BlockSpec((batch, blk, 2048), lambda c: (0, c, 2)),
                  pl.BlockSpec((batch, blk, 128), lambda c: (0, c + PAD_FRONT // blk, 0)),
                  pl.BlockSpec((GLA_RANK, GLA_HEADS * GLA_DK), lambda c: (0, 0)),
                  pl.BlockSpec((1, GLA_HEADS * GLA_DK), lambda c: (0, 0)),
                  pl.BlockSpec((1, GLA_DV), lambda c: (0, 0))],
        out_specs=pl.BlockSpec((batch, blk, GLA_HEADS * GLA_DV), lambda c: (0, c, 0)),
        out_shape=jax.ShapeDtypeStruct((batch, seq, GLA_HEADS * GLA_DV), BF16),
        scratch_shapes=[pltpu.VMEM((batch * GLA_HEADS, GLA_DV, GLA_DK), F32)],
        compiler_params=_cparams(("arbitrary",)),
        name="gla",
    )(proj3, proj3, proj3, proj3, misc, w_a2, b_a, out_norm)
    return y.reshape(batch * seq, GLA_HEADS * GLA_DV)


def _t5_bucket(rel):
    half = REL_BUCKETS // 2
    max_exact = half // 2
    ret = jnp.where(rel > 0, half, 0)
    n = jnp.abs(rel)
    nf = jnp.maximum(n, 1).astype(jnp.float32)
    large = max_exact + (jnp.log(nf / max_exact) / math.log(REL_MAX_DIST / max_exact)
                         * (half - max_exact)).astype(jnp.int32)
    large = jnp.minimum(large, half - 1)
    return ret + jnp.where(n < max_exact, n, large)


def _bias_band_kernel(bucket_ref, rb_ref, o_ref):
    far = REL_BUCKETS // 2 - 1
    bucket = bucket_ref[...]
    for h in range(DSA_HEADS):
        acc = jnp.zeros(bucket.shape, F32)
        for b in range(REL_BUCKETS):
            acc = jnp.where(bucket == b, rb_ref[b, h], acc)
        o_ref[h] = (acc - rb_ref[far, h]) * LOG2E


def _bias_band(rel_bias):
    t = jnp.arange(QBLK, dtype=jnp.int32)[:, None]
    j = jnp.arange(2 * QBLK, dtype=jnp.int32)[None, :]
    bucket = _t5_bucket(j - QBLK - t).astype(jnp.int32)
    return pl.pallas_call(
        _bias_band_kernel,
        in_specs=[pl.BlockSpec(memory_space=pltpu.VMEM), pl.BlockSpec(memory_space=pltpu.SMEM)],
        out_specs=pl.BlockSpec(memory_space=pltpu.VMEM),
        out_shape=jax.ShapeDtypeStruct((DSA_HEADS, QBLK, 2 * QBLK), F32),
        name="bias_band",
    )(bucket, rel_bias)


_SWAP_MASK = {16: 0x0000FFFF, 8: 0x00FF00FF, 4: 0x0F0F0F0F, 2: 0x33333333, 1: 0x55555555}


def _transpose_stages(words, stages):
    a = list(words)
    for j in stages:
        for k in range(len(a)):
            if k & j == 0:
                t = (a[k] ^ lax.shift_right_logical(a[k + j], jnp.int32(j))) & jnp.int32(_SWAP_MASK[j])
                a[k], a[k + j] = a[k] ^ t, a[k + j] ^ (t << j)
    return a


def _sortable(x):
    i = pltpu.bitcast(x, I32)
    return jnp.where(i < 0, i ^ jnp.int32(0x7FFFFFFF), i)


def _dsa_kernel(top_k, dq_ref, iq_ref, misc_ref, kidx_ref, clat_ref, wuk_ref, wuv_ref, band_ref,
                o_ref, keys_ref, planes_ref, eq_ref, iqs_ref, wb_ref, qlat_ref, madd_ref, s_ref, s2_ref, p_ref, alpha_ref,
                m_ref, l_ref, acc_ref):
    qb = pl.program_id(1)
    start = qb * QBLK
    hrows = lambda h: slice(h * QBLK, (h + 1) * QBLK)

    w_scale = IDX_HEADS ** -0.5 * IDX_DIM ** -0.5
    wq = misc_ref[0][:, GLA_RANK:GLA_RANK + IDX_HEADS] * w_scale
    for h in range(IDX_HEADS):
        wb_ref[hrows(h), :] = jnp.broadcast_to(wq[:, h:h + 1], (QBLK, 128))
        iqs_ref[hrows(h), :] = iq_ref[:, h * IDX_DIM:(h + 1) * IDX_DIM]

    row = lax.broadcasted_iota(I32, (QBLK, KCH), 0)
    lane = lax.broadcasted_iota(I32, (QBLK, KCH), 1)
    p_lim = start + (row // CHUNK + 1) * CHUNK + PAD_FRONT

    n_chunks = (start + PAD_FRONT + QBLK + KCH - 1) // KCH
    tiles_per_chunk = KCH // 128
    chunks_per_group = GROUP_KEYS // KCH

    def stage_planes(c, key):
        tiles = [key[:, i * 128:(i + 1) * 128] for i in range(tiles_per_chunk)]
        tiles = _transpose_stages(tiles, (2, 1))
        g = c // chunks_per_group
        w0 = (c % chunks_per_group) * tiles_per_chunk
        for i in range(tiles_per_chunk):
            planes_ref[w0 + i, :, pl.ds(pl.multiple_of(g * 128, 128), 128)] = tiles[i]

    no_key = jnp.full((QBLK, KCH), INT_MIN, I32)
    keys_ref[:, 0:KCH] = no_key
    stage_planes(jnp.int32(0), no_key)

    def index_chunk(c, dots_ref):
        off = pl.multiple_of(c * KCH, KCH)
        kc = kidx_ref[0, pl.ds(off, KCH), :]
        dots_ref[...] = _dot_nt(iqs_ref[...], kc)
        acc = jnp.zeros((QBLK, KCH), F32)
        for h in range(IDX_HEADS):
            wbh = wb_ref[hrows(h), :]
            acc = acc + jnp.concatenate([wbh] * (KCH // 128), axis=1) * jnp.maximum(dots_ref[hrows(h), :], 0.0)
        key = jnp.where(lane + off < p_lim, _sortable(acc), INT_MIN)
        keys_ref[:, pl.ds(off, KCH)] = key
        stage_planes(c, key)

    n_odd = (n_chunks - 1) % 2

    @pl.when(n_odd == 1)
    def _():
        index_chunk(1, s_ref)

    def idx_body(i, carry):
        c = 1 + n_odd + 2 * i
        index_chunk(c, s_ref)
        index_chunk(c + 1, s2_ref)
        return carry

    lax.fori_loop(0, (n_chunks - 1) // 2, idx_body, 0)

    n_groups = (n_chunks + chunks_per_group - 1) // chunks_per_group
    ngrp_max = eq_ref.shape[1] // 128

    def pad_body(c, carry):
        stage_planes(c, no_key)
        return carry

    lax.fori_loop(n_chunks, n_groups * chunks_per_group, pad_body, 0)

    def plane_body(idx, carry):
        g = idx // (QBLK // 16)
        gl = pl.ds(pl.multiple_of(g * 128, 128), 128)
        for half in range(2):
            r0 = pl.multiple_of((idx % (QBLK // 16)) * 16 + half * 8, 8)
            words = [planes_ref[j, pl.ds(r0, 8), gl] for j in range(32)]
            words = _transpose_stages(words, (16, 8, 4))
            words[0] = ~words[0]
            for i in range(32):
                planes_ref[31 - i, pl.ds(r0, 8), gl] = words[i]
        return carry

    lax.fori_loop(0, n_groups * (QBLK // 16), plane_body, 0)

    for g in range(ngrp_max):
        eq_ref[:, g * 128:(g + 1) * 128] = jnp.broadcast_to(jnp.where(g < n_groups, -1, 0), (QBLK, 128))

    def row_count(t):
        pc = lax.population_count(t)
        tot = pc[:, 0:128]
        for g in range(1, ngrp_max):
            tot = tot + pc[:, g * 128:(g + 1) * 128]
        return jnp.broadcast_to(jnp.sum(tot.astype(F32), axis=1, keepdims=True), (QBLK, 128))

    def pair_body(i, carry):
        prefix, above = carry
        b0 = 30 - 2 * i
        p1 = planes_ref[b0 + 1]
        p0 = planes_ref[b0]
        eq = eq_ref[...]
        e1 = eq & p1
        e0 = eq & ~p1
        t11 = e1 & p0
        t10 = e1 & ~p0
        t01 = e0 & p0
        t00 = e0 & ~p0
        s3 = above + row_count(t11)
        s2 = s3 + row_count(t10)
        s1 = s2 + row_count(t01)
        is3 = s3 >= top_k
        is2 = s2 >= top_k
        is1 = s1 >= top_k
        for g in range(ngrp_max):
            gs = slice(g * 128, (g + 1) * 128)
            eq_ref[:, gs] = jnp.where(is3, t11[:, gs], jnp.where(is2, t10[:, gs], jnp.where(is1, t01[:, gs], t00[:, gs])))
        above = jnp.where(is3, above, jnp.where(is2, s3, jnp.where(is1, s2, s1)))
        digit = jnp.where(is3, 3, jnp.where(is2, 2, jnp.where(is1, 1, 0)))
        return prefix | (digit << b0), above

    end = start + PAD_FRONT + QBLK
    n_att = (start + QBLK + KCH - 1) // KCH

    def chunk_off(j):
        return pl.multiple_of(end - KCH * (j + 1), 128)

    def absorbed_query(h):
        ql = _dot(dq_ref[:, h * DSA_DH:(h + 1) * DSA_DH], wuk_ref[h]) * (LOG2E * DSA_DH ** -0.5)
        qlat_ref[hrows(h), :] = ql.astype(BF16)

    def first_scores(h):
        s_ref[hrows(h), :] = _dot_nt(qlat_ref[hrows(h), :], clat_ref[0, pl.ds(chunk_off(0), KCH), :])

    carry = (jnp.zeros((QBLK, 128), I32), jnp.zeros((QBLK, 128), F32))
    for i in range(16):
        carry = pair_body(i, carry)
        for h in range(i * DSA_HEADS // 16, (i + 1) * DSA_HEADS // 16):
            absorbed_query(h)
            if h > 0:
                first_scores(h - 1)
    first_scores(DSA_HEADS - 1)
    prefix, above = carry
    thr = prefix ^ INT_MIN
    thr = jnp.maximum(thr, INT_MIN + 1)

    surplus = jnp.where(prefix != 0, above + row_count(eq_ref[...]) - top_k, 0.0)

    @pl.when(jnp.max(surplus) > 0.0)
    def _():
        keep = top_k - above
        lane1 = lax.broadcasted_iota(I32, (QBLK, 128), 1)

        def tied_before(q):
            def body(c, cnt):
                off = pl.multiple_of(c * KCH, KCH)
                kk = keys_ref[:, pl.ds(off, KCH)]
                for s in range(KCH // 128):
                    hit = jnp.logical_and(kk[:, s * 128:(s + 1) * 128] == thr, lane1 + (off + s * 128) < q)
                    cnt = cnt + jnp.where(hit, 1, 0)
                return cnt
            cnt = lax.fori_loop(1, n_chunks, body, jnp.zeros((QBLK, 128), I32))
            return jnp.broadcast_to(jnp.sum(cnt.astype(F32), axis=1, keepdims=True), (QBLK, 128))

        def pos_body(i, q):
            cand = q | (jnp.int32(1) << (POS_BITS - 1 - i))
            return jnp.where(tied_before(cand) < keep, cand, q)

        last = lax.fori_loop(0, POS_BITS, pos_body, jnp.zeros((QBLK, 128), I32))
        last = jnp.where(surplus > 0.0, last, jnp.int32(2 ** POS_BITS))

        def demote_body(c, carry):
            off = pl.multiple_of(c * KCH, KCH)
            for s in range(KCH // 128):
                cs = pl.ds(off + s * 128, 128)
                kk = keys_ref[:, cs]
                drop = jnp.logical_and(kk == thr, lane1 + (off + s * 128) > last)
                keys_ref[:, cs] = jnp.where(drop, INT_MIN, kk)
            return carry

        lax.fori_loop(1, n_chunks, demote_body, 0)

    m_ref[...] = jnp.full(m_ref.shape, NEG, F32)
    l_ref[...] = jnp.zeros(l_ref.shape, F32)
    acc_ref[...] = jnp.zeros(acc_ref.shape, F32)

    def scores(j, dst_ref):
        cc = clat_ref[0, pl.ds(chunk_off(j), KCH), :]
        dst_ref[...] = _dot_nt(qlat_ref[...], cc)

    def softmax_update(j, src_ref, near):
        off = chunk_off(j)
        kk = keys_ref[:, pl.ds(off, KCH)]
        madd_ref[...] = jnp.where(kk >= jnp.concatenate([thr] * (KCH // 128), axis=1), 0.0, NEG)
        cc = clat_ref[0, pl.ds(off, KCH), :]
        for h in range(DSA_HEADS):
            s = src_ref[hrows(h), :] + madd_ref[...]
            if near:
                s = jnp.concatenate([s[:, :KCH - 2 * QBLK], s[:, KCH - 2 * QBLK:] + band_ref[h]], axis=1)
            m_old = m_ref[hrows(h), :]
            m_new = jnp.maximum(m_old, jnp.broadcast_to(jnp.max(s, axis=1, keepdims=True), (QBLK, 128)))
            alpha = jnp.exp2(m_old - m_new)
            p = jnp.exp2(s - jnp.concatenate([m_new] * (KCH // 128), axis=1))
            l_ref[hrows(h), :] = (alpha * l_ref[hrows(h), :]
                                  + jnp.broadcast_to(jnp.sum(p, axis=1, keepdims=True), (QBLK, 128)))
            m_ref[hrows(h), :] = m_new
            alpha_ref[hrows(h), :] = alpha
            p_ref[hrows(h), :] = p.astype(BF16)
        al = alpha_ref[...]
        acc_ref[...] = (jnp.concatenate([al] * (DSA_LATENT // 128), axis=1) * acc_ref[...]
                        + _dot(p_ref[...], cc))

    def step(j, cur_ref, nxt_ref, near):
        scores(jnp.minimum(j + 1, n_att - 1), nxt_ref)
        softmax_update(j, cur_ref, near)

    step(0, s_ref, s2_ref, True)

    def att_body(i, carry):
        j = 2 * i + 1
        step(j, s2_ref, s_ref, False)
        step(j + 1, s_ref, s2_ref, False)
        return carry

    lax.fori_loop(0, (n_att - 1) // 2, att_body, 0)

    @pl.when((n_att - 1) % 2 == 1)
    def _():
        step(n_att - 1, s2_ref, s_ref, False)

    for h in range(DSA_HEADS):
        inv = 1.0 / l_ref[hrows(h), :]
        o_lat = acc_ref[hrows(h), :] * jnp.concatenate([inv] * (DSA_LATENT // 128), axis=1)
        o_ref[:, h * DSA_DV:(h + 1) * DSA_DV] = _dot(o_lat.astype(BF16), wuv_ref[h]).astype(o_ref.dtype)


def _dsa(proj, misc, kidx, clat, w_uk, w_uv, band, batch, seq):
    nqb = seq // QBLK
    t = batch * seq
    npad = clat.shape[1]
    top_k = min(INDEX_TOPK, seq // 4)
    assert npad < 2 ** POS_BITS
    hq = DSA_HEADS * QBLK
    tokb = lambda b, i: b * nqb + i
    const3 = lambda b, i: (0, 0, 0)
    ngrp = (npad + GROUP_KEYS - 1) // GROUP_KEYS
    once = pl.Buffered(1)
    return pl.pallas_call(
        functools.partial(_dsa_kernel, top_k),
        grid=(batch, nqb),
        in_specs=[pl.BlockSpec((QBLK, 2048), lambda b, i: (tokb(b, i), 3)),
                  pl.BlockSpec((QBLK, 2048), lambda b, i: (tokb(b, i), 4)),
                  pl.BlockSpec((1, QBLK, 128), lambda b, i: (b, i + PAD_FRONT // QBLK, 0)),
                  pl.BlockSpec((1, npad, IDX_DIM), lambda b, i: (b, 0, 0), pipeline_mode=once),
                  pl.BlockSpec((1, npad, DSA_LATENT), lambda b, i: (b, 0, 0), pipeline_mode=once),
                  pl.BlockSpec((DSA_HEADS, DSA_DH, DSA_LATENT), const3, pipeline_mode=once),
                  pl.BlockSpec((DSA_HEADS, DSA_LATENT, DSA_DV), const3, pipeline_mode=once),
                  pl.BlockSpec((DSA_HEADS, QBLK, 2 * QBLK), const3, pipeline_mode=once)],
        out_specs=pl.BlockSpec((QBLK, DSA_HEADS * DSA_DV), lambda b, i: (tokb(b, i), 0)),
        out_shape=jax.ShapeDtypeStruct((t, DSA_HEADS * DSA_DV), BF16),
        scratch_shapes=[pltpu.VMEM((QBLK, npad), I32),
                        pltpu.VMEM((32, QBLK, ngrp * 128), I32),
                        pltpu.VMEM((QBLK, ngrp * 128), I32),
                        pltpu.VMEM((hq, IDX_DIM), BF16),
                        pltpu.VMEM((hq, 128), F32),
                        pltpu.VMEM((hq, DSA_LATENT), BF16),
                        pltpu.VMEM((QBLK, KCH), F32),
                        pltpu.VMEM((hq, KCH), F32),
                        pltpu.VMEM((hq, KCH), F32),
                        pltpu.VMEM((hq, KCH), BF16),
                        pltpu.VMEM((hq, 128), F32),
                        pltpu.VMEM((hq, 128), F32),
                        pltpu.VMEM((hq, 128), F32),
                        pltpu.VMEM((hq, DSA_LATENT), F32)],
        compiler_params=_cparams(("parallel", "arbitrary")),
        name="dsa",
    )(proj, proj, misc, kidx, clat, w_uk, w_uv, band)


def _mix_kernel(yg_ref, yd_ref, gg_ref, gd_ref, wg_ref, wd_ref, wm_ref, g_ref, h_ref, o_ref):
    a = _dot(yg_ref[...], wg_ref[...])
    b = _dot(yd_ref[...], wd_ref[...])
    gg = jax.nn.sigmoid(gg_ref[...].astype(F32))
    gd = jax.nn.sigmoid(gd_ref[...].astype(F32))
    merged = (gg * a + gd * b).astype(BF16)
    o_ref[...] = h_ref[...] + _rms(_dot(merged, wm_ref[...]), g_ref[...])


def _mix(y_gla, y_dsa, proj, w_g, w_d, w_m, gain, h, tm):
    t, d = y_gla.shape
    once = pl.Buffered(1)
    row = lambda i: (i, 0)
    fixed = lambda i: (0, 0)
    return pl.pallas_call(
        _mix_kernel,
        grid=(t // tm,),
        in_specs=[pl.BlockSpec((tm, d), row),
                  pl.BlockSpec((tm, d), row),
                  pl.BlockSpec((tm, d), lambda i: (i, 5)),
                  pl.BlockSpec((tm, d), lambda i: (i, 6)),
                  pl.BlockSpec((d, d), fixed, pipeline_mode=once),
                  pl.BlockSpec((d, d), fixed, pipeline_mode=once),
                  pl.BlockSpec((d, d), fixed, pipeline_mode=once),
                  pl.BlockSpec((1, d), fixed),
                  pl.BlockSpec((tm, d), row)],
        out_specs=pl.BlockSpec((tm, d), row),
        out_shape=jax.ShapeDtypeStruct((t, d), F32),
        compiler_params=_cparams(("parallel",)),
        name="mix",
    )(y_gla, y_dsa, proj, proj, w_g, w_d, w_m, gain, h)


def _xa_kernel(h_ref, g_ref, wq_ref, k_ref, v_ref, wo_ref, pg_ref, o_ref):
    h = h_ref[...]
    q = _dot(_rms(h, g_ref[...]).astype(BF16), wq_ref[...]).astype(BF16)
    outs = []
    for hd in range(XA_HEADS):
        hs = slice(hd * XA_DH, (hd + 1) * XA_DH)
        s = _dot_nt(q[:, hs], k_ref[0][:, hs]) * (XA_DH ** -0.5)
        m = jnp.max(s, axis=-1, keepdims=True)
        p = jnp.exp(s - m)
        p = p / jnp.sum(p, axis=-1, keepdims=True)
        outs.append(_dot(p.astype(BF16), v_ref[0][:, hs]).astype(BF16))
    y = _dot(jnp.concatenate(outs, axis=1), wo_ref[...])
    o_ref[...] = h + _rms(y, pg_ref[...])


def _xa(h, pre_gain, w_q, kv, w_o, post_gain, batch, seq, tm):
    t, d = h.shape
    n_mem = kv.shape[1]
    nb = seq // tm
    once = pl.Buffered(1)
    row = lambda b, i: (b * nb + i, 0)
    fixed = lambda b, i: (0, 0)
    return pl.pallas_call(
        _xa_kernel,
        grid=(batch, nb),
        in_specs=[pl.BlockSpec((tm, d), row),
                  pl.BlockSpec((1, d), fixed),
                  pl.BlockSpec((d, d), fixed, pipeline_mode=once),
                  pl.BlockSpec((1, n_mem, d), lambda b, i: (b, 0, 0)),
                  pl.BlockSpec((1, n_mem, d), lambda b, i: (b, 0, 1)),
                  pl.BlockSpec((d, d), fixed, pipeline_mode=once),
                  pl.BlockSpec((1, d), fixed)],
        out_specs=pl.BlockSpec((tm, d), row),
        out_shape=jax.ShapeDtypeStruct((t, d), F32),
        compiler_params=_cparams(("parallel", "parallel")),
        name="xa",
    )(h, pre_gain, w_q, kv, kv, w_o, post_gain)


def _ffn_kernel(h_ref, g_ref, wg_ref, wu_ref, wd_ref, pg_ref, o_ref, u_ref):
    f = pl.program_id(1)

    @pl.when(f == 0)
    def _():
        u_ref[...] = _rms(h_ref[...], g_ref[...]).astype(BF16)
        o_ref[...] = jnp.zeros_like(o_ref)

    u = u_ref[...]
    n_sub = max(wg_ref.shape[1] // 256, 1)
    half = wg_ref.shape[1] // n_sub
    down = None
    for s in range(n_sub):
        cs = slice(s * half, (s + 1) * half)
        a = _dot(u, wg_ref[:, cs])
        b = _dot(u, wu_ref[:, cs])
        act = (a * jax.nn.sigmoid(a) * b).astype(BF16)
        d = _dot(act, wd_ref[cs, :])
        down = d if down is None else down + d
    o_ref[...] += down

    @pl.when(f == pl.num_programs(1) - 1)
    def _():
        o_ref[...] = h_ref[...] + _rms(o_ref[...], pg_ref[...])


def _ffn(h, pre_gain, w_gate, w_up, w_down, post_gain, tm, tf):
    t, d = h.shape
    ff = w_gate.shape[1]
    return pl.pallas_call(
        _ffn_kernel,
        grid=(t // tm, ff // tf),
        in_specs=[pl.BlockSpec((tm, d), lambda i, f: (i, 0)),
                  pl.BlockSpec((1, d), lambda i, f: (0, 0)),
                  pl.BlockSpec((d, tf), lambda i, f: (0, f)),
                  pl.BlockSpec((d, tf), lambda i, f: (0, f)),
                  pl.BlockSpec((tf, d), lambda i, f: (f, 0)),
                  pl.BlockSpec((1, d), lambda i, f: (0, 0))],
        out_specs=pl.BlockSpec((tm, d), lambda i, f: (i, 0)),
        out_shape=jax.ShapeDtypeStruct((t, d), F32),
        scratch_shapes=[pltpu.VMEM((tm, d), BF16)],
        compiler_params=_cparams(("parallel", "arbitrary")),
        name="ffn",
    )(h, pre_gain, w_gate, w_up, w_down, post_gain)


def _row(v):
    return v.reshape(1, -1).astype(F32)


def _layer(h, mem, w_in, gla_w_a2, gla_b_a, gla_out_norm, dsa_w_uk, dsa_w_uv, dsa_latent_norm,
           idx_k_norm_w, idx_k_norm_b, band, w_gla_branch, w_dsa_branch, w_mix_out,
           mix_pre_norm, mix_post_norm, xa_pre_norm, xa_post_norm, xa_mem_norm,
           w_xa_q, w_xa_kv, w_xa_o, ffn_pre_norm, ffn_post_norm, w_ffn_gate, w_ffn_up, w_ffn_down,
           batch, seq):
    d = D_MODEL
    t = batch * seq
    cols = lambda a, b: w_in[:, a:b]
    w_main = jnp.concatenate(
        [cols(_O_GQ, _O_GA), cols(_O_DQ, _O_DC), cols(_O_IQ, _O_IK), cols(_O_GG, _O_END)], axis=1).astype(BF16)
    w_small = jnp.concatenate(
        [cols(_O_DC, _O_IQ), cols(_O_IK, _O_IW), cols(_O_GA, _O_DQ), cols(_O_IW, _O_GG),
         jnp.zeros((d, SMALL_COLS - 416), w_in.dtype)], axis=1).astype(BF16)

    pre = _row(mix_pre_norm)
    proj = _norm_matmul(h, pre, w_main, BF16, 1024, 1024)
    clat, kidx, misc = _small_proj(h, pre, w_small, _row(dsa_latent_norm), _row(idx_k_norm_w),
                                   _row(idx_k_norm_b), batch, seq)

    y_gla = _gla(proj, misc, gla_w_a2.astype(BF16), _row(gla_b_a), _row(gla_out_norm), batch, seq)
    y_dsa = _dsa(proj, misc, kidx, clat, dsa_w_uk.astype(BF16), dsa_w_uv.astype(BF16), band, batch, seq)

    h = _mix(y_gla, y_dsa, proj, w_gla_branch.astype(BF16), w_dsa_branch.astype(BF16),
             w_mix_out.astype(BF16), _row(mix_post_norm), h, 256)

    n_mem = mem.shape[1]
    kv = _norm_matmul(mem.reshape(batch * n_mem, d), _row(xa_mem_norm), w_xa_kv.astype(BF16), BF16,
                      batch * n_mem, 512).reshape(batch, n_mem, 2 * d)
    h = _xa(h, _row(xa_pre_norm), w_xa_q.astype(BF16), kv, w_xa_o.astype(BF16), _row(xa_post_norm),
            batch, seq, 512)

    h = _ffn(h, _row(ffn_pre_norm), w_ffn_gate.astype(BF16), w_ffn_up.astype(BF16),
             w_ffn_down.astype(BF16), _row(ffn_post_norm), 1024, 256)
    return h


def kernel(x, mem, w_in, gla_w_a2, gla_b_a, gla_out_norm, dsa_w_uk, dsa_w_uv, dsa_latent_norm,
           idx_k_norm_w, idx_k_norm_b, rel_bias, w_gla_branch, w_dsa_branch, w_mix_out,
           mix_pre_norm, mix_post_norm, xa_pre_norm, xa_post_norm, xa_mem_norm,
           w_xa_q, w_xa_kv, w_xa_o, ffn_pre_norm, ffn_post_norm, w_ffn_gate, w_ffn_up, w_ffn_down):
    batch, seq, d = x.shape
    depth = w_in.shape[0]
    band = _bias_band(rel_bias.astype(F32))
    h = x.reshape(batch * seq, d)
    for l in range(depth):
        h = _layer(h, mem, w_in[l], gla_w_a2[l], gla_b_a[l], gla_out_norm[l], dsa_w_uk[l], dsa_w_uv[l],
                   dsa_latent_norm[l], idx_k_norm_w[l], idx_k_norm_b[l], band, w_gla_branch[l],
                   w_dsa_branch[l], w_mix_out[l], mix_pre_norm[l], mix_post_norm[l], xa_pre_norm[l],
                   xa_post_norm[l], xa_mem_norm[l], w_xa_q[l], w_xa_kv[l], w_xa_o[l], ffn_pre_norm[l],
                   ffn_post_norm[l], w_ffn_gate[l], w_ffn_up[l], w_ffn_down[l], batch, seq)
    return h.reshape(batch, seq, d)
```

```python
import functools
import math

import jax
import jax.numpy as jnp
import numpy as np
from jax import lax
from jax.experimental import pallas as pl
from jax.experimental.pallas import tpu as pltpu

F32 = jnp.float32
BF16 = jnp.bfloat16
I32 = jnp.int32

D_MODEL = 2048
CHUNK = 64
EPS = 1e-6

GLA_HEADS = 4
GLA_DK = 256
GLA_DV = 512
GLA_RANK = 16
GLA_TAU = 16.0
GLA_SUB = 1

DSA_HEADS = 16
DSA_DH = 128
DSA_DV = 128
DSA_LATENT = 256
IDX_HEADS = 16
IDX_DIM = 128
INDEX_TOPK = 256

REL_BUCKETS = 32
REL_MAX_DIST = 128

XA_HEADS = 4
XA_DH = 512

_SPLITS = (1024, 1024, 2048, 2048, 16, 2048, 256, 2048, 128, 16, 2048, 2048)
_OFFS = tuple(int(v) for v in np.cumsum((0,) + _SPLITS))
(_O_GQ, _O_GK, _O_GV, _O_GR, _O_GA, _O_DQ, _O_DC, _O_IQ, _O_IK, _O_IW, _O_GG, _O_GD, _O_END) = _OFFS

MAIN_COLS = 14336
SMALL_COLS = 512

QBLK = 128
KCH = 512
GROUP_KEYS = 32 * 128
POS_BITS = 15
PAD_FRONT = KCH
INT_MIN = -2 ** 31
NEG = -1e30
LOG2E = 1.4426950408889634

VMEM_LIMIT = 58 * 1024 * 1024


def _cparams(sem):
    return pltpu.CompilerParams(dimension_semantics=sem, vmem_limit_bytes=VMEM_LIMIT)


def _rms(x, gain):
    ms = jnp.mean(x * x, axis=-1, keepdims=True)
    return x * lax.rsqrt(ms + EPS) * gain


def _dot(a, b):
    return jnp.dot(a, b, preferred_element_type=F32)


def _dot_nt(a, b):
    return lax.dot_general(a, b, (((1,), (1,)), ((), ())), preferred_element_type=F32)


def _norm_matmul_kernel(x_ref, g_ref, w_ref, o_ref, u_ref):
    @pl.when(pl.program_id(1) == 0)
    def _():
        u_ref[...] = _rms(x_ref[...], g_ref[...]).astype(BF16)

    o_ref[...] = _dot(u_ref[...], w_ref[...]).astype(o_ref.dtype)


def _norm_matmul(x, gain, w, out_dtype, tm, tn):
    t, d = x.shape
    n = w.shape[1]
    return pl.pallas_call(
        _norm_matmul_kernel,
        grid=(t // tm, n // tn),
        in_specs=[pl.BlockSpec((tm, d), lambda i, j: (i, 0)),
                  pl.BlockSpec((1, d), lambda i, j: (0, 0)),
                  pl.BlockSpec((d, tn), lambda i, j: (0, j))],
        out_specs=pl.BlockSpec((tm, tn), lambda i, j: (i, j)),
        out_shape=jax.ShapeDtypeStruct((t, n), out_dtype),
        scratch_shapes=[pltpu.VMEM((tm, d), BF16)],
        compiler_params=_cparams(("parallel", "arbitrary")),
        name="norm_matmul",
    )(x, gain, w)


def _small_proj(x, gain, w_small, lat_g, ik_w, ik_b, batch, seq):
    tm = PAD_FRONT
    assert seq % tm == 0
    nblk = seq // tm
    npad = PAD_FRONT + seq
    grid = (batch, 1 + nblk)

    def x_map(b, i):
        return (b * nblk + jnp.maximum(i - 1, 0), 0)

    def kernel(x_ref, g_ref, w_ref, lat_g_ref, ik_w_ref, ik_b_ref, c_ref, k_ref, misc_ref):
        is_pad = pl.program_id(1) == 0

        @pl.when(is_pad)
        def _():
            c_ref[...] = jnp.zeros_like(c_ref)
            k_ref[...] = jnp.zeros_like(k_ref)
            misc_ref[...] = jnp.zeros_like(misc_ref)

        @pl.when(jnp.logical_not(is_pad))
        def _():
            u = _rms(x_ref[...], g_ref[...]).astype(BF16)
            p = _dot(u, w_ref[...])
            c_ref[0] = _rms(p[:, 0:256], lat_g_ref[...]).astype(BF16)
            ik = p[:, 256:384]
            mu = jnp.mean(ik, axis=-1, keepdims=True)
            xc = ik - mu
            var = jnp.mean(xc * xc, axis=-1, keepdims=True)
            k_ref[0] = (xc * lax.rsqrt(var + EPS) * ik_w_ref[...] + ik_b_ref[...]).astype(BF16)
            misc_ref[0] = p[:, 384:512]

    d = x.shape[1]
    const = lambda b, i: (0, 0)
    return pl.pallas_call(
        kernel,
        grid=grid,
        in_specs=[pl.BlockSpec((tm, d), x_map),
                  pl.BlockSpec((1, d), const),
                  pl.BlockSpec((d, SMALL_COLS), const),
                  pl.BlockSpec((1, DSA_LATENT), const),
                  pl.BlockSpec((1, IDX_DIM), const),
                  pl.BlockSpec((1, IDX_DIM), const)],
        out_specs=[pl.BlockSpec((1, tm, DSA_LATENT), lambda b, i: (b, i, 0)),
                   pl.BlockSpec((1, tm, IDX_DIM), lambda b, i: (b, i, 0)),
                   pl.BlockSpec((1, tm, 128), lambda b, i: (b, i, 0))],
        out_shape=[jax.ShapeDtypeStruct((batch, npad, DSA_LATENT), BF16),
                   jax.ShapeDtypeStruct((batch, npad, IDX_DIM), BF16),
                   jax.ShapeDtypeStruct((batch, npad, 128), F32)],
        compiler_params=_cparams(("parallel", "arbitrary")),
        name="small_proj",
    )(x, gain, w_small, lat_g, ik_w, ik_b)


def _log_sigmoid(z):
    return jnp.minimum(z, 0.0) - jnp.log1p(jnp.exp(-jnp.abs(z)))


def _split3(x):
    h = x.astype(BF16)
    r = x - h.astype(F32)
    m = r.astype(BF16)
    l = (r - m.astype(F32)).astype(BF16)
    return h, m, l


def _gla_kernel(nb, q_ref, k_ref, v_ref, r_ref, misc_ref, wa2_ref, ba_ref, on_ref, o_ref, state_ref):
    @pl.when(pl.program_id(0) == 0)
    def _():
        state_ref[...] = jnp.zeros_like(state_ref)

    rows = nb * GLA_SUB * CHUNK
    stack = lambda ref, cols: jnp.concatenate([ref[i][:, cols] for i in range(nb)], axis=0)
    row = lax.broadcasted_iota(I32, (rows, rows), 0)
    col = lax.broadcasted_iota(I32, (rows, rows), 1)
    same = (row // CHUNK) == (col // CHUNK)
    lower = jnp.logical_and(same, col <= row)
    upper = jnp.logical_and(same, col > row)
    tril = jnp.where(lower, 1.0, 0.0).astype(BF16)

    a_low = stack(misc_ref, slice(0, GLA_RANK)).astype(BF16)
    z = _dot(a_low, wa2_ref[...]) + ba_ref[...]
    la = _log_sigmoid(z) * (1.0 / GLA_TAU)
    l_h, l_m, l_l = _split3(la)
    b_all = _dot(tril, l_h) + _dot(tril, l_m) + _dot(tril, l_l)

    for h in range(GLA_HEADS):
        ks = slice(h * GLA_DK, (h + 1) * GLA_DK)
        vs = slice(h * GLA_DV, (h + 1) * GLA_DV)
        b = b_all[:, ks]
        eb = jnp.exp(b)
        ebi = jnp.exp(-b)
        q = stack(q_ref, ks).astype(F32) * (GLA_DK ** -0.5)
        k = stack(k_ref, ks).astype(F32)
        v = stack(v_ref, vs)
        q_fwd = (q * eb).astype(BF16)
        a_lo = _dot_nt(q_fwd, (k * ebi).astype(BF16))
        a_up = _dot_nt((q * ebi).astype(BF16), (k * eb).astype(BF16))
        scores = jnp.where(lower, a_lo, jnp.where(upper, a_up, 0.0)).astype(BF16)
        o_intra = _dot(scores, v)
        outs = []
        for i in range(nb):
            st = state_ref[i * GLA_HEADS + h]
            for sub in range(GLA_SUB):
                rs = slice((i * GLA_SUB + sub) * CHUNK, (i * GLA_SUB + sub + 1) * CHUNK)
                b_i = b[rs]
                b_last = b_i[CHUNK - 1:CHUNK, :]
                outs.append(o_intra[rs] + _dot_nt(q_fwd[rs], st.astype(BF16)))
                k_dec = (k[rs] * jnp.exp(b_last - b_i)).astype(BF16)
                v_t = v[rs].astype(F32).T.astype(BF16)
                st = st * jnp.exp(b_last) + _dot(v_t, k_dec)
            state_ref[i * GLA_HEADS + h] = st
        o = _rms(jnp.concatenate(outs, axis=0), on_ref[...])
        r = stack(r_ref, vs).astype(F32)
        y = (o * (r * jax.nn.sigmoid(r))).astype(o_ref.dtype)
        blk = GLA_SUB * CHUNK
        for i in range(nb):
            o_ref[i, :, vs] = y[i * blk:(i + 1) * blk]


def _gla(proj, misc, w_a2, b_a, out_norm, batch, seq):
    blk = GLA_SUB * CHUNK
    assert seq % blk == 0 and PAD_FRONT % blk == 0
    proj3 = proj.reshape(batch, seq, proj.shape[1])
    y = pl.pallas_call(
        functools.partial(_gla_kernel, batch),
        grid=(seq // blk,),
        in_specs=[pl.BlockSpec((batch, blk, 1024), lambda c: (0, c, 0)),
                  pl.BlockSpec((batch, blk, 1024), lambda c: (0, c, 1)),
                  pl.BlockSpec((batch, blk, 2048), lambda c: (0, c, 1)),
                  pl.BlockSpec((batch, blk, 2048), lambda c: (0, c, 2)),
                  pl.BlockSpec((batch, blk, 128), lambda c: (0, c + PAD_FRONT // blk, 0)),
                  pl.BlockSpec((GLA_RANK, GLA_HEADS * GLA_DK), lambda c: (0, 0)),
                  pl.BlockSpec((1, GLA_HEADS * GLA_DK), lambda c: (0, 0)),
                  pl.BlockSpec((1, GLA_DV), lambda c: (0, 0))],
        out_specs=pl.BlockSpec((batch, blk, GLA_HEADS * GLA_DV), lambda c: (0, c, 0)),
        out_shape=jax.ShapeDtypeStruct((batch, seq, GLA_HEADS * GLA_DV), BF16),
        scratch_shapes=[pltpu.VMEM((batch * GLA_HEADS, GLA_DV, GLA_DK), F32)],
        compiler_params=_cparams(("arbitrary",)),
        name="gla",
    )(proj3, proj3, proj3, proj3, misc, w_a2, b_a, out_norm)
    return y.reshape(batch * seq, GLA_HEADS * GLA_DV)


def _t5_bucket(rel):
    half = REL_BUCKETS // 2
    max_exact = half // 2
    ret = jnp.where(rel > 0, half, 0)
    n = jnp.abs(rel)
    nf = jnp.maximum(n, 1).astype(jnp.float32)
    large = max_exact + (jnp.log(nf / max_exact) / math.log(REL_MAX_DIST / max_exact)
                         * (half - max_exact)).astype(jnp.int32)
    large = jnp.minimum(large, half - 1)
    return ret + jnp.where(n < max_exact, n, large)


def _bias_band_kernel(bucket_ref, rb_ref, o_ref):
    far = REL_BUCKETS // 2 - 1
    bucket = bucket_ref[...]
    for h in range(DSA_HEADS):
        acc = jnp.zeros(bucket.shape, F32)
        for b in range(REL_BUCKETS):
            acc = jnp.where(bucket == b, rb_ref[b, h], acc)
        o_ref[h] = (acc - rb_ref[far, h]) * LOG2E


def _bias_band(rel_bias):
    t = jnp.arange(QBLK, dtype=jnp.int32)[:, None]
    j = jnp.arange(2 * QBLK, dtype=jnp.int32)[None, :]
    bucket = _t5_bucket(j - QBLK - t).astype(jnp.int32)
    return pl.pallas_call(
        _bias_band_kernel,
        in_specs=[pl.BlockSpec(memory_space=pltpu.VMEM), pl.BlockSpec(memory_space=pltpu.SMEM)],
        out_specs=pl.BlockSpec(memory_space=pltpu.VMEM),
        out_shape=jax.ShapeDtypeStruct((DSA_HEADS, QBLK, 2 * QBLK), F32),
        name="bias_band",
    )(bucket, rel_bias)


_SWAP_MASK = {16: 0x0000FFFF, 8: 0x00FF00FF, 4: 0x0F0F0F0F, 2: 0x33333333, 1: 0x55555555}


def _transpose_stages(words, stages):
    a = list(words)
    for j in stages:
        for k in range(len(a)):
            if k & j == 0:
                t = (a[k] ^ lax.shift_right_logical(a[k + j], jnp.int32(j))) & jnp.int32(_SWAP_MASK[j])
                a[k], a[k + j] = a[k] ^ t, a[k + j] ^ (t << j)
    return a


def _sortable(x):
    i = pltpu.bitcast(x, I32)
    return jnp.where(i < 0, i ^ jnp.int32(0x7FFFFFFF), i)


def _dsa_kernel(top_k, dq_ref, iq_ref, misc_ref, kidx_ref, clat_ref, wuk_ref, wuv_ref, band_ref,
                o_ref, keys_ref, planes_ref, eq_ref, iqs_ref, wb_ref, qlat_ref, madd_ref, s_ref, s2_ref, p_ref, alpha_ref,
                m_ref, l_ref, acc_ref):
    qb = pl.program_id(1)
    start = qb * QBLK
    hrows = lambda h: slice(h * QBLK, (h + 1) * QBLK)

    w_scale = IDX_HEADS ** -0.5 * IDX_DIM ** -0.5
    wq = misc_ref[0][:, GLA_RANK:GLA_RANK + IDX_HEADS] * w_scale
    for h in range(IDX_HEADS):
        wb_ref[hrows(h), :] = jnp.broadcast_to(wq[:, h:h + 1], (QBLK, 128))
        iqs_ref[hrows(h), :] = iq_ref[:, h * IDX_DIM:(h + 1) * IDX_DIM]

    row = lax.broadcasted_iota(I32, (QBLK, KCH), 0)
    lane = lax.broadcasted_iota(I32, (QBLK, KCH), 1)
    p_lim = start + (row // CHUNK + 1) * CHUNK + PAD_FRONT

    n_chunks = (start + PAD_FRONT + QBLK + KCH - 1) // KCH
    tiles_per_chunk = KCH // 128
    chunks_per_group = GROUP_KEYS // KCH

    def stage_planes(c, key):
        tiles = [key[:, i * 128:(i + 1) * 128] for i in range(tiles_per_chunk)]
        tiles = _transpose_stages(tiles, (2, 1))
        g = c // chunks_per_group
        w0 = (c % chunks_per_group) * tiles_per_chunk
        for i in range(tiles_per_chunk):
            planes_ref[w0 + i, :, pl.ds(pl.multiple_of(g * 128, 128), 128)] = tiles[i]

    no_key = jnp.full((QBLK, KCH), INT_MIN, I32)
    keys_ref[:, 0:KCH] = no_key
    stage_planes(jnp.int32(0), no_key)

    def index_chunk(c, dots_ref):
        off = pl.multiple_of(c * KCH, KCH)
        kc = kidx_ref[0, pl.ds(off, KCH), :]
        dots_ref[...] = _dot_nt(iqs_ref[...], kc)
        acc = jnp.zeros((QBLK, KCH), F32)
        for h in range(IDX_HEADS):
            wbh = wb_ref[hrows(h), :]
            acc = acc + jnp.concatenate([wbh] * (KCH // 128), axis=1) * jnp.maximum(dots_ref[hrows(h), :], 0.0)
        key = jnp.where(lane + off < p_lim, _sortable(acc), INT_MIN)
        keys_ref[:, pl.ds(off, KCH)] = key
        stage_planes(c, key)

    n_odd = (n_chunks - 1) % 2

    @pl.when(n_odd == 1)
    def _():
        index_chunk(1, s_ref)

    def idx_body(i, carry):
        c = 1 + n_odd + 2 * i
        index_chunk(c, s_ref)
        index_chunk(c + 1, s2_ref)
        return carry

    lax.fori_loop(0, (n_chunks - 1) // 2, idx_body, 0)

    n_groups = (n_chunks + chunks_per_group - 1) // chunks_per_group
    ngrp_max = eq_ref.shape[1] // 128

    def pad_body(c, carry):
        stage_planes(c, no_key)
        return carry

    lax.fori_loop(n_chunks, n_groups * chunks_per_group, pad_body, 0)

    def plane_body(idx, carry):
        g = idx // (QBLK // 16)
        gl = pl.ds(pl.multiple_of(g * 128, 128), 128)
        for half in range(2):
            r0 = pl.multiple_of((idx % (QBLK // 16)) * 16 + half * 8, 8)
            words = [planes_ref[j, pl.ds(r0, 8), gl] for j in range(32)]
            words = _transpose_stages(words, (16, 8, 4))
            words[0] = ~words[0]
            for i in range(32):
                planes_ref[31 - i, pl.ds(r0, 8), gl] = words[i]
        return carry

    lax.fori_loop(0, n_groups * (QBLK // 16), plane_body, 0)

    for g in range(ngrp_max):
        eq_ref[:, g * 128:(g + 1) * 128] = jnp.broadcast_to(jnp.where(g < n_groups, -1, 0), (QBLK, 128))

    def row_count(t):
        pc = lax.population_count(t)
        tot = pc[:, 0:128]
        for g in range(1, ngrp_max):
            tot = tot + pc[:, g * 128:(g + 1) * 128]
        return jnp.broadcast_to(jnp.sum(tot.astype(F32), axis=1, keepdims=True), (QBLK, 128))

    def pair_body(i, carry):
        prefix, above = carry
        b0 = 30 - 2 * i
        p1 = planes_ref[b0 + 1]
        p0 = planes_ref[b0]
        eq = eq_ref[...]
        e1 = eq & p1
        e0 = eq & ~p1
        t11 = e1 & p0
        t10 = e1 & ~p0
        t01 = e0 & p0
        t00 = e0 & ~p0
        s3 = above + row_count(t11)
        s2 = s3 + row_count(t10)
        s1 = s2 + row_count(t01)
        is3 = s3 >= top_k
        is2 = s2 >= top_k
        is1 = s1 >= top_k
        for g in range(ngrp_max):
            gs = slice(g * 128, (g + 1) * 128)
            eq_ref[:, gs] = jnp.where(is3, t11[:, gs], jnp.where(is2, t10[:, gs], jnp.where(is1, t01[:, gs], t00[:, gs])))
        above = jnp.where(is3, above, jnp.where(is2, s3, jnp.where(is1, s2, s1)))
        digit = jnp.where(is3, 3, jnp.where(is2, 2, jnp.where(is1, 1, 0)))
        return prefix | (digit << b0), above

    end = start + PAD_FRONT + QBLK
    n_att = (start + QBLK + KCH - 1) // KCH

    def chunk_off(j):
        return pl.multiple_of(end - KCH * (j + 1), 128)

    def absorbed_query(h):
        ql = _dot(dq_ref[:, h * DSA_DH:(h + 1) * DSA_DH], wuk_ref[h]) * (LOG2E * DSA_DH ** -0.5)
        qlat_ref[hrows(h), :] = ql.astype(BF16)

    def first_scores(h):
        s_ref[hrows(h), :] = _dot_nt(qlat_ref[hrows(h), :], clat_ref[0, pl.ds(chunk_off(0), KCH), :])

    carry = (jnp.zeros((QBLK, 128), I32), jnp.zeros((QBLK, 128), F32))
    for i in range(16):
        carry = pair_body(i, carry)
        for h in range(i * DSA_HEADS // 16, (i + 1) * DSA_HEADS // 16):
            absorbed_query(h)
            if h > 0:
                first_scores(h - 1)
    first_scores(DSA_HEADS - 1)
    prefix, above = carry
    thr = prefix ^ INT_MIN
    thr = jnp.maximum(thr, INT_MIN + 1)

    surplus = jnp.where(prefix != 0, above + row_count(eq_ref[...]) - top_k, 0.0)

    def reset_softmax():
        m_ref[...] = jnp.full(m_ref.shape, NEG, F32)
        l_ref[...] = jnp.zeros(l_ref.shape, F32)
        acc_ref[...] = jnp.zeros(acc_ref.shape, F32)

    def demote_surplus_ties():
        keep = top_k - above
        lane1 = lax.broadcasted_iota(I32, (QBLK, 128), 1)

        def tied_before(q):
            def body(c, cnt):
                off = pl.multiple_of(c * KCH, KCH)
                kk = keys_ref[:, pl.ds(off, KCH)]
                for s in range(KCH // 128):
                    hit = jnp.logical_and(kk[:, s * 128:(s + 1) * 128] == thr, lane1 + (off + s * 128) < q)
                    cnt = cnt + jnp.where(hit, 1, 0)
                return cnt
            cnt = lax.fori_loop(1, n_chunks, body, jnp.zeros((QBLK, 128), I32))
            return jnp.broadcast_to(jnp.sum(cnt.astype(F32), axis=1, keepdims=True), (QBLK, 128))

        def pos_body(i, q):
            cand = q | (jnp.int32(1) << (POS_BITS - 1 - i))
            return jnp.where(tied_before(cand) < keep, cand, q)

        last = lax.fori_loop(0, POS_BITS, pos_body, jnp.zeros((QBLK, 128), I32))
        last = jnp.where(surplus > 0.0, last, jnp.int32(2 ** POS_BITS))

        def demote_body(c, carry):
            off = pl.multiple_of(c * KCH, KCH)
            for s in range(KCH // 128):
                cs = pl.ds(off + s * 128, 128)
                kk = keys_ref[:, cs]
                drop = jnp.logical_and(kk == thr, lane1 + (off + s * 128) > last)
                keys_ref[:, cs] = jnp.where(drop, INT_MIN, kk)
            return carry

        lax.fori_loop(1, n_chunks, demote_body, 0)

    reset_softmax()

    def scores(j, dst_ref):
        cc = clat_ref[0, pl.ds(chunk_off(j), KCH), :]
        dst_ref[...] = _dot_nt(qlat_ref[...], cc)

    def softmax_update(j, src_ref, near):
        off = chunk_off(j)
        kk = keys_ref[:, pl.ds(off, KCH)]
        madd_ref[...] = jnp.where(kk >= jnp.concatenate([thr] * (KCH // 128), axis=1), 0.0, NEG)
        cc = clat_ref[0, pl.ds(off, KCH), :]
        for h in range(DSA_HEADS):
            s = src_ref[hrows(h), :] + madd_ref[...]
            if near:
                s = jnp.concatenate([s[:, :KCH - 2 * QBLK], s[:, KCH - 2 * QBLK:] + band_ref[h]], axis=1)
            m_old = m_ref[hrows(h), :]
            m_new = jnp.maximum(m_old, jnp.broadcast_to(jnp.max(s, axis=1, keepdims=True), (QBLK, 128)))
            alpha = jnp.exp2(m_old - m_new)
            p = jnp.exp2(s - jnp.concatenate([m_new] * (KCH // 128), axis=1))
            l_ref[hrows(h), :] = (alpha * l_ref[hrows(h), :]
                                  + jnp.broadcast_to(jnp.sum(p, axis=1, keepdims=True), (QBLK, 128)))
            m_ref[hrows(h), :] = m_new
            alpha_ref[hrows(h), :] = alpha
            p_ref[hrows(h), :] = p.astype(BF16)
        al = alpha_ref[...]
        acc_ref[...] = (jnp.concatenate([al] * (DSA_LATENT // 128), axis=1) * acc_ref[...]
                        + _dot(p_ref[...], cc))

    def step(j, cur_ref, nxt_ref, near):
        scores(jnp.minimum(j + 1, n_att - 1), nxt_ref)
        softmax_update(j, cur_ref, near)

    step(0, s_ref, s2_ref, True)

    @pl.when(jnp.max(surplus) > 0.0)
    def _():
        demote_surplus_ties()
        reset_softmax()
        softmax_update(0, s_ref, True)

    def att_body(i, carry):
        j = 2 * i + 1
        step(j, s2_ref, s_ref, False)
        step(j + 1, s_ref, s2_ref, False)
        return carry

    lax.fori_loop(0, (n_att - 1) // 2, att_body, 0)

    @pl.when((n_att - 1) % 2 == 1)
    def _():
        step(n_att - 1, s2_ref, s_ref, False)

    for h in range(DSA_HEADS):
        inv = 1.0 / l_ref[hrows(h), :]
        o_lat = acc_ref[hrows(h), :] * jnp.concatenate([inv] * (DSA_LATENT // 128), axis=1)
        o_ref[:, h * DSA_DV:(h + 1) * DSA_DV] = _dot(o_lat.astype(BF16), wuv_ref[h]).astype(o_ref.dtype)


def _dsa(proj, misc, kidx, clat, w_uk, w_uv, band, batch, seq):
    nqb = seq // QBLK
    t = batch * seq
    npad = clat.shape[1]
    top_k = min(INDEX_TOPK, seq // 4)
    assert npad < 2 ** POS_BITS
    hq = DSA_HEADS * QBLK
    tokb = lambda b, i: b * nqb + i
    const3 = lambda b, i: (0, 0, 0)
    ngrp = (npad + GROUP_KEYS - 1) // GROUP_KEYS
    once = pl.Buffered(1)
    return pl.pallas_call(
        functools.partial(_dsa_kernel, top_k),
        grid=(batch, nqb),
        in_specs=[pl.BlockSpec((QBLK, 2048), lambda b, i: (tokb(b, i), 3)),
                  pl.BlockSpec((QBLK, 2048), lambda b, i: (tokb(b, i), 4)),
                  pl.BlockSpec((1, QBLK, 128), lambda b, i: (b, i + PAD_FRONT // QBLK, 0)),
                  pl.BlockSpec((1, npad, IDX_DIM), lambda b, i: (b, 0, 0), pipeline_mode=once),
                  pl.BlockSpec((1, npad, DSA_LATENT), lambda b, i: (b, 0, 0), pipeline_mode=once),
                  pl.BlockSpec((DSA_HEADS, DSA_DH, DSA_LATENT), const3, pipeline_mode=once),
                  pl.BlockSpec((DSA_HEADS, DSA_LATENT, DSA_DV), const3, pipeline_mode=once),
                  pl.BlockSpec((DSA_HEADS, QBLK, 2 * QBLK), const3, pipeline_mode=once)],
        out_specs=pl.BlockSpec((QBLK, DSA_HEADS * DSA_DV), lambda b, i: (tokb(b, i), 0)),
        out_shape=jax.ShapeDtypeStruct((t, DSA_HEADS * DSA_DV), BF16),
        scratch_shapes=[pltpu.VMEM((QBLK, npad), I32),
                        pltpu.VMEM((32, QBLK, ngrp * 128), I32),
                        pltpu.VMEM((QBLK, ngrp * 128), I32),
                        pltpu.VMEM((hq, IDX_DIM), BF16),
                        pltpu.VMEM((hq, 128), F32),
                        pltpu.VMEM((hq, DSA_LATENT), BF16),
                        pltpu.VMEM((QBLK, KCH), F32),
                        pltpu.VMEM((hq, KCH), F32),
                        pltpu.VMEM((hq, KCH), F32),
                        pltpu.VMEM((hq, KCH), BF16),
                        pltpu.VMEM((hq, 128), F32),
                        pltpu.VMEM((hq, 128), F32),
                        pltpu.VMEM((hq, 128), F32),
                        pltpu.VMEM((hq, DSA_LATENT), F32)],
        compiler_params=_cparams(("parallel", "arbitrary")),
        name="dsa",
    )(proj, proj, misc, kidx, clat, w_uk, w_uv, band)


def _mix_kernel(yg_ref, yd_ref, gg_ref, gd_ref, wg_ref, wd_ref, wm_ref, g_ref, h_ref, o_ref):
    a = _dot(yg_ref[...], wg_ref[...])
    b = _dot(yd_ref[...], wd_ref[...])
    gg = jax.nn.sigmoid(gg_ref[...].astype(F32))
    gd = jax.nn.sigmoid(gd_ref[...].astype(F32))
    merged = (gg * a + gd * b).astype(BF16)
    o_ref[...] = h_ref[...] + _rms(_dot(merged, wm_ref[...]), g_ref[...])


def _mix(y_gla, y_dsa, proj, w_g, w_d, w_m, gain, h, tm):
    t, d = y_gla.shape
    once = pl.Buffered(1)
    row = lambda i: (i, 0)
    fixed = lambda i: (0, 0)
    return pl.pallas_call(
        _mix_kernel,
        grid=(t // tm,),
        in_specs=[pl.BlockSpec((tm, d), row),
                  pl.BlockSpec((tm, d), row),
                  pl.BlockSpec((tm, d), lambda i: (i, 5)),
                  pl.BlockSpec((tm, d), lambda i: (i, 6)),
                  pl.BlockSpec((d, d), fixed, pipeline_mode=once),
                  pl.BlockSpec((d, d), fixed, pipeline_mode=once),
                  pl.BlockSpec((d, d), fixed, pipeline_mode=once),
                  pl.BlockSpec((1, d), fixed),
                  pl.BlockSpec((tm, d), row)],
        out_specs=pl.BlockSpec((tm, d), row),
        out_shape=jax.ShapeDtypeStruct((t, d), F32),
        compiler_params=_cparams(("parallel",)),
        name="mix",
    )(y_gla, y_dsa, proj, proj, w_g, w_d, w_m, gain, h)


def _xa_kernel(h_ref, g_ref, wq_ref, k_ref, v_ref, wo_ref, pg_ref, o_ref):
    h = h_ref[...]
    q = _dot(_rms(h, g_ref[...]).astype(BF16), wq_ref[...]).astype(BF16)
    outs = []
    for hd in range(XA_HEADS):
        hs = slice(hd * XA_DH, (hd + 1) * XA_DH)
        s = _dot_nt(q[:, hs], k_ref[0][:, hs]) * (XA_DH ** -0.5)
        m = jnp.max(s, axis=-1, keepdims=True)
        p = jnp.exp(s - m)
        p = p / jnp.sum(p, axis=-1, keepdims=True)
        outs.append(_dot(p.astype(BF16), v_ref[0][:, hs]).astype(BF16))
    y = _dot(jnp.concatenate(outs, axis=1), wo_ref[...])
    o_ref[...] = h + _rms(y, pg_ref[...])


def _xa(h, pre_gain, w_q, kv, w_o, post_gain, batch, seq, tm):
    t, d = h.shape
    n_mem = kv.shape[1]
    nb = seq // tm
    once = pl.Buffered(1)
    row = lambda b, i: (b * nb + i, 0)
    fixed = lambda b, i: (0, 0)
    return pl.pallas_call(
        _xa_kernel,
        grid=(batch, nb),
        in_specs=[pl.BlockSpec((tm, d), row),
                  pl.BlockSpec((1, d), fixed),
                  pl.BlockSpec((d, d), fixed, pipeline_mode=once),
                  pl.BlockSpec((1, n_mem, d), lambda b, i: (b, 0, 0)),
                  pl.BlockSpec((1, n_mem, d), lambda b, i: (b, 0, 1)),
                  pl.BlockSpec((d, d), fixed, pipeline_mode=once),
                  pl.BlockSpec((1, d), fixed)],
        out_specs=pl.BlockSpec((tm, d), row),
        out_shape=jax.ShapeDtypeStruct((t, d), F32),
        compiler_params=_cparams(("parallel", "parallel")),
        name="xa",
    )(h, pre_gain, w_q, kv, kv, w_o, post_gain)


def _ffn_kernel(h_ref, g_ref, wg_ref, wu_ref, wd_ref, pg_ref, o_ref, u_ref):
    f = pl.program_id(1)

    @pl.when(f == 0)
    def _():
        u_ref[...] = _rms(h_ref[...], g_ref[...]).astype(BF16)
        o_ref[...] = jnp.zeros_like(o_ref)

    u = u_ref[...]
    n_sub = max(wg_ref.shape[1] // 256, 1)
    half = wg_ref.shape[1] // n_sub
    down = None
    for s in range(n_sub):
        cs = slice(s * half, (s + 1) * half)
        a = _dot(u, wg_ref[:, cs])
        b = _dot(u, wu_ref[:, cs])
        act = (a * jax.nn.sigmoid(a) * b).astype(BF16)
        d = _dot(act, wd_ref[cs, :])
        down = d if down is None else down + d
    o_ref[...] += down

    @pl.when(f == pl.num_programs(1) - 1)
    def _():
        o_ref[...] = h_ref[...] + _rms(o_ref[...], pg_ref[...])


def _ffn(h, pre_gain, w_gate, w_up, w_down, post_gain, tm, tf):
    t, d = h.shape
    ff = w_gate.shape[1]
    return pl.pallas_call(
        _ffn_kernel,
        grid=(t // tm, ff // tf),
        in_specs=[pl.BlockSpec((tm, d), lambda i, f: (i, 0)),
                  pl.BlockSpec((1, d), lambda i, f: (0, 0)),
                  pl.BlockSpec((d, tf), lambda i, f: (0, f)),
                  pl.BlockSpec((d, tf), lambda i, f: (0, f)),
                  pl.BlockSpec((tf, d), lambda i, f: (f, 0)),
                  pl.BlockSpec((1, d), lambda i, f: (0, 0))],
        out_specs=pl.BlockSpec((tm, d), lambda i, f: (i, 0)),
        out_shape=jax.ShapeDtypeStruct((t, d), F32),
        scratch_shapes=[pltpu.VMEM((tm, d), BF16)],
        compiler_params=_cparams(("parallel", "arbitrary")),
        name="ffn",
    )(h, pre_gain, w_gate, w_up, w_down, post_gain)


def _row(v):
    return v.reshape(1, -1).astype(F32)


def _layer(h, mem, w_in, gla_w_a2, gla_b_a, gla_out_norm, dsa_w_uk, dsa_w_uv, dsa_latent_norm,
           idx_k_norm_w, idx_k_norm_b, band, w_gla_branch, w_dsa_branch, w_mix_out,
           mix_pre_norm, mix_post_norm, xa_pre_norm, xa_post_norm, xa_mem_norm,
           w_xa_q, w_xa_kv, w_xa_o, ffn_pre_norm, ffn_post_norm, w_ffn_gate, w_ffn_up, w_ffn_down,
           batch, seq):
    d = D_MODEL
    t = batch * seq
    cols = lambda a, b: w_in[:, a:b]
    w_main = jnp.concatenate(
        [cols(_O_GQ, _O_GA), cols(_O_DQ, _O_DC), cols(_O_IQ, _O_IK), cols(_O_GG, _O_END)], axis=1).astype(BF16)
    w_small = jnp.concatenate(
        [cols(_O_DC, _O_IQ), cols(_O_IK, _O_IW), cols(_O_GA, _O_DQ), cols(_O_IW, _O_GG),
         jnp.zeros((d, SMALL_COLS - 416), w_in.dtype)], axis=1).astype(BF16)

    pre = _row(mix_pre_norm)
    proj = _norm_matmul(h, pre, w_main, BF16, 1024, 1024)
    clat, kidx, misc = _small_proj(h, pre, w_small, _row(dsa_latent_norm), _row(idx_k_norm_w),
                                   _row(idx_k_norm_b), batch, seq)

    y_gla = _gla(proj, misc, gla_w_a2.astype(BF16), _row(gla_b_a), _row(gla_out_norm), batch, seq)
    y_dsa = _dsa(proj, misc, kidx, clat, dsa_w_uk.astype(BF16), dsa_w_uv.astype(BF16), band, batch, seq)

    h = _mix(y_gla, y_dsa, proj, w_gla_branch.astype(BF16), w_dsa_branch.astype(BF16),
             w_mix_out.astype(BF16), _row(mix_post_norm), h, 256)

    n_mem = mem.shape[1]
    kv = _norm_matmul(mem.reshape(batch * n_mem, d), _row(xa_mem_norm), w_xa_kv.astype(BF16), BF16,
                      batch * n_mem, 512).reshape(batch, n_mem, 2 * d)
    h = _xa(h, _row(xa_pre_norm), w_xa_q.astype(BF16), kv, w_xa_o.astype(BF16), _row(xa_post_norm),
            batch, seq, 512)

    h = _ffn(h, _row(ffn_pre_norm), w_ffn_gate.astype(BF16), w_ffn_up.astype(BF16),
             w_ffn_down.astype(BF16), _row(ffn_post_norm), 1024, 256)
    return h


def kernel(x, mem, w_in, gla_w_a2, gla_b_a, gla_out_norm, dsa_w_uk, dsa_w_uv, dsa_latent_norm,
           idx_k_norm_w, idx_k_norm_b, rel_bias, w_gla_branch, w_dsa_branch, w_mix_out,
           mix_pre_norm, mix_post_norm, xa_pre_norm, xa_post_norm, xa_mem_norm,
           w_xa_q, w_xa_kv, w_xa_o, ffn_pre_norm, ffn_post_norm, w_ffn_gate, w_ffn_up, w_ffn_down):
    batch, seq, d = x.shape
    depth = w_in.shape[0]
    band = _bias_band(rel_bias.astype(F32))
    h = x.reshape(batch * seq, d)
    for l in range(depth):
        h = _layer(h, mem, w_in[l], gla_w_a2[l], gla_b_a[l], gla_out_norm[l], dsa_w_uk[l], dsa_w_uv[l],
                   dsa_latent_norm[l], idx_k_norm_w[l], idx_k_norm_b[l], band, w_gla_branch[l],
                   w_dsa_branch[l], w_mix_out[l], mix_pre_norm[l], mix_post_norm[l], xa_pre_norm[l],
                   xa_post_norm[l], xa_mem_norm[l], w_xa_q[l], w_xa_kv[l], w_xa_o[l], ffn_pre_norm[l],
                   ffn_post_norm[l], w_ffn_gate[l], w_ffn_up[l], w_ffn_down[l], batch, seq)
    return h.reshape(batch, seq, d)
```

```python
import functools
import math

import jax
import jax.numpy as jnp
import numpy as np
from jax import lax
from jax.experimental import pallas as pl
from jax.experimental.pallas import tpu as pltpu

F32 = jnp.float32
BF16 = jnp.bfloat16
I32 = jnp.int32

D_MODEL = 2048
CHUNK = 64
EPS = 1e-6

GLA_HEADS = 4
GLA_DK = 256
GLA_DV = 512
GLA_RANK = 16
GLA_TAU = 16.0
GLA_SUB = 1

DSA_HEADS = 16
DSA_DH = 128
DSA_DV = 128
DSA_LATENT = 256
IDX_HEADS = 16
IDX_DIM = 128
INDEX_TOPK = 256

REL_BUCKETS = 32
REL_MAX_DIST = 128

XA_HEADS = 4
XA_DH = 512

_SPLITS = (1024, 1024, 2048, 2048, 16, 2048, 256, 2048, 128, 16, 2048, 2048)
_OFFS = tuple(int(v) for v in np.cumsum((0,) + _SPLITS))
(_O_GQ, _O_GK, _O_GV, _O_GR, _O_GA, _O_DQ, _O_DC, _O_IQ, _O_IK, _O_IW, _O_GG, _O_GD, _O_END) = _OFFS

SMALL_COLS = 512

QBLK = 128
KCH = 512
GROUP_KEYS = 32 * 128
POS_BITS = 15
PAD_FRONT = KCH
INT_MIN = -2 ** 31
NEG = -1e30
LOG2E = 1.4426950408889634

VMEM_LIMIT = 58 * 1024 * 1024


def _cparams(sem):
    return pltpu.CompilerParams(dimension_semantics=sem, vmem_limit_bytes=VMEM_LIMIT)


def _rms(x, gain):
    ms = jnp.mean(x * x, axis=-1, keepdims=True)
    return x * lax.rsqrt(ms + EPS) * gain


def _dot(a, b):
    return jnp.dot(a, b, preferred_element_type=F32)


def _dot_nt(a, b):
    return lax.dot_general(a, b, (((1,), (1,)), ((), ())), preferred_element_type=F32)


def _norm_matmul_kernel(x_ref, g_ref, w_ref, o_ref, u_ref):
    @pl.when(pl.program_id(1) == 0)
    def _():
        u_ref[...] = _rms(x_ref[...], g_ref[...]).astype(BF16)

    o_ref[...] = _dot(u_ref[...], w_ref[...]).astype(o_ref.dtype)


def _norm_matmul(x, gain, w, out_dtype, tm, tn):
    t, d = x.shape
    n = w.shape[1]
    return pl.pallas_call(
        _norm_matmul_kernel,
        grid=(t // tm, n // tn),
        in_specs=[pl.BlockSpec((tm, d), lambda i, j: (i, 0)),
                  pl.BlockSpec((1, d), lambda i, j: (0, 0)),
                  pl.BlockSpec((d, tn), lambda i, j: (0, j))],
        out_specs=pl.BlockSpec((tm, tn), lambda i, j: (i, j)),
        out_shape=jax.ShapeDtypeStruct((t, n), out_dtype),
        scratch_shapes=[pltpu.VMEM((tm, d), BF16)],
        compiler_params=_cparams(("parallel", "arbitrary")),
        name="norm_matmul",
    )(x, gain, w)


def _small_proj(x, gain, w_small, lat_g, ik_w, ik_b, batch, seq):
    tm = PAD_FRONT
    assert seq % tm == 0
    nblk = seq // tm
    npad = PAD_FRONT + seq
    grid = (batch, 1 + nblk)

    def x_map(b, i):
        return (b * nblk + jnp.maximum(i - 1, 0), 0)

    def kernel(x_ref, g_ref, w_ref, lat_g_ref, ik_w_ref, ik_b_ref, c_ref, k_ref, misc_ref):
        is_pad = pl.program_id(1) == 0

        @pl.when(is_pad)
        def _():
            c_ref[...] = jnp.zeros_like(c_ref)
            k_ref[...] = jnp.zeros_like(k_ref)
            misc_ref[...] = jnp.zeros_like(misc_ref)

        @pl.when(jnp.logical_not(is_pad))
        def _():
            u = _rms(x_ref[...], g_ref[...]).astype(BF16)
            p = _dot(u, w_ref[...])
            c_ref[0] = _rms(p[:, 0:256], lat_g_ref[...]).astype(BF16)
            ik = p[:, 256:384]
            mu = jnp.mean(ik, axis=-1, keepdims=True)
            xc = ik - mu
            var = jnp.mean(xc * xc, axis=-1, keepdims=True)
            k_ref[0] = (xc * lax.rsqrt(var + EPS) * ik_w_ref[...] + ik_b_ref[...]).astype(BF16)
            misc_ref[0] = p[:, 384:512]

    d = x.shape[1]
    const = lambda b, i: (0, 0)
    return pl.pallas_call(
        kernel,
        grid=grid,
        in_specs=[pl.BlockSpec((tm, d), x_map),
                  pl.BlockSpec((1, d), const),
                  pl.BlockSpec((d, SMALL_COLS), const),
                  pl.BlockSpec((1, DSA_LATENT), const),
                  pl.BlockSpec((1, IDX_DIM), const),
                  pl.BlockSpec((1, IDX_DIM), const)],
        out_specs=[pl.BlockSpec((1, tm, DSA_LATENT), lambda b, i: (b, i, 0)),
                   pl.BlockSpec((1, tm, IDX_DIM), lambda b, i: (b, i, 0)),
                   pl.BlockSpec((1, tm, 128), lambda b, i: (b, i, 0))],
        out_shape=[jax.ShapeDtypeStruct((batch, npad, DSA_LATENT), BF16),
                   jax.ShapeDtypeStruct((batch, npad, IDX_DIM), BF16),
                   jax.ShapeDtypeStruct((batch, npad, 128), F32)],
        compiler_params=_cparams(("parallel", "arbitrary")),
        name="small_proj",
    )(x, gain, w_small, lat_g, ik_w, ik_b)


def _log_sigmoid(z):
    return jnp.minimum(z, 0.0) - jnp.log1p(jnp.exp(-jnp.abs(z)))


def _split3(x):
    h = x.astype(BF16)
    r = x - h.astype(F32)
    m = r.astype(BF16)
    l = (r - m.astype(F32)).astype(BF16)
    return h, m, l


def _gla_kernel(nb, q_ref, k_ref, v_ref, r_ref, misc_ref, wa2_ref, ba_ref, on_ref, o_ref, state_ref):
    @pl.when(pl.program_id(0) == 0)
    def _():
        state_ref[...] = jnp.zeros_like(state_ref)

    rows = nb * GLA_SUB * CHUNK
    stack = lambda ref, cols: jnp.concatenate([ref[i][:, cols] for i in range(nb)], axis=0)
    row = lax.broadcasted_iota(I32, (rows, rows), 0)
    col = lax.broadcasted_iota(I32, (rows, rows), 1)
    same = (row // CHUNK) == (col // CHUNK)
    lower = jnp.logical_and(same, col <= row)
    upper = jnp.logical_and(same, col > row)
    tril = jnp.where(lower, 1.0, 0.0).astype(BF16)

    a_low = stack(misc_ref, slice(0, GLA_RANK)).astype(BF16)
    z = _dot(a_low, wa2_ref[...]) + ba_ref[...]
    la = _log_sigmoid(z) * (1.0 / GLA_TAU)
    l_h, l_m, l_l = _split3(la)
    b_all = _dot(tril, l_h) + _dot(tril, l_m) + _dot(tril, l_l)

    for h in range(GLA_HEADS):
        ks = slice(h * GLA_DK, (h + 1) * GLA_DK)
        vs = slice(h * GLA_DV, (h + 1) * GLA_DV)
        b = b_all[:, ks]
        eb = jnp.exp(b)
        ebi = jnp.exp(-b)
        q = stack(q_ref, ks).astype(F32) * (GLA_DK ** -0.5)
        k = stack(k_ref, ks).astype(F32)
        v = stack(v_ref, vs)
        q_fwd = (q * eb).astype(BF16)
        a_lo = _dot_nt(q_fwd, (k * ebi).astype(BF16))
        a_up = _dot_nt((q * ebi).astype(BF16), (k * eb).astype(BF16))
        scores = jnp.where(lower, a_lo, jnp.where(upper, a_up, 0.0)).astype(BF16)
        o_intra = _dot(scores, v)
        outs = []
        for i in range(nb):
            st = state_ref[i * GLA_HEADS + h]
            for sub in range(GLA_SUB):
                rs = slice((i * GLA_SUB + sub) * CHUNK, (i * GLA_SUB + sub + 1) * CHUNK)
                b_i = b[rs]
                b_last = b_i[CHUNK - 1:CHUNK, :]
                outs.append(o_intra[rs] + _dot_nt(q_fwd[rs], st.astype(BF16)))
                k_dec = (k[rs] * jnp.exp(b_last - b_i)).astype(BF16)
                v_t = v[rs].astype(F32).T.astype(BF16)
                st = st * jnp.exp(b_last) + _dot(v_t, k_dec)
            state_ref[i * GLA_HEADS + h] = st
        o = _rms(jnp.concatenate(outs, axis=0), on_ref[...])
        r = stack(r_ref, vs).astype(F32)
        y = (o * (r * jax.nn.sigmoid(r))).astype(o_ref.dtype)
        blk = GLA_SUB * CHUNK
        for i in range(nb):
            o_ref[i, :, vs] = y[i * blk:(i + 1) * blk]


def _gla(proj, misc, w_a2, b_a, out_norm, batch, seq):
    blk = GLA_SUB * CHUNK
    assert seq % blk == 0 and PAD_FRONT % blk == 0
    proj3 = proj.reshape(batch, seq, proj.shape[1])
    y = pl.pallas_call(
        functools.partial(_gla_kernel, batch),
        grid=(seq // blk,),
        in_specs=[pl.BlockSpec((batch, blk, 1024), lambda c: (0, c, 0)),
                  pl.BlockSpec((batch, blk, 1024), lambda c: (0, c, 1)),
                  pl.BlockSpec((batch, blk, 2048), lambda c: (0, c, 1)),
                  pl.BlockSpec((batch, blk, 2048), lambda c: (0, c, 2)),
                  pl.BlockSpec((batch, blk, 128), lambda c: (0, c + PAD_FRONT // blk, 0)),
                  pl.BlockSpec((GLA_RANK, GLA_HEADS * GLA_DK), lambda c: (0, 0)),
                  pl.BlockSpec((1, GLA_HEADS * GLA_DK), lambda c: (0, 0)),
                  pl.BlockSpec((1, GLA_DV), lambda c: (0, 0))],
        out_specs=pl.BlockSpec((batch, blk, GLA_HEADS * GLA_DV), lambda c: (0, c, 0)),
        out_shape=jax.ShapeDtypeStruct((batch, seq, GLA_HEADS * GLA_DV), BF16),
        scratch_shapes=[pltpu.VMEM((batch * GLA_HEADS, GLA_DV, GLA_DK), F32)],
        compiler_params=_cparams(("arbitrary",)),
        name="gla",
    )(proj3, proj3, proj3, proj3, misc, w_a2, b_a, out_norm)
    return y.reshape(batch * seq, GLA_HEADS * GLA_DV)


def _t5_bucket(rel):
    half = REL_BUCKETS // 2
    max_exact = half // 2
    ret = jnp.where(rel > 0, half, 0)
    n = jnp.abs(rel)
    nf = jnp.maximum(n, 1).astype(jnp.float32)
    large = max_exact + (jnp.log(nf / max_exact) / math.log(REL_MAX_DIST / max_exact)
                         * (half - max_exact)).astype(jnp.int32)
    large = jnp.minimum(large, half - 1)
    return ret + jnp.where(n < max_exact, n, large)


def _bias_band_kernel(bucket_ref, rb_ref, o_ref):
    far = REL_BUCKETS // 2 - 1
    bucket = bucket_ref[...]
    for h in range(DSA_HEADS):
        acc = jnp.zeros(bucket.shape, F32)
        for b in range(REL_BUCKETS):
            acc = jnp.where(bucket == b, rb_ref[b, h], acc)
        o_ref[h] = (acc - rb_ref[far, h]) * LOG2E


def _bias_band(rel_bias):
    t = jnp.arange(QBLK, dtype=jnp.int32)[:, None]
    j = jnp.arange(2 * QBLK, dtype=jnp.int32)[None, :]
    bucket = _t5_bucket(j - QBLK - t).astype(jnp.int32)
    return pl.pallas_call(
        _bias_band_kernel,
        in_specs=[pl.BlockSpec(memory_space=pltpu.VMEM), pl.BlockSpec(memory_space=pltpu.SMEM)],
        out_specs=pl.BlockSpec(memory_space=pltpu.VMEM),
        out_shape=jax.ShapeDtypeStruct((DSA_HEADS, QBLK, 2 * QBLK), F32),
        name="bias_band",
    )(bucket, rel_bias)


_SWAP_MASK = {16: 0x0000FFFF, 8: 0x00FF00FF, 4: 0x0F0F0F0F, 2: 0x33333333, 1: 0x55555555}


def _transpose_stages(words, stages):
    a = list(words)
    for j in stages:
        for k in range(len(a)):
            if k & j == 0:
                t = (a[k] ^ lax.shift_right_logical(a[k + j], jnp.int32(j))) & jnp.int32(_SWAP_MASK[j])
                a[k], a[k + j] = a[k] ^ t, a[k + j] ^ (t << j)
    return a


def _sortable(x):
    i = pltpu.bitcast(x, I32)
    return jnp.where(i < 0, i ^ jnp.int32(0x7FFFFFFF), i)


def _dsa_kernel(top_k, dq_ref, iq_ref, misc_ref, kidx_ref, clat_ref, wuk_ref, wuv_ref, band_ref,
                o_ref, keys_ref, planes_ref, eq_ref, iqs_ref, wb_ref, qlat_ref, madd_ref, s_ref, s2_ref, p_ref, alpha_ref,
                m_ref, l_ref, acc_ref):
    qb = pl.program_id(1)
    start = qb * QBLK
    hrows = lambda h: slice(h * QBLK, (h + 1) * QBLK)

    w_scale = IDX_HEADS ** -0.5 * IDX_DIM ** -0.5
    wq = misc_ref[0][:, GLA_RANK:GLA_RANK + IDX_HEADS] * w_scale
    for h in range(IDX_HEADS):
        wb_ref[hrows(h), :] = jnp.broadcast_to(wq[:, h:h + 1], (QBLK, 128))
        iqs_ref[hrows(h), :] = iq_ref[:, h * IDX_DIM:(h + 1) * IDX_DIM]

    row = lax.broadcasted_iota(I32, (QBLK, KCH), 0)
    lane = lax.broadcasted_iota(I32, (QBLK, KCH), 1)
    p_lim = start + (row // CHUNK + 1) * CHUNK + PAD_FRONT

    n_chunks = (start + PAD_FRONT + QBLK + KCH - 1) // KCH
    tiles_per_chunk = KCH // 128
    chunks_per_group = GROUP_KEYS // KCH

    def stage_planes(c, key):
        tiles = [key[:, i * 128:(i + 1) * 128] for i in range(tiles_per_chunk)]
        tiles = _transpose_stages(tiles, (2, 1))
        g = c // chunks_per_group
        w0 = (c % chunks_per_group) * tiles_per_chunk
        for i in range(tiles_per_chunk):
            planes_ref[w0 + i, :, pl.ds(pl.multiple_of(g * 128, 128), 128)] = tiles[i]

    no_key = jnp.full((QBLK, KCH), INT_MIN, I32)
    keys_ref[:, 0:KCH] = no_key
    stage_planes(jnp.int32(0), no_key)

    def index_chunk(c, dots_ref):
        off = pl.multiple_of(c * KCH, KCH)
        kc = kidx_ref[0, pl.ds(off, KCH), :]
        dots_ref[...] = _dot_nt(iqs_ref[...], kc)
        acc = jnp.zeros((QBLK, KCH), F32)
        for h in range(IDX_HEADS):
            wbh = wb_ref[hrows(h), :]
            acc = acc + jnp.concatenate([wbh] * (KCH // 128), axis=1) * jnp.maximum(dots_ref[hrows(h), :], 0.0)
        key = jnp.where(lane + off < p_lim, _sortable(acc), INT_MIN)
        keys_ref[:, pl.ds(off, KCH)] = key
        stage_planes(c, key)

    n_odd = (n_chunks - 1) % 2

    @pl.when(n_odd == 1)
    def _():
        index_chunk(1, s_ref)

    def idx_body(i, carry):
        c = 1 + n_odd + 2 * i
        index_chunk(c, s_ref)
        index_chunk(c + 1, s2_ref)
        return carry

    lax.fori_loop(0, (n_chunks - 1) // 2, idx_body, 0)

    n_groups = (n_chunks + chunks_per_group - 1) // chunks_per_group
    ngrp_max = eq_ref.shape[1] // 128

    def pad_body(c, carry):
        stage_planes(c, no_key)
        return carry

    lax.fori_loop(n_chunks, n_groups * chunks_per_group, pad_body, 0)

    def plane_body(idx, carry):
        g = idx // (QBLK // 16)
        gl = pl.ds(pl.multiple_of(g * 128, 128), 128)
        for half in range(2):
            r0 = pl.multiple_of((idx % (QBLK // 16)) * 16 + half * 8, 8)
            words = [planes_ref[j, pl.ds(r0, 8), gl] for j in range(32)]
            words = _transpose_stages(words, (16, 8, 4))
            words[0] = ~words[0]
            for i in range(32):
                planes_ref[31 - i, pl.ds(r0, 8), gl] = words[i]
        return carry

    lax.fori_loop(0, n_groups * (QBLK // 16), plane_body, 0)

    for g in range(ngrp_max):
        eq_ref[:, g * 128:(g + 1) * 128] = jnp.broadcast_to(jnp.where(g < n_groups, -1, 0), (QBLK, 128))

    def row_count(t):
        pc = lax.population_count(t)
        tot = pc[:, 0:128]
        for g in range(1, ngrp_max):
            tot = tot + pc[:, g * 128:(g + 1) * 128]
        return jnp.broadcast_to(jnp.sum(tot.astype(F32), axis=1, keepdims=True), (QBLK, 128))

    def pair_body(i, carry):
        prefix, above = carry
        b0 = 30 - 2 * i
        p1 = planes_ref[b0 + 1]
        p0 = planes_ref[b0]
        eq = eq_ref[...]
        e1 = eq & p1
        e0 = eq & ~p1
        t11 = e1 & p0
        t10 = e1 & ~p0
        t01 = e0 & p0
        t00 = e0 & ~p0
        s3 = above + row_count(t11)
        s2 = s3 + row_count(t10)
        s1 = s2 + row_count(t01)
        is3 = s3 >= top_k
        is2 = s2 >= top_k
        is1 = s1 >= top_k
        for g in range(ngrp_max):
            gs = slice(g * 128, (g + 1) * 128)
            eq_ref[:, gs] = jnp.where(is3, t11[:, gs], jnp.where(is2, t10[:, gs], jnp.where(is1, t01[:, gs], t00[:, gs])))
        above = jnp.where(is3, above, jnp.where(is2, s3, jnp.where(is1, s2, s1)))
        digit = jnp.where(is3, 3, jnp.where(is2, 2, jnp.where(is1, 1, 0)))
        return prefix | (digit << b0), above

    end = start + PAD_FRONT + QBLK
    n_att = (start + QBLK + KCH - 1) // KCH

    def chunk_off(j):
        return pl.multiple_of(end - KCH * (j + 1), 128)

    def absorbed_query(h):
        ql = _dot(dq_ref[:, h * DSA_DH:(h + 1) * DSA_DH], wuk_ref[h]) * (LOG2E * DSA_DH ** -0.5)
        qlat_ref[hrows(h), :] = ql.astype(BF16)

    def first_scores(h):
        s_ref[hrows(h), :] = _dot_nt(qlat_ref[hrows(h), :], clat_ref[0, pl.ds(chunk_off(0), KCH), :])

    carry = (jnp.zeros((QBLK, 128), I32), jnp.zeros((QBLK, 128), F32))
    for i in range(16):
        carry = pair_body(i, carry)
        for h in range(i * DSA_HEADS // 16, (i + 1) * DSA_HEADS // 16):
            absorbed_query(h)
            if h > 0:
                first_scores(h - 1)
    first_scores(DSA_HEADS - 1)
    prefix, above = carry
    thr = prefix ^ INT_MIN
    thr = jnp.maximum(thr, INT_MIN + 1)

    surplus = jnp.where(prefix != 0, above + row_count(eq_ref[...]) - top_k, 0.0)

    def reset_softmax():
        m_ref[...] = jnp.full(m_ref.shape, NEG, F32)
        l_ref[...] = jnp.zeros(l_ref.shape, F32)
        acc_ref[...] = jnp.zeros(acc_ref.shape, F32)

    def demote_surplus_ties():
        keep = top_k - above
        lane1 = lax.broadcasted_iota(I32, (QBLK, 128), 1)

        def tied_before(q):
            def body(c, cnt):
                off = pl.multiple_of(c * KCH, KCH)
                kk = keys_ref[:, pl.ds(off, KCH)]
                for s in range(KCH // 128):
                    hit = jnp.logical_and(kk[:, s * 128:(s + 1) * 128] == thr, lane1 + (off + s * 128) < q)
                    cnt = cnt + jnp.where(hit, 1, 0)
                return cnt
            cnt = lax.fori_loop(1, n_chunks, body, jnp.zeros((QBLK, 128), I32))
            return jnp.broadcast_to(jnp.sum(cnt.astype(F32), axis=1, keepdims=True), (QBLK, 128))

        def pos_body(i, q):
            cand = q | (jnp.int32(1) << (POS_BITS - 1 - i))
            return jnp.where(tied_before(cand) < keep, cand, q)

        last = lax.fori_loop(0, POS_BITS, pos_body, jnp.zeros((QBLK, 128), I32))
        last = jnp.where(surplus > 0.0, last, jnp.int32(2 ** POS_BITS))

        def demote_body(c, carry):
            off = pl.multiple_of(c * KCH, KCH)
            for s in range(KCH // 128):
                cs = pl.ds(off + s * 128, 128)
                kk = keys_ref[:, cs]
                drop = jnp.logical_and(kk == thr, lane1 + (off + s * 128) > last)
                keys_ref[:, cs] = jnp.where(drop, INT_MIN, kk)
            return carry

        lax.fori_loop(1, n_chunks, demote_body, 0)

    reset_softmax()

    def scores(j, dst_ref):
        cc = clat_ref[0, pl.ds(chunk_off(j), KCH), :]
        dst_ref[...] = _dot_nt(qlat_ref[...], cc)

    def softmax_update(j, src_ref, near):
        off = chunk_off(j)
        kk = keys_ref[:, pl.ds(off, KCH)]
        madd_ref[...] = jnp.where(kk >= jnp.concatenate([thr] * (KCH // 128), axis=1), 0.0, NEG)
        cc = clat_ref[0, pl.ds(off, KCH), :]
        for h in range(DSA_HEADS):
            s = src_ref[hrows(h), :] + madd_ref[...]
            if near:
                s = jnp.concatenate([s[:, :KCH - 2 * QBLK], s[:, KCH - 2 * QBLK:] + band_ref[h]], axis=1)
            m_old = m_ref[hrows(h), :]
            m_new = jnp.maximum(m_old, jnp.broadcast_to(jnp.max(s, axis=1, keepdims=True), (QBLK, 128)))
            alpha = jnp.exp2(m_old - m_new)
            p = jnp.exp2(s - jnp.concatenate([m_new] * (KCH // 128), axis=1))
            l_ref[hrows(h), :] = (alpha * l_ref[hrows(h), :]
                                  + jnp.broadcast_to(jnp.sum(p, axis=1, keepdims=True), (QBLK, 128)))
            m_ref[hrows(h), :] = m_new
            alpha_ref[hrows(h), :] = alpha
            p_ref[hrows(h), :] = p.astype(BF16)
        al = alpha_ref[...]
        acc_ref[...] = (jnp.concatenate([al] * (DSA_LATENT // 128), axis=1) * acc_ref[...]
                        + _dot(p_ref[...], cc))

    def step(j, cur_ref, nxt_ref, near):
        scores(jnp.minimum(j + 1, n_att - 1), nxt_ref)
        softmax_update(j, cur_ref, near)

    step(0, s_ref, s2_ref, True)

    @pl.when(jnp.max(surplus) > 0.0)
    def _():
        demote_surplus_ties()
        reset_softmax()
        softmax_update(0, s_ref, True)

    def att_body(i, carry):
        j = 2 * i + 1
        step(j, s2_ref, s_ref, False)
        step(j + 1, s_ref, s2_ref, False)
        return carry

    lax.fori_loop(0, (n_att - 1) // 2, att_body, 0)

    @pl.when((n_att - 1) % 2 == 1)
    def _():
        step(n_att - 1, s2_ref, s_ref, False)

    for h in range(DSA_HEADS):
        inv = 1.0 / l_ref[hrows(h), :]
        o_lat = acc_ref[hrows(h), :] * jnp.concatenate([inv] * (DSA_LATENT // 128), axis=1)
        o_ref[:, h * DSA_DV:(h + 1) * DSA_DV] = _dot(o_lat.astype(BF16), wuv_ref[h]).astype(o_ref.dtype)


def _dsa(proj, misc, kidx, clat, w_uk, w_uv, band, batch, seq):
    nqb = seq // QBLK
    t = batch * seq
    npad = clat.shape[1]
    top_k = min(INDEX_TOPK, seq // 4)
    assert npad < 2 ** POS_BITS
    hq = DSA_HEADS * QBLK
    tokb = lambda b, i: b * nqb + i
    const3 = lambda b, i: (0, 0, 0)
    ngrp = (npad + GROUP_KEYS - 1) // GROUP_KEYS
    once = pl.Buffered(1)
    return pl.pallas_call(
        functools.partial(_dsa_kernel, top_k),
        grid=(batch, nqb),
        in_specs=[pl.BlockSpec((QBLK, 2048), lambda b, i: (tokb(b, i), 3)),
                  pl.BlockSpec((QBLK, 2048), lambda b, i: (tokb(b, i), 4)),
                  pl.BlockSpec((1, QBLK, 128), lambda b, i: (b, i + PAD_FRONT // QBLK, 0)),
                  pl.BlockSpec((1, npad, IDX_DIM), lambda b, i: (b, 0, 0), pipeline_mode=once),
                  pl.BlockSpec((1, npad, DSA_LATENT), lambda b, i: (b, 0, 0), pipeline_mode=once),
                  pl.BlockSpec((DSA_HEADS, DSA_DH, DSA_LATENT), const3, pipeline_mode=once),
                  pl.BlockSpec((DSA_HEADS, DSA_LATENT, DSA_DV), const3, pipeline_mode=once),
                  pl.BlockSpec((DSA_HEADS, QBLK, 2 * QBLK), const3, pipeline_mode=once)],
        out_specs=pl.BlockSpec((QBLK, DSA_HEADS * DSA_DV), lambda b, i: (tokb(b, i), 0)),
        out_shape=jax.ShapeDtypeStruct((t, DSA_HEADS * DSA_DV), BF16),
        scratch_shapes=[pltpu.VMEM((QBLK, npad), I32),
                        pltpu.VMEM((32, QBLK, ngrp * 128), I32),
                        pltpu.VMEM((QBLK, ngrp * 128), I32),
                        pltpu.VMEM((hq, IDX_DIM), BF16),
                        pltpu.VMEM((hq, 128), F32),
                        pltpu.VMEM((hq, DSA_LATENT), BF16),
                        pltpu.VMEM((QBLK, KCH), F32),
                        pltpu.VMEM((hq, KCH), F32),
                        pltpu.VMEM((hq, KCH), F32),
                        pltpu.VMEM((hq, KCH), BF16),
                        pltpu.VMEM((hq, 128), F32),
                        pltpu.VMEM((hq, 128), F32),
                        pltpu.VMEM((hq, 128), F32),
                        pltpu.VMEM((hq, DSA_LATENT), F32)],
        compiler_params=_cparams(("parallel", "arbitrary")),
        name="dsa",
    )(proj, proj, misc, kidx, clat, w_uk, w_uv, band)


def _mix_kernel(yg_ref, yd_ref, gg_ref, gd_ref, wg_ref, wd_ref, wm_ref, g_ref, h_ref, o_ref):
    a = _dot(yg_ref[...], wg_ref[...])
    b = _dot(yd_ref[...], wd_ref[...])
    gg = jax.nn.sigmoid(gg_ref[...].astype(F32))
    gd = jax.nn.sigmoid(gd_ref[...].astype(F32))
    merged = (gg * a + gd * b).astype(BF16)
    o_ref[...] = h_ref[...] + _rms(_dot(merged, wm_ref[...]), g_ref[...])


def _mix(y_gla, y_dsa, proj, w_g, w_d, w_m, gain, h, tm):
    t, d = y_gla.shape
    once = pl.Buffered(1)
    row = lambda i: (i, 0)
    fixed = lambda i: (0, 0)
    return pl.pallas_call(
        _mix_kernel,
        grid=(t // tm,),
        in_specs=[pl.BlockSpec((tm, d), row),
                  pl.BlockSpec((tm, d), row),
                  pl.BlockSpec((tm, d), lambda i: (i, 5)),
                  pl.BlockSpec((tm, d), lambda i: (i, 6)),
                  pl.BlockSpec((d, d), fixed, pipeline_mode=once),
                  pl.BlockSpec((d, d), fixed, pipeline_mode=once),
                  pl.BlockSpec((d, d), fixed, pipeline_mode=once),
                  pl.BlockSpec((1, d), fixed),
                  pl.BlockSpec((tm, d), row)],
        out_specs=pl.BlockSpec((tm, d), row),
        out_shape=jax.ShapeDtypeStruct((t, d), F32),
        compiler_params=_cparams(("parallel",)),
        name="mix",
    )(y_gla, y_dsa, proj, proj, w_g, w_d, w_m, gain, h)


def _xa_kernel(h_ref, g_ref, wq_ref, k_ref, v_ref, wo_ref, pg_ref, o_ref):
    h = h_ref[...]
    q = _dot(_rms(h, g_ref[...]).astype(BF16), wq_ref[...]).astype(BF16)
    outs = []
    for hd in range(XA_HEADS):
        hs = slice(hd * XA_DH, (hd + 1) * XA_DH)
        s = _dot_nt(q[:, hs], k_ref[0][:, hs]) * (XA_DH ** -0.5)
        m = jnp.max(s, axis=-1, keepdims=True)
        p = jnp.exp(s - m)
        p = p / jnp.sum(p, axis=-1, keepdims=True)
        outs.append(_dot(p.astype(BF16), v_ref[0][:, hs]).astype(BF16))
    y = _dot(jnp.concatenate(outs, axis=1), wo_ref[...])
    o_ref[...] = h + _rms(y, pg_ref[...])


def _xa(h, pre_gain, w_q, kv, w_o, post_gain, batch, seq, tm):
    t, d = h.shape
    n_mem = kv.shape[1]
    nb = seq // tm
    once = pl.Buffered(1)
    row = lambda b, i: (b * nb + i, 0)
    fixed = lambda b, i: (0, 0)
    return pl.pallas_call(
        _xa_kernel,
        grid=(batch, nb),
        in_specs=[pl.BlockSpec((tm, d), row),
                  pl.BlockSpec((1, d), fixed),
                  pl.BlockSpec((d, d), fixed, pipeline_mode=once),
                  pl.BlockSpec((1, n_mem, d), lambda b, i: (b, 0, 0)),
                  pl.BlockSpec((1, n_mem, d), lambda b, i: (b, 0, 1)),
                  pl.BlockSpec((d, d), fixed, pipeline_mode=once),
                  pl.BlockSpec((1, d), fixed)],
        out_specs=pl.BlockSpec((tm, d), row),
        out_shape=jax.ShapeDtypeStruct((t, d), F32),
        compiler_params=_cparams(("parallel", "parallel")),
        name="xa",
    )(h, pre_gain, w_q, kv, kv, w_o, post_gain)


def _ffn_kernel(h_ref, g_ref, wg_ref, wu_ref, wd_ref, pg_ref, o_ref, u_ref):
    f = pl.program_id(1)

    @pl.when(f == 0)
    def _():
        u_ref[...] = _rms(h_ref[...], g_ref[...]).astype(BF16)
        o_ref[...] = jnp.zeros_like(o_ref)

    u = u_ref[...]
    a = _dot(u, wg_ref[...])
    b = _dot(u, wu_ref[...])
    act = (a * jax.nn.sigmoid(a) * b).astype(BF16)
    o_ref[...] += _dot(act, wd_ref[...])

    @pl.when(f == pl.num_programs(1) - 1)
    def _():
        o_ref[...] = h_ref[...] + _rms(o_ref[...], pg_ref[...])


def _ffn(h, pre_gain, w_gate, w_up, w_down, post_gain, tm, tf):
    t, d = h.shape
    ff = w_gate.shape[1]
    return pl.pallas_call(
        _ffn_kernel,
        grid=(t // tm, ff // tf),
        in_specs=[pl.BlockSpec((tm, d), lambda i, f: (i, 0)),
                  pl.BlockSpec((1, d), lambda i, f: (0, 0)),
                  pl.BlockSpec((d, tf), lambda i, f: (0, f)),
                  pl.BlockSpec((d, tf), lambda i, f: (0, f)),
                  pl.BlockSpec((tf, d), lambda i, f: (f, 0)),
                  pl.BlockSpec((1, d), lambda i, f: (0, 0))],
        out_specs=pl.BlockSpec((tm, d), lambda i, f: (i, 0)),
        out_shape=jax.ShapeDtypeStruct((t, d), F32),
        scratch_shapes=[pltpu.VMEM((tm, d), BF16)],
        compiler_params=_cparams(("parallel", "arbitrary")),
        name="ffn",
    )(h, pre_gain, w_gate, w_up, w_down, post_gain)


def _row(v):
    return v.reshape(1, -1).astype(F32)


def _layer(h, mem, w_in, gla_w_a2, gla_b_a, gla_out_norm, dsa_w_uk, dsa_w_uv, dsa_latent_norm,
           idx_k_norm_w, idx_k_norm_b, band, w_gla_branch, w_dsa_branch, w_mix_out,
           mix_pre_norm, mix_post_norm, xa_pre_norm, xa_post_norm, xa_mem_norm,
           w_xa_q, w_xa_kv, w_xa_o, ffn_pre_norm, ffn_post_norm, w_ffn_gate, w_ffn_up, w_ffn_down,
           batch, seq):
    d = D_MODEL
    t = batch * seq
    cols = lambda a, b: w_in[:, a:b]
    w_main = jnp.concatenate(
        [cols(_O_GQ, _O_GA), cols(_O_DQ, _O_DC), cols(_O_IQ, _O_IK), cols(_O_GG, _O_END)], axis=1).astype(BF16)
    w_small = jnp.concatenate(
        [cols(_O_DC, _O_IQ), cols(_O_IK, _O_IW), cols(_O_GA, _O_DQ), cols(_O_IW, _O_GG),
         jnp.zeros((d, SMALL_COLS - 416), w_in.dtype)], axis=1).astype(BF16)

    pre = _row(mix_pre_norm)
    proj = _norm_matmul(h, pre, w_main, BF16, 1024, 1024)
    clat, kidx, misc = _small_proj(h, pre, w_small, _row(dsa_latent_norm), _row(idx_k_norm_w),
                                   _row(idx_k_norm_b), batch, seq)

    y_gla = _gla(proj, misc, gla_w_a2.astype(BF16), _row(gla_b_a), _row(gla_out_norm), batch, seq)
    y_dsa = _dsa(proj, misc, kidx, clat, dsa_w_uk.astype(BF16), dsa_w_uv.astype(BF16), band, batch, seq)

    h = _mix(y_gla, y_dsa, proj, w_gla_branch.astype(BF16), w_dsa_branch.astype(BF16),
             w_mix_out.astype(BF16), _row(mix_post_norm), h, 256)

    n_mem = mem.shape[1]
    kv = _norm_matmul(mem.reshape(batch * n_mem, d), _row(xa_mem_norm), w_xa_kv.astype(BF16), BF16,
                      batch * n_mem, 512).reshape(batch, n_mem, 2 * d)
    h = _xa(h, _row(xa_pre_norm), w_xa_q.astype(BF16), kv, w_xa_o.astype(BF16), _row(xa_post_norm),
            batch, seq, 512)

    h = _ffn(h, _row(ffn_pre_norm), w_ffn_gate.astype(BF16), w_ffn_up.astype(BF16),
             w_ffn_down.astype(BF16), _row(ffn_post_norm), 1024, 256)
    return h


def kernel(x, mem, w_in, gla_w_a2, gla_b_a, gla_out_norm, dsa_w_uk, dsa_w_uv, dsa_latent_norm,
           idx_k_norm_w, idx_k_norm_b, rel_bias, w_gla_branch, w_dsa_branch, w_mix_out,
           mix_pre_norm, mix_post_norm, xa_pre_norm, xa_post_norm, xa_mem_norm,
           w_xa_q, w_xa_kv, w_xa_o, ffn_pre_norm, ffn_post_norm, w_ffn_gate, w_ffn_up, w_ffn_down):
    batch, seq, d = x.shape
    depth = w_in.shape[0]
    band = _bias_band(rel_bias.astype(F32))
    h = x.reshape(batch * seq, d)
    for l in range(depth):
        h = _layer(h, mem, w_in[l], gla_w_a2[l], gla_b_a[l], gla_out_norm[l], dsa_w_uk[l], dsa_w_uv[l],
                   dsa_latent_norm[l], idx_k_norm_w[l], idx_k_norm_b[l], band, w_gla_branch[l],
                   w_dsa_branch[l], w_mix_out[l], mix_pre_norm[l], mix_post_norm[l], xa_pre_norm[l],
                   xa_post_norm[l], xa_mem_norm[l], w_xa_q[l], w_xa_kv[l], w_xa_o[l], ffn_pre_norm[l],
                   ffn_post_norm[l], w_ffn_gate[l], w_ffn_up[l], w_ffn_down[l], batch, seq)
    return h.reshape(batch, seq, d)
```

```python
import functools
import math

import jax
import jax.numpy as jnp
import numpy as np
from jax import lax
from jax.experimental import pallas as pl
from jax.experimental.pallas import tpu as pltpu

F32 = jnp.float32
BF16 = jnp.bfloat16
I32 = jnp.int32

D_MODEL = 2048
CHUNK = 64
EPS = 1e-6

GLA_HEADS = 4
GLA_DK = 256
GLA_DV = 512
GLA_RANK = 16
GLA_TAU = 16.0
GLA_SUB = 1

DSA_HEADS = 16
DSA_DH = 128
DSA_DV = 128
DSA_LATENT = 256
IDX_HEADS = 16
IDX_DIM = 128
INDEX_TOPK = 256

REL_BUCKETS = 32
REL_MAX_DIST = 128

XA_HEADS = 4
XA_DH = 512

_SPLITS = (1024, 1024, 2048, 2048, 16, 2048, 256, 2048, 128, 16, 2048, 2048)
_OFFS = tuple(int(v) for v in np.cumsum((0,) + _SPLITS))
(_O_GQ, _O_GK, _O_GV, _O_GR, _O_GA, _O_DQ, _O_DC, _O_IQ, _O_IK, _O_IW, _O_GG, _O_GD, _O_END) = _OFFS

MAIN_COLS = 14336
SMALL_COLS = 512

QBLK = 128
KCH = 512
GROUP_KEYS = 32 * 128
POS_BITS = 15
PAD_FRONT = KCH
INT_MIN = -2 ** 31
NEG = -1e30
LOG2E = 1.4426950408889634

VMEM_LIMIT = 58 * 1024 * 1024


def _cparams(sem):
    return pltpu.CompilerParams(dimension_semantics=sem, vmem_limit_bytes=VMEM_LIMIT)


def _rms(x, gain):
    ms = jnp.mean(x * x, axis=-1, keepdims=True)
    return x * lax.rsqrt(ms + EPS) * gain


def _dot(a, b):
    return jnp.dot(a, b, preferred_element_type=F32)


def _dot_nt(a, b):
    return lax.dot_general(a, b, (((1,), (1,)), ((), ())), preferred_element_type=F32)


def _norm_matmul_kernel(x_ref, g_ref, w_ref, o_ref, u_ref):
    @pl.when(pl.program_id(1) == 0)
    def _():
        u_ref[...] = _rms(x_ref[...], g_ref[...]).astype(BF16)

    o_ref[...] = _dot(u_ref[...], w_ref[...]).astype(o_ref.dtype)


def _norm_matmul(x, gain, w, out_dtype, tm, tn):
    t, d = x.shape
    n = w.shape[1]
    return pl.pallas_call(
        _norm_matmul_kernel,
        grid=(t // tm, n // tn),
        in_specs=[pl.BlockSpec((tm, d), lambda i, j: (i, 0)),
                  pl.BlockSpec((1, d), lambda i, j: (0, 0)),
                  pl.BlockSpec((d, tn), lambda i, j: (0, j))],
        out_specs=pl.BlockSpec((tm, tn), lambda i, j: (i, j)),
        out_shape=jax.ShapeDtypeStruct((t, n), out_dtype),
        scratch_shapes=[pltpu.VMEM((tm, d), BF16)],
        compiler_params=_cparams(("parallel", "arbitrary")),
        name="norm_matmul",
    )(x, gain, w)


def _small_proj(x, gain, w_small, lat_g, ik_w, ik_b, batch, seq):
    tm = PAD_FRONT
    assert seq % tm == 0
    nblk = seq // tm
    npad = PAD_FRONT + seq
    grid = (batch, 1 + nblk)

    def x_map(b, i):
        return (b * nblk + jnp.maximum(i - 1, 0), 0)

    def kernel(x_ref, g_ref, w_ref, lat_g_ref, ik_w_ref, ik_b_ref, c_ref, k_ref, misc_ref):
        is_pad = pl.program_id(1) == 0

        @pl.when(is_pad)
        def _():
            c_ref[...] = jnp.zeros_like(c_ref)
            k_ref[...] = jnp.zeros_like(k_ref)
            misc_ref[...] = jnp.zeros_like(misc_ref)

        @pl.when(jnp.logical_not(is_pad))
        def _():
            u = _rms(x_ref[...], g_ref[...]).astype(BF16)
            p = _dot(u, w_ref[...])
            c_ref[0] = _rms(p[:, 0:256], lat_g_ref[...]).astype(BF16)
            ik = p[:, 256:384]
            mu = jnp.mean(ik, axis=-1, keepdims=True)
            xc = ik - mu
            var = jnp.mean(xc * xc, axis=-1, keepdims=True)
            k_ref[0] = (xc * lax.rsqrt(var + EPS) * ik_w_ref[...] + ik_b_ref[...]).astype(BF16)
            misc_ref[0] = p[:, 384:512]

    d = x.shape[1]
    const = lambda b, i: (0, 0)
    return pl.pallas_call(
        kernel,
        grid=grid,
        in_specs=[pl.BlockSpec((tm, d), x_map),
                  pl.BlockSpec((1, d), const),
                  pl.BlockSpec((d, SMALL_COLS), const),
                  pl.BlockSpec((1, DSA_LATENT), const),
                  pl.BlockSpec((1, IDX_DIM), const),
                  pl.BlockSpec((1, IDX_DIM), const)],
        out_specs=[pl.BlockSpec((1, tm, DSA_LATENT), lambda b, i: (b, i, 0)),
                   pl.BlockSpec((1, tm, IDX_DIM), lambda b, i: (b, i, 0)),
                   pl.BlockSpec((1, tm, 128), lambda b, i: (b, i, 0))],
        out_shape=[jax.ShapeDtypeStruct((batch, npad, DSA_LATENT), BF16),
                   jax.ShapeDtypeStruct((batch, npad, IDX_DIM), BF16),
                   jax.ShapeDtypeStruct((batch, npad, 128), F32)],
        compiler_params=_cparams(("parallel", "arbitrary")),
        name="small_proj",
    )(x, gain, w_small, lat_g, ik_w, ik_b)


def _log_sigmoid(z):
    return jnp.minimum(z, 0.0) - jnp.log1p(jnp.exp(-jnp.abs(z)))


def _split3(x):
    h = x.astype(BF16)
    r = x - h.astype(F32)
    m = r.astype(BF16)
    l = (r - m.astype(F32)).astype(BF16)
    return h, m, l


def _gla_kernel(nb, q_ref, k_ref, v_ref, r_ref, misc_ref, wa2_ref, ba_ref, on_ref, o_ref, state_ref):
    @pl.when(pl.program_id(0) == 0)
    def _():
        state_ref[...] = jnp.zeros_like(state_ref)

    rows = nb * GLA_SUB * CHUNK
    stack = lambda ref, cols: jnp.concatenate([ref[i][:, cols] for i in range(nb)], axis=0)
    row = lax.broadcasted_iota(I32, (rows, rows), 0)
    col = lax.broadcasted_iota(I32, (rows, rows), 1)
    same = (row // CHUNK) == (col // CHUNK)
    lower = jnp.logical_and(same, col <= row)
    upper = jnp.logical_and(same, col > row)
    tril = jnp.where(lower, 1.0, 0.0).astype(BF16)

    a_low = stack(misc_ref, slice(0, GLA_RANK)).astype(BF16)
    z = _dot(a_low, wa2_ref[...]) + ba_ref[...]
    la = _log_sigmoid(z) * (1.0 / GLA_TAU)
    l_h, l_m, l_l = _split3(la)
    b_all = _dot(tril, l_h) + _dot(tril, l_m) + _dot(tril, l_l)

    for h in range(GLA_HEADS):
        ks = slice(h * GLA_DK, (h + 1) * GLA_DK)
        vs = slice(h * GLA_DV, (h + 1) * GLA_DV)
        b = b_all[:, ks]
        eb = jnp.exp(b)
        ebi = jnp.exp(-b)
        q = stack(q_ref, ks).astype(F32) * (GLA_DK ** -0.5)
        k = stack(k_ref, ks).astype(F32)
        v = stack(v_ref, vs)
        q_fwd = (q * eb).astype(BF16)
        a_lo = _dot_nt(q_fwd, (k * ebi).astype(BF16))
        a_up = _dot_nt((q * ebi).astype(BF16), (k * eb).astype(BF16))
        scores = jnp.where(lower, a_lo, jnp.where(upper, a_up, 0.0)).astype(BF16)
        o_intra = _dot(scores, v)
        outs = []
        for i in range(nb):
            st = state_ref[i * GLA_HEADS + h]
            for sub in range(GLA_SUB):
                rs = slice((i * GLA_SUB + sub) * CHUNK, (i * GLA_SUB + sub + 1) * CHUNK)
                b_i = b[rs]
                b_last = b_i[CHUNK - 1:CHUNK, :]
                outs.append(o_intra[rs] + _dot_nt(q_fwd[rs], st.astype(BF16)))
                k_dec = (k[rs] * jnp.exp(b_last - b_i)).astype(BF16)
                v_t = v[rs].astype(F32).T.astype(BF16)
                st = st * jnp.exp(b_last) + _dot(v_t, k_dec)
            state_ref[i * GLA_HEADS + h] = st
        o = _rms(jnp.concatenate(outs, axis=0), on_ref[...])
        r = stack(r_ref, vs).astype(F32)
        y = (o * (r * jax.nn.sigmoid(r))).astype(o_ref.dtype)
        blk = GLA_SUB * CHUNK
        for i in range(nb):
            o_ref[i, :, vs] = y[i * blk:(i + 1) * blk]


def _gla(proj, misc, w_a2, b_a, out_norm, batch, seq):
    blk = GLA_SUB * CHUNK
    assert seq % blk == 0 and PAD_FRONT % blk == 0
    proj3 = proj.reshape(batch, seq, proj.shape[1])
    y = pl.pallas_call(
        functools.partial(_gla_kernel, batch),
        grid=(seq // blk,),
        in_specs=[pl.BlockSpec((batch, blk, 1024), lambda c: (0, c, 0)),
                  pl.BlockSpec((batch, blk, 1024), lambda c: (0, c, 1)),
                  pl.BlockSpec((batch, blk, 2048), lambda c: (0, c, 1)),
                  pl.BlockSpec((batch, blk, 2048), lambda c: (0, c, 2)),
                  pl.BlockSpec((batch, blk, 128), lambda c: (0, c + PAD_FRONT // blk, 0)),
                  pl.BlockSpec((GLA_RANK, GLA_HEADS * GLA_DK), lambda c: (0, 0)),
                  pl.BlockSpec((1, GLA_HEADS * GLA_DK), lambda c: (0, 0)),
                  pl.BlockSpec((1, GLA_DV), lambda c: (0, 0))],
        out_specs=pl.BlockSpec((batch, blk, GLA_HEADS * GLA_DV), lambda c: (0, c, 0)),
        out_shape=jax.ShapeDtypeStruct((batch, seq, GLA_HEADS * GLA_DV), BF16),
        scratch_shapes=[pltpu.VMEM((batch * GLA_HEADS, GLA_DV, GLA_DK), F32)],
        compiler_params=_cparams(("arbitrary",)),
        name="gla",
    )(proj3, proj3, proj3, proj3, misc, w_a2, b_a, out_norm)
    return y.reshape(batch * seq, GLA_HEADS * GLA_DV)


def _t5_bucket(rel):
    half = REL_BUCKETS // 2
    max_exact = half // 2
    ret = jnp.where(rel > 0, half, 0)
    n = jnp.abs(rel)
    nf = jnp.maximum(n, 1).astype(jnp.float32)
    large = max_exact + (jnp.log(nf / max_exact) / math.log(REL_MAX_DIST / max_exact)
                         * (half - max_exact)).astype(jnp.int32)
    large = jnp.minimum(large, half - 1)
    return ret + jnp.where(n < max_exact, n, large)


def _bias_band_kernel(bucket_ref, rb_ref, o_ref):
    far = REL_BUCKETS // 2 - 1
    bucket = bucket_ref[...]
    for h in range(DSA_HEADS):
        acc = jnp.zeros(bucket.shape, F32)
        for b in range(REL_BUCKETS):
            acc = jnp.where(bucket == b, rb_ref[b, h], acc)
        o_ref[h] = (acc - rb_ref[far, h]) * LOG2E


def _bias_band(rel_bias):
    t = jnp.arange(QBLK, dtype=jnp.int32)[:, None]
    j = jnp.arange(2 * QBLK, dtype=jnp.int32)[None, :]
    bucket = _t5_bucket(j - QBLK - t).astype(jnp.int32)
    return pl.pallas_call(
        _bias_band_kernel,
        in_specs=[pl.BlockSpec(memory_space=pltpu.VMEM), pl.BlockSpec(memory_space=pltpu.SMEM)],
        out_specs=pl.BlockSpec(memory_space=pltpu.VMEM),
        out_shape=jax.ShapeDtypeStruct((DSA_HEADS, QBLK, 2 * QBLK), F32),
        name="bias_band",
    )(bucket, rel_bias)


_SWAP_MASK = {16: 0x0000FFFF, 8: 0x00FF00FF, 4: 0x0F0F0F0F, 2: 0x33333333, 1: 0x55555555}


def _transpose_stages(words, stages):
    a = list(words)
    for j in stages:
        for k in range(len(a)):
            if k & j == 0:
                t = (a[k] ^ lax.shift_right_logical(a[k + j], jnp.int32(j))) & jnp.int32(_SWAP_MASK[j])
                a[k], a[k + j] = a[k] ^ t, a[k + j] ^ (t << j)
    return a


def _sortable(x):
    i = pltpu.bitcast(x, I32)
    return jnp.where(i < 0, i ^ jnp.int32(0x7FFFFFFF), i)


def _dsa_kernel(top_k, dq_ref, iq_ref, misc_ref, kidx_ref, clat_ref, wuk_ref, wuv_ref, band_ref,
                o_ref, keys_ref, planes_ref, eq_ref, iqs_ref, wb_ref, qlat_ref, madd_ref, s_ref, s2_ref, p_ref, alpha_ref,
                m_ref, l_ref, acc_ref):
    qb = pl.program_id(1)
    start = qb * QBLK
    hrows = lambda h: slice(h * QBLK, (h + 1) * QBLK)

    w_scale = IDX_HEADS ** -0.5 * IDX_DIM ** -0.5
    wq = misc_ref[0][:, GLA_RANK:GLA_RANK + IDX_HEADS] * w_scale
    for h in range(IDX_HEADS):
        wb_ref[hrows(h), :] = jnp.broadcast_to(wq[:, h:h + 1], (QBLK, 128))
        iqs_ref[hrows(h), :] = iq_ref[:, h * IDX_DIM:(h + 1) * IDX_DIM]

    row = lax.broadcasted_iota(I32, (QBLK, KCH), 0)
    lane = lax.broadcasted_iota(I32, (QBLK, KCH), 1)
    p_lim = start + (row // CHUNK + 1) * CHUNK + PAD_FRONT

    n_chunks = (start + PAD_FRONT + QBLK + KCH - 1) // KCH
    tiles_per_chunk = KCH // 128
    chunks_per_group = GROUP_KEYS // KCH

    def stage_planes(c, key):
        tiles = [key[:, i * 128:(i + 1) * 128] for i in range(tiles_per_chunk)]
        tiles = _transpose_stages(tiles, (2, 1))
        g = c // chunks_per_group
        w0 = (c % chunks_per_group) * tiles_per_chunk
        for i in range(tiles_per_chunk):
            planes_ref[w0 + i, :, pl.ds(pl.multiple_of(g * 128, 128), 128)] = tiles[i]

    no_key = jnp.full((QBLK, KCH), INT_MIN, I32)
    keys_ref[:, 0:KCH] = no_key
    stage_planes(jnp.int32(0), no_key)

    def index_chunk(c, dots_ref):
        off = pl.multiple_of(c * KCH, KCH)
        kc = kidx_ref[0, pl.ds(off, KCH), :]
        dots_ref[...] = _dot_nt(iqs_ref[...], kc)
        acc = jnp.zeros((QBLK, KCH), F32)
        for h in range(IDX_HEADS):
            wbh = wb_ref[hrows(h), :]
            acc = acc + jnp.concatenate([wbh] * (KCH // 128), axis=1) * jnp.maximum(dots_ref[hrows(h), :], 0.0)
        key = jnp.where(lane + off < p_lim, _sortable(acc), INT_MIN)
        keys_ref[:, pl.ds(off, KCH)] = key
        stage_planes(c, key)

    n_odd = (n_chunks - 1) % 2

    @pl.when(n_odd == 1)
    def _():
        index_chunk(1, s_ref)

    def idx_body(i, carry):
        c = 1 + n_odd + 2 * i
        index_chunk(c, s_ref)
        index_chunk(c + 1, s2_ref)
        return carry

    lax.fori_loop(0, (n_chunks - 1) // 2, idx_body, 0)

    n_groups = (n_chunks + chunks_per_group - 1) // chunks_per_group
    ngrp_max = eq_ref.shape[1] // 128

    def pad_body(c, carry):
        stage_planes(c, no_key)
        return carry

    lax.fori_loop(n_chunks, n_groups * chunks_per_group, pad_body, 0)

    def plane_body(idx, carry):
        g = idx // (QBLK // 16)
        gl = pl.ds(pl.multiple_of(g * 128, 128), 128)
        for half in range(2):
            r0 = pl.multiple_of((idx % (QBLK // 16)) * 16 + half * 8, 8)
            words = [planes_ref[j, pl.ds(r0, 8), gl] for j in range(32)]
            words = _transpose_stages(words, (16, 8, 4))
            words[0] = ~words[0]
            for i in range(32):
                planes_ref[31 - i, pl.ds(r0, 8), gl] = words[i]
        return carry

    lax.fori_loop(0, n_groups * (QBLK // 16), plane_body, 0)

    for g in range(ngrp_max):
        eq_ref[:, g * 128:(g + 1) * 128] = jnp.broadcast_to(jnp.where(g < n_groups, -1, 0), (QBLK, 128))

    def row_count(t):
        pc = lax.population_count(t)
        tot = pc[:, 0:128]
        for g in range(1, ngrp_max):
            tot = tot + pc[:, g * 128:(g + 1) * 128]
        return jnp.broadcast_to(jnp.sum(tot.astype(F32), axis=1, keepdims=True), (QBLK, 128))

    def pair_body(i, carry):
        prefix, above = carry
        b0 = 30 - 2 * i
        p1 = planes_ref[b0 + 1]
        p0 = planes_ref[b0]
        eq = eq_ref[...]
        e1 = eq & p1
        e0 = eq & ~p1
        t11 = e1 & p0
        t10 = e1 & ~p0
        t01 = e0 & p0
        t00 = e0 & ~p0
        s3 = above + row_count(t11)
        s2 = s3 + row_count(t10)
        s1 = s2 + row_count(t01)
        is3 = s3 >= top_k
        is2 = s2 >= top_k
        is1 = s1 >= top_k
        for g in range(ngrp_max):
            gs = slice(g * 128, (g + 1) * 128)
            eq_ref[:, gs] = jnp.where(is3, t11[:, gs], jnp.where(is2, t10[:, gs], jnp.where(is1, t01[:, gs], t00[:, gs])))
        above = jnp.where(is3, above, jnp.where(is2, s3, jnp.where(is1, s2, s1)))
        digit = jnp.where(is3, 3, jnp.where(is2, 2, jnp.where(is1, 1, 0)))
        return prefix | (digit << b0), above

    end = start + PAD_FRONT + QBLK
    n_att = (start + QBLK + KCH - 1) // KCH

    def chunk_off(j):
        return pl.multiple_of(end - KCH * (j + 1), 128)

    def absorbed_query(h):
        ql = _dot(dq_ref[:, h * DSA_DH:(h + 1) * DSA_DH], wuk_ref[h]) * (LOG2E * DSA_DH ** -0.5)
        qlat_ref[hrows(h), :] = ql.astype(BF16)

    def first_scores(h):
        s_ref[hrows(h), :] = _dot_nt(qlat_ref[hrows(h), :], clat_ref[0, pl.ds(chunk_off(0), KCH), :])

    carry = (jnp.zeros((QBLK, 128), I32), jnp.zeros((QBLK, 128), F32))
    for i in range(16):
        carry = pair_body(i, carry)
        for h in range(i * DSA_HEADS // 16, (i + 1) * DSA_HEADS // 16):
            absorbed_query(h)
            if h > 0:
                first_scores(h - 1)
    first_scores(DSA_HEADS - 1)
    prefix, above = carry
    thr = prefix ^ INT_MIN
    thr = jnp.maximum(thr, INT_MIN + 1)

    surplus = jnp.where(prefix != 0, above + row_count(eq_ref[...]) - top_k, 0.0)

    def reset_softmax():
        m_ref[...] = jnp.full(m_ref.shape, NEG, F32)
        l_ref[...] = jnp.zeros(l_ref.shape, F32)
        acc_ref[...] = jnp.zeros(acc_ref.shape, F32)

    def demote_surplus_ties():
        keep = top_k - above
        lane1 = lax.broadcasted_iota(I32, (QBLK, 128), 1)

        def tied_before(q):
            def body(c, cnt):
                off = pl.multiple_of(c * KCH, KCH)
                kk = keys_ref[:, pl.ds(off, KCH)]
                for s in range(KCH // 128):
                    hit = jnp.logical_and(kk[:, s * 128:(s + 1) * 128] == thr, lane1 + (off + s * 128) < q)
                    cnt = cnt + jnp.where(hit, 1, 0)
                return cnt
            cnt = lax.fori_loop(1, n_chunks, body, jnp.zeros((QBLK, 128), I32))
            return jnp.broadcast_to(jnp.sum(cnt.astype(F32), axis=1, keepdims=True), (QBLK, 128))

        def pos_body(i, q):
            cand = q | (jnp.int32(1) << (POS_BITS - 1 - i))
            return jnp.where(tied_before(cand) < keep, cand, q)

        last = lax.fori_loop(0, POS_BITS, pos_body, jnp.zeros((QBLK, 128), I32))
        last = jnp.where(surplus > 0.0, last, jnp.int32(2 ** POS_BITS))

        def demote_body(c, carry):
            off = pl.multiple_of(c * KCH, KCH)
            for s in range(KCH // 128):
                cs = pl.ds(off + s * 128, 128)
                kk = keys_ref[:, cs]
                drop = jnp.logical_and(kk == thr, lane1 + (off + s * 128) > last)
                keys_ref[:, cs] = jnp.where(drop, INT_MIN, kk)
            return carry

        lax.fori_loop(1, n_chunks, demote_body, 0)

    reset_softmax()

    def scores(j, dst_ref):
        cc = clat_ref[0, pl.ds(chunk_off(j), KCH), :]
        dst_ref[...] = _dot_nt(qlat_ref[...], cc)

    def softmax_update(j, src_ref, near):
        off = chunk_off(j)
        kk = keys_ref[:, pl.ds(off, KCH)]
        madd_ref[...] = jnp.where(kk >= jnp.concatenate([thr] * (KCH // 128), axis=1), 0.0, NEG)
        cc = clat_ref[0, pl.ds(off, KCH), :]
        for h in range(DSA_HEADS):
            s = src_ref[hrows(h), :] + madd_ref[...]
            if near:
                s = jnp.concatenate([s[:, :KCH - 2 * QBLK], s[:, KCH - 2 * QBLK:] + band_ref[h]], axis=1)
            m_old = m_ref[hrows(h), :]
            m_new = jnp.maximum(m_old, jnp.broadcast_to(jnp.max(s, axis=1, keepdims=True), (QBLK, 128)))
            alpha = jnp.exp2(m_old - m_new)
            p = jnp.exp2(s - jnp.concatenate([m_new] * (KCH // 128), axis=1))
            l_ref[hrows(h), :] = (alpha * l_ref[hrows(h), :]
                                  + jnp.broadcast_to(jnp.sum(p, axis=1, keepdims=True), (QBLK, 128)))
            m_ref[hrows(h), :] = m_new
            alpha_ref[hrows(h), :] = alpha
            p_ref[hrows(h), :] = p.astype(BF16)
        al = alpha_ref[...]
        acc_ref[...] = (jnp.concatenate([al] * (DSA_LATENT // 128), axis=1) * acc_ref[...]
                        + _dot(p_ref[...], cc))

    def step(j, cur_ref, nxt_ref, near):
        scores(jnp.minimum(j + 1, n_att - 1), nxt_ref)
        softmax_update(j, cur_ref, near)

    step(0, s_ref, s2_ref, True)

    @pl.when(jnp.max(surplus) > 0.0)
    def _():
        demote_surplus_ties()
        reset_softmax()
        softmax_update(0, s_ref, True)

    def att_body(i, carry):
        j = 2 * i + 1
        step(j, s2_ref, s_ref, False)
        step(j + 1, s_ref, s2_ref, False)
        return carry

    lax.fori_loop(0, (n_att - 1) // 2, att_body, 0)

    @pl.when((n_att - 1) % 2 == 1)
    def _():
        step(n_att - 1, s2_ref, s_ref, False)

    for h in range(DSA_HEADS):
        inv = 1.0 / l_ref[hrows(h), :]
        o_lat = acc_ref[hrows(h), :] * jnp.concatenate([inv] * (DSA_LATENT // 128), axis=1)
        o_ref[:, h * DSA_DV:(h + 1) * DSA_DV] = _dot(o_lat.astype(BF16), wuv_ref[h]).astype(o_ref.dtype)


def _dsa(proj, misc, kidx, clat, w_uk, w_uv, band, batch, seq):
    nqb = seq // QBLK
    t = batch * seq
    npad = clat.shape[1]
    top_k = min(INDEX_TOPK, seq // 4)
    assert npad < 2 ** POS_BITS
    hq = DSA_HEADS * QBLK
    tokb = lambda b, i: b * nqb + i
    const3 = lambda b, i: (0, 0, 0)
    ngrp = (npad + GROUP_KEYS - 1) // GROUP_KEYS
    once = pl.Buffered(1)
    return pl.pallas_call(
        functools.partial(_dsa_kernel, top_k),
        grid=(batch, nqb),
        in_specs=[pl.BlockSpec((QBLK, 2048), lambda b, i: (tokb(b, i), 3)),
                  pl.BlockSpec((QBLK, 2048), lambda b, i: (tokb(b, i), 4)),
                  pl.BlockSpec((1, QBLK, 128), lambda b, i: (b, i + PAD_FRONT // QBLK, 0)),
                  pl.BlockSpec((1, npad, IDX_DIM), lambda b, i: (b, 0, 0), pipeline_mode=once),
                  pl.BlockSpec((1, npad, DSA_LATENT), lambda b, i: (b, 0, 0), pipeline_mode=once),
                  pl.BlockSpec((DSA_HEADS, DSA_DH, DSA_LATENT), const3, pipeline_mode=once),
                  pl.BlockSpec((DSA_HEADS, DSA_LATENT, DSA_DV), const3, pipeline_mode=once),
                  pl.BlockSpec((DSA_HEADS, QBLK, 2 * QBLK), const3, pipeline_mode=once)],
        out_specs=pl.BlockSpec((QBLK, DSA_HEADS * DSA_DV), lambda b, i: (tokb(b, i), 0)),
        out_shape=jax.ShapeDtypeStruct((t, DSA_HEADS * DSA_DV), BF16),
        scratch_shapes=[pltpu.VMEM((QBLK, npad), I32),
                        pltpu.VMEM((32, QBLK, ngrp * 128), I32),
                        pltpu.VMEM((QBLK, ngrp * 128), I32),
                        pltpu.VMEM((hq, IDX_DIM), BF16),
                        pltpu.VMEM((hq, 128), F32),
                        pltpu.VMEM((hq, DSA_LATENT), BF16),
                        pltpu.VMEM((QBLK, KCH), F32),
                        pltpu.VMEM((hq, KCH), F32),
                        pltpu.VMEM((hq, KCH), F32),
                        pltpu.VMEM((hq, KCH), BF16),
                        pltpu.VMEM((hq, 128), F32),
                        pltpu.VMEM((hq, 128), F32),
                        pltpu.VMEM((hq, 128), F32),
                        pltpu.VMEM((hq, DSA_LATENT), F32)],
        compiler_params=_cparams(("parallel", "arbitrary")),
        name="dsa",
    )(proj, proj, misc, kidx, clat, w_uk, w_uv, band)


def _mix_kernel(yg_ref, yd_ref, gg_ref, gd_ref, wg_ref, wd_ref, wm_ref, g_ref, h_ref, o_ref):
    a = _dot(yg_ref[...], wg_ref[...])
    b = _dot(yd_ref[...], wd_ref[...])
    gg = jax.nn.sigmoid(gg_ref[...].astype(F32))
    gd = jax.nn.sigmoid(gd_ref[...].astype(F32))
    merged = (gg * a + gd * b).astype(BF16)
    o_ref[...] = h_ref[...] + _rms(_dot(merged, wm_ref[...]), g_ref[...])


def _mix(y_gla, y_dsa, proj, w_g, w_d, w_m, gain, h, tm):
    t, d = y_gla.shape
    once = pl.Buffered(1)
    row = lambda i: (i, 0)
    fixed = lambda i: (0, 0)
    return pl.pallas_call(
        _mix_kernel,
        grid=(t // tm,),
        in_specs=[pl.BlockSpec((tm, d), row),
                  pl.BlockSpec((tm, d), row),
                  pl.BlockSpec((tm, d), lambda i: (i, 5)),
                  pl.BlockSpec((tm, d), lambda i: (i, 6)),
                  pl.BlockSpec((d, d), fixed, pipeline_mode=once),
                  pl.BlockSpec((d, d), fixed, pipeline_mode=once),
                  pl.BlockSpec((d, d), fixed, pipeline_mode=once),
                  pl.BlockSpec((1, d), fixed),
                  pl.BlockSpec((tm, d), row)],
        out_specs=pl.BlockSpec((tm, d), row),
        out_shape=jax.ShapeDtypeStruct((t, d), F32),
        compiler_params=_cparams(("parallel",)),
        name="mix",
    )(y_gla, y_dsa, proj, proj, w_g, w_d, w_m, gain, h)


def _xa_kernel(h_ref, g_ref, wq_ref, k_ref, v_ref, wo_ref, pg_ref, o_ref):
    h = h_ref[...]
    q = _dot(_rms(h, g_ref[...]).astype(BF16), wq_ref[...]).astype(BF16)
    outs = []
    for hd in range(XA_HEADS):
        hs = slice(hd * XA_DH, (hd + 1) * XA_DH)
        s = _dot_nt(q[:, hs], k_ref[0][:, hs]) * (XA_DH ** -0.5)
        m = jnp.max(s, axis=-1, keepdims=True)
        p = jnp.exp(s - m)
        p = p / jnp.sum(p, axis=-1, keepdims=True)
        outs.append(_dot(p.astype(BF16), v_ref[0][:, hs]).astype(BF16))
    y = _dot(jnp.concatenate(outs, axis=1), wo_ref[...])
    o_ref[...] = h + _rms(y, pg_ref[...])


def _xa(h, pre_gain, w_q, kv, w_o, post_gain, batch, seq, tm):
    t, d = h.shape
    n_mem = kv.shape[1]
    nb = seq // tm
    once = pl.Buffered(1)
    row = lambda b, i: (b * nb + i, 0)
    fixed = lambda b, i: (0, 0)
    return pl.pallas_call(
        _xa_kernel,
        grid=(batch, nb),
        in_specs=[pl.BlockSpec((tm, d), row),
                  pl.BlockSpec((1, d), fixed),
                  pl.BlockSpec((d, d), fixed, pipeline_mode=once),
                  pl.BlockSpec((1, n_mem, d), lambda b, i: (b, 0, 0)),
                  pl.BlockSpec((1, n_mem, d), lambda b, i: (b, 0, 1)),
                  pl.BlockSpec((d, d), fixed, pipeline_mode=once),
                  pl.BlockSpec((1, d), fixed)],
        out_specs=pl.BlockSpec((tm, d), row),
        out_shape=jax.ShapeDtypeStruct((t, d), F32),
        compiler_params=_cparams(("parallel", "parallel")),
        name="xa",
    )(h, pre_gain, w_q, kv, kv, w_o, post_gain)


def _ffn_kernel(h_ref, g_ref, wg_ref, wu_ref, wd_ref, pg_ref, o_ref, u_ref):
    f = pl.program_id(1)

    @pl.when(f == 0)
    def _():
        u_ref[...] = _rms(h_ref[...], g_ref[...]).astype(BF16)
        o_ref[...] = jnp.zeros_like(o_ref)

    u = u_ref[...]
    n_sub = max(wg_ref.shape[1] // 256, 1)
    half = wg_ref.shape[1] // n_sub
    down = None
    for s in range(n_sub):
        cs = slice(s * half, (s + 1) * half)
        a = _dot(u, wg_ref[:, cs])
        b = _dot(u, wu_ref[:, cs])
        act = (a * jax.nn.sigmoid(a) * b).astype(BF16)
        d = _dot(act, wd_ref[cs, :])
        down = d if down is None else down + d
    o_ref[...] += down

    @pl.when(f == pl.num_programs(1) - 1)
    def _():
        o_ref[...] = h_ref[...] + _rms(o_ref[...], pg_ref[...])


def _ffn(h, pre_gain, w_gate, w_up, w_down, post_gain, tm, tf):
    t, d = h.shape
    ff = w_gate.shape[1]
    return pl.pallas_call(
        _ffn_kernel,
        grid=(t // tm, ff // tf),
        in_specs=[pl.BlockSpec((tm, d), lambda i, f: (i, 0)),
                  pl.BlockSpec((1, d), lambda i, f: (0, 0)),
                  pl.BlockSpec((d, tf), lambda i, f: (0, f)),
                  pl.BlockSpec((d, tf), lambda i, f: (0, f)),
                  pl.BlockSpec((tf, d), lambda i, f: (f, 0)),
                  pl.BlockSpec((1, d), lambda i, f: (0, 0))],
        out_specs=pl.BlockSpec((tm, d), lambda i, f: (i, 0)),
        out_shape=jax.ShapeDtypeStruct((t, d), F32),
        scratch_shapes=[pltpu.VMEM((tm, d), BF16)],
        compiler_params=_cparams(("parallel", "arbitrary")),
        name="ffn",
    )(h, pre_gain, w_gate, w_up, w_down, post_gain)


def _row(v):
    return v.reshape(1, -1).astype(F32)


def _layer(h, mem, w_in, gla_w_a2, gla_b_a, gla_out_norm, dsa_w_uk, dsa_w_uv, dsa_latent_norm,
           idx_k_norm_w, idx_k_norm_b, band, w_gla_branch, w_dsa_branch, w_mix_out,
           mix_pre_norm, mix_post_norm, xa_pre_norm, xa_post_norm, xa_mem_norm,
           w_xa_q, w_xa_kv, w_xa_o, ffn_pre_norm, ffn_post_norm, w_ffn_gate, w_ffn_up, w_ffn_down,
           batch, seq):
    d = D_MODEL
    t = batch * seq
    cols = lambda a, b: w_in[:, a:b]
    w_main = jnp.concatenate(
        [cols(_O_GQ, _O_GA), cols(_O_DQ, _O_DC), cols(_O_IQ, _O_IK), cols(_O_GG, _O_END)], axis=1).astype(BF16)
    w_small = jnp.concatenate(
        [cols(_O_DC, _O_IQ), cols(_O_IK, _O_IW), cols(_O_GA, _O_DQ), cols(_O_IW, _O_GG),
         jnp.zeros((d, SMALL_COLS - 416), w_in.dtype)], axis=1).astype(BF16)

    pre = _row(mix_pre_norm)
    proj = _norm_matmul(h, pre, w_main, BF16, 1024, 2048)
    clat, kidx, misc = _small_proj(h, pre, w_small, _row(dsa_latent_norm), _row(idx_k_norm_w),
                                   _row(idx_k_norm_b), batch, seq)

    y_gla = _gla(proj, misc, gla_w_a2.astype(BF16), _row(gla_b_a), _row(gla_out_norm), batch, seq)
    y_dsa = _dsa(proj, misc, kidx, clat, dsa_w_uk.astype(BF16), dsa_w_uv.astype(BF16), band, batch, seq)

    h = _mix(y_gla, y_dsa, proj, w_gla_branch.astype(BF16), w_dsa_branch.astype(BF16),
             w_mix_out.astype(BF16), _row(mix_post_norm), h, 256)

    n_mem = mem.shape[1]
    kv = _norm_matmul(mem.reshape(batch * n_mem, d), _row(xa_mem_norm), w_xa_kv.astype(BF16), BF16,
                      batch * n_mem, 512).reshape(batch, n_mem, 2 * d)
    h = _xa(h, _row(xa_pre_norm), w_xa_q.astype(BF16), kv, w_xa_o.astype(BF16), _row(xa_post_norm),
            batch, seq, 512)

    h = _ffn(h, _row(ffn_pre_norm), w_ffn_gate.astype(BF16), w_ffn_up.astype(BF16),
             w_ffn_down.astype(BF16), _row(ffn_post_norm), 1024, 256)
    return h


def kernel(x, mem, w_in, gla_w_a2, gla_b_a, gla_out_norm, dsa_w_uk, dsa_w_uv, dsa_latent_norm,
           idx_k_norm_w, idx_k_norm_b, rel_bias, w_gla_branch, w_dsa_branch, w_mix_out,
           mix_pre_norm, mix_post_norm, xa_pre_norm, xa_post_norm, xa_mem_norm,
           w_xa_q, w_xa_kv, w_xa_o, ffn_pre_norm, ffn_post_norm, w_ffn_gate, w_ffn_up, w_ffn_down):
    batch, seq, d = x.shape
    depth = w_in.shape[0]
    band = _bias_band(rel_bias.astype(F32))
    h = x.reshape(batch * seq, d)
    for l in range(depth):
        h = _layer(h, mem, w_in[l], gla_w_a2[l], gla_b_a[l], gla_out_norm[l], dsa_w_uk[l], dsa_w_uv[l],
                   dsa_latent_norm[l], idx_k_norm_w[l], idx_k_norm_b[l], band, w_gla_branch[l],
                   w_dsa_branch[l], w_mix_out[l], mix_pre_norm[l], mix_post_norm[l], xa_pre_norm[l],
                   xa_post_norm[l], xa_mem_norm[l], w_xa_q[l], w_xa_kv[l], w_xa_o[l], ffn_pre_norm[l],
                   ffn_post_norm[l], w_ffn_gate[l], w_ffn_up[l], w_ffn_down[l], batch, seq)
    return h.reshape(batch, seq, d)
```

```python
import functools
import math

import jax
import jax.numpy as jnp
import numpy as np
from jax import lax
from jax.experimental import pallas as pl
from jax.experimental.pallas import tpu as pltpu

F32 = jnp.float32
BF16 = jnp.bfloat16
I32 = jnp.int32

D_MODEL = 2048
CHUNK = 64
EPS = 1e-6

GLA_HEADS = 4
GLA_DK = 256
GLA_DV = 512
GLA_RANK = 16
GLA_TAU = 16.0
GLA_SUB = 1

DSA_HEADS = 16
DSA_DH = 128
DSA_DV = 128
DSA_LATENT = 256
IDX_HEADS = 16
IDX_DIM = 128
INDEX_TOPK = 256

REL_BUCKETS = 32
REL_MAX_DIST = 128

XA_HEADS = 4
XA_DH = 512

_SPLITS = (1024, 1024, 2048, 2048, 16, 2048, 256, 2048, 128, 16, 2048, 2048)
_OFFS = tuple(int(v) for v in np.cumsum((0,) + _SPLITS))
(_O_GQ, _O_GK, _O_GV, _O_GR, _O_GA, _O_DQ, _O_DC, _O_IQ, _O_IK, _O_IW, _O_GG, _O_GD, _O_END) = _OFFS

MAIN_COLS = 14336
SMALL_COLS = 512

QBLK = 128
KCH = 512
GROUP_KEYS = 32 * 128
POS_BITS = 15
PAD_FRONT = KCH
INT_MIN = -2 ** 31
NEG = -1e30
LOG2E = 1.4426950408889634

VMEM_LIMIT = 58 * 1024 * 1024


def _cparams(sem):
    return pltpu.CompilerParams(dimension_semantics=sem, vmem_limit_bytes=VMEM_LIMIT)


def _rms(x, gain):
    ms = jnp.mean(x * x, axis=-1, keepdims=True)
    return x * lax.rsqrt(ms + EPS) * gain


def _dot(a, b):
    return jnp.dot(a, b, preferred_element_type=F32)


def _dot_nt(a, b):
    return lax.dot_general(a, b, (((1,), (1,)), ((), ())), preferred_element_type=F32)


def _norm_matmul_kernel(x_ref, g_ref, w_ref, o_ref, u_ref):
    @pl.when(pl.program_id(1) == 0)
    def _():
        u_ref[...] = _rms(x_ref[...], g_ref[...]).astype(BF16)

    o_ref[...] = _dot(u_ref[...], w_ref[...]).astype(o_ref.dtype)


def _norm_matmul(x, gain, w, out_dtype, tm, tn):
    t, d = x.shape
    n = w.shape[1]
    return pl.pallas_call(
        _norm_matmul_kernel,
        grid=(t // tm, n // tn),
        in_specs=[pl.BlockSpec((tm, d), lambda i, j: (i, 0)),
                  pl.BlockSpec((1, d), lambda i, j: (0, 0)),
                  pl.BlockSpec((d, tn), lambda i, j: (0, j))],
        out_specs=pl.BlockSpec((tm, tn), lambda i, j: (i, j)),
        out_shape=jax.ShapeDtypeStruct((t, n), out_dtype),
        scratch_shapes=[pltpu.VMEM((tm, d), BF16)],
        compiler_params=_cparams(("parallel", "arbitrary")),
        name="norm_matmul",
    )(x, gain, w)


def _small_proj(x, gain, w_small, lat_g, ik_w, ik_b, batch, seq):
    tm = PAD_FRONT
    assert seq % tm == 0
    nblk = seq // tm
    npad = PAD_FRONT + seq
    grid = (batch, 1 + nblk)

    def x_map(b, i):
        return (b * nblk + jnp.maximum(i - 1, 0), 0)

    def kernel(x_ref, g_ref, w_ref, lat_g_ref, ik_w_ref, ik_b_ref, c_ref, k_ref, misc_ref):
        is_pad = pl.program_id(1) == 0

        @pl.when(is_pad)
        def _():
            c_ref[...] = jnp.zeros_like(c_ref)
            k_ref[...] = jnp.zeros_like(k_ref)
            misc_ref[...] = jnp.zeros_like(misc_ref)

        @pl.when(jnp.logical_not(is_pad))
        def _():
            u = _rms(x_ref[...], g_ref[...]).astype(BF16)
            p = _dot(u, w_ref[...])
            c_ref[0] = _rms(p[:, 0:256], lat_g_ref[...]).astype(BF16)
            ik = p[:, 256:384]
            mu = jnp.mean(ik, axis=-1, keepdims=True)
            xc = ik - mu
            var = jnp.mean(xc * xc, axis=-1, keepdims=True)
            k_ref[0] = (xc * lax.rsqrt(var + EPS) * ik_w_ref[...] + ik_b_ref[...]).astype(BF16)
            misc_ref[0] = p[:, 384:512]

    d = x.shape[1]
    const = lambda b, i: (0, 0)
    return pl.pallas_call(
        kernel,
        grid=grid,
        in_specs=[pl.BlockSpec((tm, d), x_map),
                  pl.BlockSpec((1, d), const),
                  pl.BlockSpec((d, SMALL_COLS), const),
                  pl.BlockSpec((1, DSA_LATENT), const),
                  pl.BlockSpec((1, IDX_DIM), const),
                  pl.BlockSpec((1, IDX_DIM), const)],
        out_specs=[pl.BlockSpec((1, tm, DSA_LATENT), lambda b, i: (b, i, 0)),
                   pl.BlockSpec((1, tm, IDX_DIM), lambda b, i: (b, i, 0)),
                   pl.BlockSpec((1, tm, 128), lambda b, i: (b, i, 0))],
        out_shape=[jax.ShapeDtypeStruct((batch, npad, DSA_LATENT), BF16),
                   jax.ShapeDtypeStruct((batch, npad, IDX_DIM), BF16),
                   jax.ShapeDtypeStruct((batch, npad, 128), F32)],
        compiler_params=_cparams(("parallel", "arbitrary")),
        name="small_proj",
    )(x, gain, w_small, lat_g, ik_w, ik_b)


def _log_sigmoid(z):
    return jnp.minimum(z, 0.0) - jnp.log1p(jnp.exp(-jnp.abs(z)))


def _split3(x):
    h = x.astype(BF16)
    r = x - h.astype(F32)
    m = r.astype(BF16)
    l = (r - m.astype(F32)).astype(BF16)
    return h, m, l


def _gla_kernel(nb, q_ref, k_ref, v_ref, r_ref, misc_ref, wa2_ref, ba_ref, on_ref, o_ref, state_ref):
    @pl.when(pl.program_id(0) == 0)
    def _():
        state_ref[...] = jnp.zeros_like(state_ref)

    rows = nb * GLA_SUB * CHUNK
    stack = lambda ref, cols: jnp.concatenate([ref[i][:, cols] for i in range(nb)], axis=0)
    row = lax.broadcasted_iota(I32, (rows, rows), 0)
    col = lax.broadcasted_iota(I32, (rows, rows), 1)
    same = (row // CHUNK) == (col // CHUNK)
    lower = jnp.logical_and(same, col <= row)
    upper = jnp.logical_and(same, col > row)
    tril = jnp.where(lower, 1.0, 0.0).astype(BF16)

    a_low = stack(misc_ref, slice(0, GLA_RANK)).astype(BF16)
    z = _dot(a_low, wa2_ref[...]) + ba_ref[...]
    la = _log_sigmoid(z) * (1.0 / GLA_TAU)
    l_h, l_m, l_l = _split3(la)
    b_all = _dot(tril, l_h) + _dot(tril, l_m) + _dot(tril, l_l)

    for h in range(GLA_HEADS):
        ks = slice(h * GLA_DK, (h + 1) * GLA_DK)
        vs = slice(h * GLA_DV, (h + 1) * GLA_DV)
        b = b_all[:, ks]
        eb = jnp.exp(b)
        ebi = jnp.exp(-b)
        q = stack(q_ref, ks).astype(F32) * (GLA_DK ** -0.5)
        k = stack(k_ref, ks).astype(F32)
        v = stack(v_ref, vs)
        q_fwd = (q * eb).astype(BF16)
        a_lo = _dot_nt(q_fwd, (k * ebi).astype(BF16))
        a_up = _dot_nt((q * ebi).astype(BF16), (k * eb).astype(BF16))
        scores = jnp.where(lower, a_lo, jnp.where(upper, a_up, 0.0)).astype(BF16)
        o_intra = _dot(scores, v)
        outs = []
        for i in range(nb):
            st = state_ref[i * GLA_HEADS + h]
            for sub in range(GLA_SUB):
                rs = slice((i * GLA_SUB + sub) * CHUNK, (i * GLA_SUB + sub + 1) * CHUNK)
                b_i = b[rs]
                b_last = b_i[CHUNK - 1:CHUNK, :]
                outs.append(o_intra[rs] + _dot_nt(q_fwd[rs], st.astype(BF16)))
                k_dec = (k[rs] * jnp.exp(b_last - b_i)).astype(BF16)
                v_t = v[rs].astype(F32).T.astype(BF16)
                st = st * jnp.exp(b_last) + _dot(v_t, k_dec)
            state_ref[i * GLA_HEADS + h] = st
        o = _rms(jnp.concatenate(outs, axis=0), on_ref[...])
        r = stack(r_ref, vs).astype(F32)
        y = (o * (r * jax.nn.sigmoid(r))).astype(o_ref.dtype)
        blk = GLA_SUB * CHUNK
        for i in range(nb):
            o_ref[i, :, vs] = y[i * blk:(i + 1) * blk]


def _gla(proj, misc, w_a2, b_a, out_norm, batch, seq):
    blk = GLA_SUB * CHUNK
    assert seq % blk == 0 and PAD_FRONT % blk == 0
    proj3 = proj.reshape(batch, seq, proj.shape[1])
    y = pl.pallas_call(
        functools.partial(_gla_kernel, batch),
        grid=(seq // blk,),
        in_specs=[pl.BlockSpec((batch, blk, 1024), lambda c: (0, c, 0)),
                  pl.BlockSpec((batch, blk, 1024), lambda c: (0, c, 1)),
                  pl.BlockSpec((batch, blk, 2048), lambda c: (0, c, 1)),
                  pl.BlockSpec((batch, blk, 2048), lambda c: (0, c, 2)),
                  pl.BlockSpec((batch, blk, 128), lambda c: (0, c + PAD_FRONT // blk, 0)),
                  pl.BlockSpec((GLA_RANK, GLA_HEADS * GLA_DK), lambda c: (0, 0)),
                  pl.BlockSpec((1, GLA_HEADS * GLA_DK), lambda c: (0, 0)),
                  pl.BlockSpec((1, GLA_DV), lambda c: (0, 0))],
        out_specs=pl.BlockSpec((batch, blk, GLA_HEADS * GLA_DV), lambda c: (0, c, 0)),
        out_shape=jax.ShapeDtypeStruct((batch, seq, GLA_HEADS * GLA_DV), BF16),
        scratch_shapes=[pltpu.VMEM((batch * GLA_HEADS, GLA_DV, GLA_DK), F32)],
        compiler_params=_cparams(("arbitrary",)),
        name="gla",
    )(proj3, proj3, proj3, proj3, misc, w_a2, b_a, out_norm)
    return y.reshape(batch * seq, GLA_HEADS * GLA_DV)


def _t5_bucket(rel):
    half = REL_BUCKETS // 2
    max_exact = half // 2
    ret = jnp.where(rel > 0, half, 0)
    n = jnp.abs(rel)
    nf = jnp.maximum(n, 1).astype(jnp.float32)
    large = max_exact + (jnp.log(nf / max_exact) / math.log(REL_MAX_DIST / max_exact)
                         * (half - max_exact)).astype(jnp.int32)
    large = jnp.minimum(large, half - 1)
    return ret + jnp.where(n < max_exact, n, large)


def _bias_band_kernel(bucket_ref, rb_ref, o_ref):
    far = REL_BUCKETS // 2 - 1
    bucket = bucket_ref[...]
    for h in range(DSA_HEADS):
        acc = jnp.zeros(bucket.shape, F32)
        for b in range(REL_BUCKETS):
            acc = jnp.where(bucket == b, rb_ref[b, h], acc)
        o_ref[h] = (acc - rb_ref[far, h]) * LOG2E


def _bias_band(rel_bias):
    t = jnp.arange(QBLK, dtype=jnp.int32)[:, None]
    j = jnp.arange(2 * QBLK, dtype=jnp.int32)[None, :]
    bucket = _t5_bucket(j - QBLK - t).astype(jnp.int32)
    return pl.pallas_call(
        _bias_band_kernel,
        in_specs=[pl.BlockSpec(memory_space=pltpu.VMEM), pl.BlockSpec(memory_space=pltpu.SMEM)],
        out_specs=pl.BlockSpec(memory_space=pltpu.VMEM),
        out_shape=jax.ShapeDtypeStruct((DSA_HEADS, QBLK, 2 * QBLK), F32),
        name="bias_band",
    )(bucket, rel_bias)


_SWAP_MASK = {16: 0x0000FFFF, 8: 0x00FF00FF, 4: 0x0F0F0F0F, 2: 0x33333333, 1: 0x55555555}


def _transpose_stages(words, stages):
    a = list(words)
    for j in stages:
        for k in range(len(a)):
            if k & j == 0:
                t = (a[k] ^ lax.shift_right_logical(a[k + j], jnp.int32(j))) & jnp.int32(_SWAP_MASK[j])
                a[k], a[k + j] = a[k] ^ t, a[k + j] ^ (t << j)
    return a


def _sortable(x):
    i = pltpu.bitcast(x, I32)
    return jnp.where(i < 0, i ^ jnp.int32(0x7FFFFFFF), i)


def _dsa_kernel(top_k, dq_ref, iq_ref, misc_ref, kidx_ref, clat_ref, wuk_ref, wuv_ref, band_ref,
                o_ref, keys_ref, planes_ref, eq_ref, iqs_ref, wb_ref, qlat_ref, madd_ref, s_ref, s2_ref, p_ref, alpha_ref,
                m_ref, l_ref, acc_ref):
    qb = pl.program_id(1)
    start = qb * QBLK
    hrows = lambda h: slice(h * QBLK, (h + 1) * QBLK)

    w_scale = IDX_HEADS ** -0.5 * IDX_DIM ** -0.5
    wq = misc_ref[0][:, GLA_RANK:GLA_RANK + IDX_HEADS] * w_scale
    for h in range(IDX_HEADS):
        wb_ref[hrows(h), :] = jnp.broadcast_to(wq[:, h:h + 1], (QBLK, 128))
        iqs_ref[hrows(h), :] = iq_ref[:, h * IDX_DIM:(h + 1) * IDX_DIM]

    row = lax.broadcasted_iota(I32, (QBLK, KCH), 0)
    lane = lax.broadcasted_iota(I32, (QBLK, KCH), 1)
    p_lim = start + (row // CHUNK + 1) * CHUNK + PAD_FRONT

    n_chunks = (start + PAD_FRONT + QBLK + KCH - 1) // KCH
    tiles_per_chunk = KCH // 128
    chunks_per_group = GROUP_KEYS // KCH

    def stage_planes(c, key):
        tiles = [key[:, i * 128:(i + 1) * 128] for i in range(tiles_per_chunk)]
        tiles = _transpose_stages(tiles, (2, 1))
        g = c // chunks_per_group
        w0 = (c % chunks_per_group) * tiles_per_chunk
        for i in range(tiles_per_chunk):
            planes_ref[w0 + i, :, pl.ds(pl.multiple_of(g * 128, 128), 128)] = tiles[i]

    no_key = jnp.full((QBLK, KCH), INT_MIN, I32)
    keys_ref[:, 0:KCH] = no_key
    stage_planes(jnp.int32(0), no_key)

    def index_chunk(c, dots_ref):
        off = pl.multiple_of(c * KCH, KCH)
        kc = kidx_ref[0, pl.ds(off, KCH), :]
        dots_ref[...] = _dot_nt(iqs_ref[...], kc)
        acc = jnp.zeros((QBLK, KCH), F32)
        for h in range(IDX_HEADS):
            wbh = wb_ref[hrows(h), :]
            acc = acc + jnp.concatenate([wbh] * (KCH // 128), axis=1) * jnp.maximum(dots_ref[hrows(h), :], 0.0)
        key = jnp.where(lane + off < p_lim, _sortable(acc), INT_MIN)
        keys_ref[:, pl.ds(off, KCH)] = key
        stage_planes(c, key)

    n_odd = (n_chunks - 1) % 2

    @pl.when(n_odd == 1)
    def _():
        index_chunk(1, s_ref)

    def idx_body(i, carry):
        c = 1 + n_odd + 2 * i
        index_chunk(c, s_ref)
        index_chunk(c + 1, s2_ref)
        return carry

    lax.fori_loop(0, (n_chunks - 1) // 2, idx_body, 0)

    n_groups = (n_chunks + chunks_per_group - 1) // chunks_per_group
    ngrp_max = eq_ref.shape[1] // 128

    def pad_body(c, carry):
        stage_planes(c, no_key)
        return carry

    lax.fori_loop(n_chunks, n_groups * chunks_per_group, pad_body, 0)

    def plane_body(idx, carry):
        g = idx // (QBLK // 16)
        gl = pl.ds(pl.multiple_of(g * 128, 128), 128)
        for half in range(2):
            r0 = pl.multiple_of((idx % (QBLK // 16)) * 16 + half * 8, 8)
            words = [planes_ref[j, pl.ds(r0, 8), gl] for j in range(32)]
            words = _transpose_stages(words, (16, 8, 4))
            words[0] = ~words[0]
            for i in range(32):
                planes_ref[31 - i, pl.ds(r0, 8), gl] = words[i]
        return carry

    lax.fori_loop(0, n_groups * (QBLK // 16), plane_body, 0)

    for g in range(ngrp_max):
        eq_ref[:, g * 128:(g + 1) * 128] = jnp.broadcast_to(jnp.where(g < n_groups, -1, 0), (QBLK, 128))

    def row_count(t):
        pc = lax.population_count(t)
        tot = pc[:, 0:128]
        for g in range(1, ngrp_max):
            tot = tot + pc[:, g * 128:(g + 1) * 128]
        return jnp.broadcast_to(jnp.sum(tot.astype(F32), axis=1, keepdims=True), (QBLK, 128))

    def pair_body(i, carry):
        prefix, above = carry
        b0 = 30 - 2 * i
        p1 = planes_ref[b0 + 1]
        p0 = planes_ref[b0]
        eq = eq_ref[...]
        e1 = eq & p1
        e0 = eq & ~p1
        t11 = e1 & p0
        t10 = e1 & ~p0
        t01 = e0 & p0
        t00 = e0 & ~p0
        s3 = above + row_count(t11)
        s2 = s3 + row_count(t10)
        s1 = s2 + row_count(t01)
        is3 = s3 >= top_k
        is2 = s2 >= top_k
        is1 = s1 >= top_k
        for g in range(ngrp_max):
            gs = slice(g * 128, (g + 1) * 128)
            eq_ref[:, gs] = jnp.where(is3, t11[:, gs], jnp.where(is2, t10[:, gs], jnp.where(is1, t01[:, gs], t00[:, gs])))
        above = jnp.where(is3, above, jnp.where(is2, s3, jnp.where(is1, s2, s1)))
        digit = jnp.where(is3, 3, jnp.where(is2, 2, jnp.where(is1, 1, 0)))
        return prefix | (digit << b0), above

    end = start + PAD_FRONT + QBLK
    n_att = (start + QBLK + KCH - 1) // KCH

    def chunk_off(j):
        return pl.multiple_of(end - KCH * (j + 1), 128)

    def absorbed_query(h):
        ql = _dot(dq_ref[:, h * DSA_DH:(h + 1) * DSA_DH], wuk_ref[h]) * (LOG2E * DSA_DH ** -0.5)
        qlat_ref[hrows(h), :] = ql.astype(BF16)

    def first_scores(h):
        s_ref[hrows(h), :] = _dot_nt(qlat_ref[hrows(h), :], clat_ref[0, pl.ds(chunk_off(0), KCH), :])

    carry = (jnp.zeros((QBLK, 128), I32), jnp.zeros((QBLK, 128), F32))
    for i in range(16):
        carry = pair_body(i, carry)
        for h in range(i * DSA_HEADS // 16, (i + 1) * DSA_HEADS // 16):
            absorbed_query(h)
            if h > 0:
                first_scores(h - 1)
    first_scores(DSA_HEADS - 1)
    prefix, above = carry
    thr = prefix ^ INT_MIN
    thr = jnp.maximum(thr, INT_MIN + 1)

    surplus = jnp.where(prefix != 0, above + row_count(eq_ref[...]) - top_k, 0.0)

    def reset_softmax():
        m_ref[...] = jnp.full(m_ref.shape, NEG, F32)
        l_ref[...] = jnp.zeros(l_ref.shape, F32)
        acc_ref[...] = jnp.zeros(acc_ref.shape, F32)

    def demote_surplus_ties():
        keep = top_k - above
        lane1 = lax.broadcasted_iota(I32, (QBLK, 128), 1)

        def tied_before(q):
            def body(c, cnt):
                off = pl.multiple_of(c * KCH, KCH)
                kk = keys_ref[:, pl.ds(off, KCH)]
                for s in range(KCH // 128):
                    hit = jnp.logical_and(kk[:, s * 128:(s + 1) * 128] == thr, lane1 + (off + s * 128) < q)
                    cnt = cnt + jnp.where(hit, 1, 0)
                return cnt
            cnt = lax.fori_loop(1, n_chunks, body, jnp.zeros((QBLK, 128), I32))
            return jnp.broadcast_to(jnp.sum(cnt.astype(F32), axis=1, keepdims=True), (QBLK, 128))

        def pos_body(i, q):
            cand = q | (jnp.int32(1) << (POS_BITS - 1 - i))
            return jnp.where(tied_before(cand) < keep, cand, q)

        last = lax.fori_loop(0, POS_BITS, pos_body, jnp.zeros((QBLK, 128), I32))
        last = jnp.where(surplus > 0.0, last, jnp.int32(2 ** POS_BITS))

        def demote_body(c, carry):
            off = pl.multiple_of(c * KCH, KCH)
            for s in range(KCH // 128):
                cs = pl.ds(off + s * 128, 128)
                kk = keys_ref[:, cs]
                drop = jnp.logical_and(kk == thr, lane1 + (off + s * 128) > last)
                keys_ref[:, cs] = jnp.where(drop, INT_MIN, kk)
            return carry

        lax.fori_loop(1, n_chunks, demote_body, 0)

    reset_softmax()

    def scores(j, dst_ref):
        cc = clat_ref[0, pl.ds(chunk_off(j), KCH), :]
        dst_ref[...] = _dot_nt(qlat_ref[...], cc)

    def softmax_update(j, src_ref, near):
        off = chunk_off(j)
        kk = keys_ref[:, pl.ds(off, KCH)]
        madd_ref[...] = jnp.where(kk >= jnp.concatenate([thr] * (KCH // 128), axis=1), 0.0, NEG)
        cc = clat_ref[0, pl.ds(off, KCH), :]
        for h in range(DSA_HEADS):
            s = src_ref[hrows(h), :] + madd_ref[...]
            if near:
                s = jnp.concatenate([s[:, :KCH - 2 * QBLK], s[:, KCH - 2 * QBLK:] + band_ref[h]], axis=1)
            m_old = m_ref[hrows(h), :]
            m_new = jnp.maximum(m_old, jnp.broadcast_to(jnp.max(s, axis=1, keepdims=True), (QBLK, 128)))
            alpha = jnp.exp2(m_old - m_new)
            p = jnp.exp2(s - jnp.concatenate([m_new] * (KCH // 128), axis=1))
            l_ref[hrows(h), :] = (alpha * l_ref[hrows(h), :]
                                  + jnp.broadcast_to(jnp.sum(p, axis=1, keepdims=True), (QBLK, 128)))
            m_ref[hrows(h), :] = m_new
            alpha_ref[hrows(h), :] = alpha
            p_ref[hrows(h), :] = p.astype(BF16)
        al = alpha_ref[...]
        acc_ref[...] = (jnp.concatenate([al] * (DSA_LATENT // 128), axis=1) * acc_ref[...]
                        + _dot(p_ref[...], cc))

    def step(j, cur_ref, nxt_ref, near):
        scores(jnp.minimum(j + 1, n_att - 1), nxt_ref)
        softmax_update(j, cur_ref, near)

    step(0, s_ref, s2_ref, True)

    @pl.when(jnp.max(surplus) > 0.0)
    def _():
        demote_surplus_ties()
        reset_softmax()
        softmax_update(0, s_ref, True)

    def att_body(i, carry):
        j = 2 * i + 1
        step(j, s2_ref, s_ref, False)
        step(j + 1, s_ref, s2_ref, False)
        return carry

    lax.fori_loop(0, (n_att - 1) // 2, att_body, 0)

    @pl.when((n_att - 1) % 2 == 1)
    def _():
        softmax_update(n_att - 1, s2_ref, False)

    for h in range(DSA_HEADS):
        inv = 1.0 / l_ref[hrows(h), :]
        o_lat = acc_ref[hrows(h), :] * jnp.concatenate([inv] * (DSA_LATENT // 128), axis=1)
        o_ref[:, h * DSA_DV:(h + 1) * DSA_DV] = _dot(o_lat.astype(BF16), wuv_ref[h]).astype(o_ref.dtype)


def _dsa(proj, misc, kidx, clat, w_uk, w_uv, band, batch, seq):
    nqb = seq // QBLK
    t = batch * seq
    npad = clat.shape[1]
    top_k = min(INDEX_TOPK, seq // 4)
    assert npad < 2 ** POS_BITS
    hq = DSA_HEADS * QBLK
    tokb = lambda b, i: b * nqb + i
    const3 = lambda b, i: (0, 0, 0)
    ngrp = (npad + GROUP_KEYS - 1) // GROUP_KEYS
    once = pl.Buffered(1)
    return pl.pallas_call(
        functools.partial(_dsa_kernel, top_k),
        grid=(batch, nqb),
        in_specs=[pl.BlockSpec((QBLK, 2048), lambda b, i: (tokb(b, i), 3)),
                  pl.BlockSpec((QBLK, 2048), lambda b, i: (tokb(b, i), 4)),
                  pl.BlockSpec((1, QBLK, 128), lambda b, i: (b, i + PAD_FRONT // QBLK, 0)),
                  pl.BlockSpec((1, npad, IDX_DIM), lambda b, i: (b, 0, 0), pipeline_mode=once),
                  pl.BlockSpec((1, npad, DSA_LATENT), lambda b, i: (b, 0, 0), pipeline_mode=once),
                  pl.BlockSpec((DSA_HEADS, DSA_DH, DSA_LATENT), const3, pipeline_mode=once),
                  pl.BlockSpec((DSA_HEADS, DSA_LATENT, DSA_DV), const3, pipeline_mode=once),
                  pl.BlockSpec((DSA_HEADS, QBLK, 2 * QBLK), const3, pipeline_mode=once)],
        out_specs=pl.BlockSpec((QBLK, DSA_HEADS * DSA_DV), lambda b, i: (tokb(b, i), 0)),
        out_shape=jax.ShapeDtypeStruct((t, DSA_HEADS * DSA_DV), BF16),
        scratch_shapes=[pltpu.VMEM((QBLK, npad), I32),
                        pltpu.VMEM((32, QBLK, ngrp * 128), I32),
                        pltpu.VMEM((QBLK, ngrp * 128), I32),
                        pltpu.VMEM((hq, IDX_DIM), BF16),
                        pltpu.VMEM((hq, 128), F32),
                        pltpu.VMEM((hq, DSA_LATENT), BF16),
                        pltpu.VMEM((QBLK, KCH), F32),
                        pltpu.VMEM((hq, KCH), F32),
                        pltpu.VMEM((hq, KCH), F32),
                        pltpu.VMEM((hq, KCH), BF16),
                        pltpu.VMEM((hq, 128), F32),
                        pltpu.VMEM((hq, 128), F32),
                        pltpu.VMEM((hq, 128), F32),
                        pltpu.VMEM((hq, DSA_LATENT), F32)],
        compiler_params=_cparams(("parallel", "arbitrary")),
        name="dsa",
    )(proj, proj, misc, kidx, clat, w_uk, w_uv, band)


def _mix_kernel(yg_ref, yd_ref, gg_ref, gd_ref, wg_ref, wd_ref, wm_ref, g_ref, h_ref, o_ref):
    a = _dot(yg_ref[...], wg_ref[...])
    b = _dot(yd_ref[...], wd_ref[...])
    gg = jax.nn.sigmoid(gg_ref[...].astype(F32))
    gd = jax.nn.sigmoid(gd_ref[...].astype(F32))
    merged = (gg * a + gd * b).astype(BF16)
    o_ref[...] = h_ref[...] + _rms(_dot(merged, wm_ref[...]), g_ref[...])


def _mix(y_gla, y_dsa, proj, w_g, w_d, w_m, gain, h, tm):
    t, d = y_gla.shape
    once = pl.Buffered(1)
    row = lambda i: (i, 0)
    fixed = lambda i: (0, 0)
    return pl.pallas_call(
        _mix_kernel,
        grid=(t // tm,),
        in_specs=[pl.BlockSpec((tm, d), row),
                  pl.BlockSpec((tm, d), row),
                  pl.BlockSpec((tm, d), lambda i: (i, 5)),
                  pl.BlockSpec((tm, d), lambda i: (i, 6)),
                  pl.BlockSpec((d, d), fixed, pipeline_mode=once),
                  pl.BlockSpec((d, d), fixed, pipeline_mode=once),
                  pl.BlockSpec((d, d), fixed, pipeline_mode=once),
                  pl.BlockSpec((1, d), fixed),
                  pl.BlockSpec((tm, d), row)],
        out_specs=pl.BlockSpec((tm, d), row),
        out_shape=jax.ShapeDtypeStruct((t, d), F32),
        compiler_params=_cparams(("parallel",)),
        name="mix",
    )(y_gla, y_dsa, proj, proj, w_g, w_d, w_m, gain, h)


def _xa_kernel(h_ref, g_ref, wq_ref, k_ref, v_ref, wo_ref, pg_ref, o_ref):
    h = h_ref[...]
    q = _dot(_rms(h, g_ref[...]).astype(BF16), wq_ref[...]).astype(BF16)
    outs = []
    for hd in range(XA_HEADS):
        hs = slice(hd * XA_DH, (hd + 1) * XA_DH)
        s = _dot_nt(q[:, hs], k_ref[0][:, hs]) * (XA_DH ** -0.5)
        m = jnp.max(s, axis=-1, keepdims=True)
        p = jnp.exp(s - m)
        p = p / jnp.sum(p, axis=-1, keepdims=True)
        outs.append(_dot(p.astype(BF16), v_ref[0][:, hs]).astype(BF16))
    y = _dot(jnp.concatenate(outs, axis=1), wo_ref[...])
    o_ref[...] = h + _rms(y, pg_ref[...])


def _xa(h, pre_gain, w_q, kv, w_o, post_gain, batch, seq, tm):
    t, d = h.shape
    n_mem = kv.shape[1]
    nb = seq // tm
    once = pl.Buffered(1)
    row = lambda b, i: (b * nb + i, 0)
    fixed = lambda b, i: (0, 0)
    return pl.pallas_call(
        _xa_kernel,
        grid=(batch, nb),
        in_specs=[pl.BlockSpec((tm, d), row),
                  pl.BlockSpec((1, d), fixed),
                  pl.BlockSpec((d, d), fixed, pipeline_mode=once),
                  pl.BlockSpec((1, n_mem, d), lambda b, i: (b, 0, 0)),
                  pl.BlockSpec((1, n_mem, d), lambda b, i: (b, 0, 1)),
                  pl.BlockSpec((d, d), fixed, pipeline_mode=once),
                  pl.BlockSpec((1, d), fixed)],
        out_specs=pl.BlockSpec((tm, d), row),
        out_shape=jax.ShapeDtypeStruct((t, d), F32),
        compiler_params=_cparams(("parallel", "parallel")),
        name="xa",
    )(h, pre_gain, w_q, kv, kv, w_o, post_gain)


def _ffn_kernel(h_ref, g_ref, wg_ref, wu_ref, wd_ref, pg_ref, o_ref, u_ref):
    f = pl.program_id(1)

    @pl.when(f == 0)
    def _():
        u_ref[...] = _rms(h_ref[...], g_ref[...]).astype(BF16)
        o_ref[...] = jnp.zeros_like(o_ref)

    u = u_ref[...]
    n_sub = max(wg_ref.shape[1] // 256, 1)
    half = wg_ref.shape[1] // n_sub
    down = None
    for s in range(n_sub):
        cs = slice(s * half, (s + 1) * half)
        a = _dot(u, wg_ref[:, cs])
        b = _dot(u, wu_ref[:, cs])
        act = (a * jax.nn.sigmoid(a) * b).astype(BF16)
        d = _dot(act, wd_ref[cs, :])
        down = d if down is None else down + d
    o_ref[...] += down

    @pl.when(f == pl.num_programs(1) - 1)
    def _():
        o_ref[...] = h_ref[...] + _rms(o_ref[...], pg_ref[...])


def _ffn(h, pre_gain, w_gate, w_up, w_down, post_gain, tm, tf):
    t, d = h.shape
    ff = w_gate.shape[1]
    return pl.pallas_call(
        _ffn_kernel,
        grid=(t // tm, ff // tf),
        in_specs=[pl.BlockSpec((tm, d), lambda i, f: (i, 0)),
                  pl.BlockSpec((1, d), lambda i, f: (0, 0)),
                  pl.BlockSpec((d, tf), lambda i, f: (0, f)),
                  pl.BlockSpec((d, tf), lambda i, f: (0, f)),
                  pl.BlockSpec((tf, d), lambda i, f: (f, 0)),
                  pl.BlockSpec((1, d), lambda i, f: (0, 0))],
        out_specs=pl.BlockSpec((tm, d), lambda i, f: (i, 0)),
        out_shape=jax.ShapeDtypeStruct((t, d), F32),
        scratch_shapes=[pltpu.VMEM((tm, d), BF16)],
        compiler_params=_cparams(("parallel", "arbitrary")),
        name="ffn",
    )(h, pre_gain, w_gate, w_up, w_down, post_gain)


def _row(v):
    return v.reshape(1, -1).astype(F32)


def _layer(h, mem, w_in, gla_w_a2, gla_b_a, gla_out_norm, dsa_w_uk, dsa_w_uv, dsa_latent_norm,
           idx_k_norm_w, idx_k_norm_b, band, w_gla_branch, w_dsa_branch, w_mix_out,
           mix_pre_norm, mix_post_norm, xa_pre_norm, xa_post_norm, xa_mem_norm,
           w_xa_q, w_xa_kv, w_xa_o, ffn_pre_norm, ffn_post_norm, w_ffn_gate, w_ffn_up, w_ffn_down,
           batch, seq):
    d = D_MODEL
    t = batch * seq
    cols = lambda a, b: w_in[:, a:b]
    w_main = jnp.concatenate(
        [cols(_O_GQ, _O_GA), cols(_O_DQ, _O_DC), cols(_O_IQ, _O_IK), cols(_O_GG, _O_END)], axis=1).astype(BF16)
    w_small = jnp.concatenate(
        [cols(_O_DC, _O_IQ), cols(_O_IK, _O_IW), cols(_O_GA, _O_DQ), cols(_O_IW, _O_GG),
         jnp.zeros((d, SMALL_COLS - 416), w_in.dtype)], axis=1).astype(BF16)

    pre = _row(mix_pre_norm)
    proj = _norm_matmul(h, pre, w_main, BF16, 1024, 2048)
    clat, kidx, misc = _small_proj(h, pre, w_small, _row(dsa_latent_norm), _row(idx_k_norm_w),
                                   _row(idx_k_norm_b), batch, seq)

    y_gla = _gla(proj, misc, gla_w_a2.astype(BF16), _row(gla_b_a), _row(gla_out_norm), batch, seq)
    y_dsa = _dsa(proj, misc, kidx, clat, dsa_w_uk.astype(BF16), dsa_w_uv.astype(BF16), band, batch, seq)

    h = _mix(y_gla, y_dsa, proj, w_gla_branch.astype(BF16), w_dsa_branch.astype(BF16),
             w_mix_out.astype(BF16), _row(mix_post_norm), h, 256)

    n_mem = mem.shape[1]
    kv = _norm_matmul(mem.reshape(batch * n_mem, d), _row(xa_mem_norm), w_xa_kv.astype(BF16), BF16,
                      batch * n_mem, 512).reshape(batch, n_mem, 2 * d)
    h = _xa(h, _row(xa_pre_norm), w_xa_q.astype(BF16), kv, w_xa_o.astype(BF16), _row(xa_post_norm),
            batch, seq, 512)

    h = _ffn(h, _row(ffn_pre_norm), w_ffn_gate.astype(BF16), w_ffn_up.astype(BF16),
             w_ffn_down.astype(BF16), _row(ffn_post_norm), 1024, 256)
    return h


def kernel(x, mem, w_in, gla_w_a2, gla_b_a, gla_out_norm, dsa_w_uk, dsa_w_uv, dsa_latent_norm,
           idx_k_norm_w, idx_k_norm_b, rel_bias, w_gla_branch, w_dsa_branch, w_mix_out,
           mix_pre_norm, mix_post_norm, xa_pre_norm, xa_post_norm, xa_mem_norm,
           w_xa_q, w_xa_kv, w_xa_o, ffn_pre_norm, ffn_post_norm, w_ffn_gate, w_ffn_up, w_ffn_down):
    batch, seq, d = x.shape
    depth = w_in.shape[0]
    band = _bias_band(rel_bias.astype(F32))
    h = x.reshape(batch * seq, d)
    for l in range(depth):
        h = _layer(h, mem, w_in[l], gla_w_a2[l], gla_b_a[l], gla_out_norm[l], dsa_w_uk[l], dsa_w_uv[l],
                   dsa_latent_norm[l], idx_k_norm_w[l], idx_k_norm_b[l], band, w_gla_branch[l],
                   w_dsa_branch[l], w_mix_out[l], mix_pre_norm[l], mix_post_norm[l], xa_pre_norm[l],
                   xa_post_norm[l], xa_mem_norm[l], w_xa_q[l], w_xa_kv[l], w_xa_o[l], ffn_pre_norm[l],
                   ffn_post_norm[l], w_ffn_gate[l], w_ffn_up[l], w_ffn_down[l], batch, seq)
    return h.reshape(batch, seq, d)
```

```python
import functools
import math

import jax
import jax.numpy as jnp
import numpy as np
from jax import lax
from jax.experimental import pallas as pl
from jax.experimental.pallas import tpu as pltpu

F32 = jnp.float32
BF16 = jnp.bfloat16
I32 = jnp.int32

D_MODEL = 2048
CHUNK = 64
EPS = 1e-6

GLA_HEADS = 4
GLA_DK = 256
GLA_DV = 512
GLA_RANK = 16
GLA_TAU = 16.0
GLA_SUB = 1

DSA_HEADS = 16
DSA_DH = 128
DSA_DV = 128
DSA_LATENT = 256
IDX_HEADS = 16
IDX_DIM = 128
INDEX_TOPK = 256

REL_BUCKETS = 32
REL_MAX_DIST = 128

XA_HEADS = 4
XA_DH = 512

_SPLITS = (1024, 1024, 2048, 2048, 16, 2048, 256, 2048, 128, 16, 2048, 2048)
_OFFS = tuple(int(v) for v in np.cumsum((0,) + _SPLITS))
(_O_GQ, _O_GK, _O_GV, _O_GR, _O_GA, _O_DQ, _O_DC, _O_IQ, _O_IK, _O_IW, _O_GG, _O_GD, _O_END) = _OFFS

MAIN_COLS = 14336
SMALL_COLS = 512

QBLK = 128
KCH = 512
GROUP_KEYS = 32 * 128
POS_BITS = 15
PAD_FRONT = KCH
INT_MIN = -2 ** 31
NEG = -1e30
LOG2E = 1.4426950408889634

VMEM_LIMIT = 58 * 1024 * 1024


def _cparams(sem):
    return pltpu.CompilerParams(dimension_semantics=sem, vmem_limit_bytes=VMEM_LIMIT)


def _rms(x, gain):
    ms = jnp.mean(x * x, axis=-1, keepdims=True)
    return x * lax.rsqrt(ms + EPS) * gain


def _dot(a, b):
    return jnp.dot(a, b, preferred_element_type=F32)


def _dot_nt(a, b):
    return lax.dot_general(a, b, (((1,), (1,)), ((), ())), preferred_element_type=F32)


def _norm_matmul_kernel(x_ref, g_ref, w_ref, o_ref, u_ref):
    @pl.when(pl.program_id(1) == 0)
    def _():
        u_ref[...] = _rms(x_ref[...], g_ref[...]).astype(BF16)

    o_ref[...] = _dot(u_ref[...], w_ref[...]).astype(o_ref.dtype)


def _norm_matmul(x, gain, w, out_dtype, tm, tn):
    t, d = x.shape
    n = w.shape[1]
    return pl.pallas_call(
        _norm_matmul_kernel,
        grid=(t // tm, n // tn),
        in_specs=[pl.BlockSpec((tm, d), lambda i, j: (i, 0)),
                  pl.BlockSpec((1, d), lambda i, j: (0, 0)),
                  pl.BlockSpec((d, tn), lambda i, j: (0, j))],
        out_specs=pl.BlockSpec((tm, tn), lambda i, j: (i, j)),
        out_shape=jax.ShapeDtypeStruct((t, n), out_dtype),
        scratch_shapes=[pltpu.VMEM((tm, d), BF16)],
        compiler_params=_cparams(("parallel", "arbitrary")),
        name="norm_matmul",
    )(x, gain, w)


def _small_proj(x, gain, w_small, lat_g, ik_w, ik_b, batch, seq):
    tm = PAD_FRONT
    assert seq % tm == 0
    nblk = seq // tm
    npad = PAD_FRONT + seq
    grid = (batch, 1 + nblk)

    def x_map(b, i):
        return (b * nblk + jnp.maximum(i - 1, 0), 0)

    def kernel(x_ref, g_ref, w_ref, lat_g_ref, ik_w_ref, ik_b_ref, c_ref, k_ref, misc_ref):
        is_pad = pl.program_id(1) == 0

        @pl.when(is_pad)
        def _():
            c_ref[...] = jnp.zeros_like(c_ref)
            k_ref[...] = jnp.zeros_like(k_ref)
            misc_ref[...] = jnp.zeros_like(misc_ref)

        @pl.when(jnp.logical_not(is_pad))
        def _():
            u = _rms(x_ref[...], g_ref[...]).astype(BF16)
            p = _dot(u, w_ref[...])
            c_ref[0] = _rms(p[:, 0:256], lat_g_ref[...]).astype(BF16)
            ik = p[:, 256:384]
            mu = jnp.mean(ik, axis=-1, keepdims=True)
            xc = ik - mu
            var = jnp.mean(xc * xc, axis=-1, keepdims=True)
            k_ref[0] = (xc * lax.rsqrt(var + EPS) * ik_w_ref[...] + ik_b_ref[...]).astype(BF16)
            misc_ref[0] = p[:, 384:512]

    d = x.shape[1]
    const = lambda b, i: (0, 0)
    return pl.pallas_call(
        kernel,
        grid=grid,
        in_specs=[pl.BlockSpec((tm, d), x_map),
                  pl.BlockSpec((1, d), const),
                  pl.BlockSpec((d, SMALL_COLS), const),
                  pl.BlockSpec((1, DSA_LATENT), const),
                  pl.BlockSpec((1, IDX_DIM), const),
                  pl.BlockSpec((1, IDX_DIM), const)],
        out_specs=[pl.BlockSpec((1, tm, DSA_LATENT), lambda b, i: (b, i, 0)),
                   pl.BlockSpec((1, tm, IDX_DIM), lambda b, i: (b, i, 0)),
                   pl.BlockSpec((1, tm, 128), lambda b, i: (b, i, 0))],
        out_shape=[jax.ShapeDtypeStruct((batch, npad, DSA_LATENT), BF16),
                   jax.ShapeDtypeStruct((batch, npad, IDX_DIM), BF16),
                   jax.ShapeDtypeStruct((batch, npad, 128), F32)],
        compiler_params=_cparams(("parallel", "arbitrary")),
        name="small_proj",
    )(x, gain, w_small, lat_g, ik_w, ik_b)


def _log_sigmoid(z):
    return jnp.minimum(z, 0.0) - jnp.log1p(jnp.exp(-jnp.abs(z)))


def _split3(x):
    h = x.astype(BF16)
    r = x - h.astype(F32)
    m = r.astype(BF16)
    l = (r - m.astype(F32)).astype(BF16)
    return h, m, l


def _gla_kernel(nb, q_ref, k_ref, v_ref, r_ref, misc_ref, wa2_ref, ba_ref, on_ref, o_ref, state_ref):
    @pl.when(pl.program_id(0) == 0)
    def _():
        state_ref[...] = jnp.zeros_like(state_ref)

    rows = nb * GLA_SUB * CHUNK
    stack = lambda ref, cols: jnp.concatenate([ref[i][:, cols] for i in range(nb)], axis=0)
    row = lax.broadcasted_iota(I32, (rows, rows), 0)
    col = lax.broadcasted_iota(I32, (rows, rows), 1)
    same = (row // CHUNK) == (col // CHUNK)
    lower = jnp.logical_and(same, col <= row)
    upper = jnp.logical_and(same, col > row)
    tril = jnp.where(lower, 1.0, 0.0).astype(BF16)

    a_low = stack(misc_ref, slice(0, GLA_RANK)).astype(BF16)
    z = _dot(a_low, wa2_ref[...]) + ba_ref[...]
    la = _log_sigmoid(z) * (1.0 / GLA_TAU)
    l_h, l_m, l_l = _split3(la)
    b_all = _dot(tril, l_h) + _dot(tril, l_m) + _dot(tril, l_l)

    for h in range(GLA_HEADS):
        ks = slice(h * GLA_DK, (h + 1) * GLA_DK)
        vs = slice(h * GLA_DV, (h + 1) * GLA_DV)
        b = b_all[:, ks]
        eb = jnp.exp(b)
        ebi = jnp.exp(-b)
        q = stack(q_ref, ks).astype(F32) * (GLA_DK ** -0.5)
        k = stack(k_ref, ks).astype(F32)
        v = stack(v_ref, vs)
        q_fwd = (q * eb).astype(BF16)
        a_lo = _dot_nt(q_fwd, (k * ebi).astype(BF16))
        a_up = _dot_nt((q * ebi).astype(BF16), (k * eb).astype(BF16))
        scores = jnp.where(lower, a_lo, jnp.where(upper, a_up, 0.0)).astype(BF16)
        o_intra = _dot(scores, v)
        outs = []
        for i in range(nb):
            st = state_ref[i * GLA_HEADS + h]
            for sub in range(GLA_SUB):
                rs = slice((i * GLA_SUB + sub) * CHUNK, (i * GLA_SUB + sub + 1) * CHUNK)
                b_i = b[rs]
                b_last = b_i[CHUNK - 1:CHUNK, :]
                outs.append(o_intra[rs] + _dot_nt(q_fwd[rs], st.astype(BF16)))
                k_dec = (k[rs] * jnp.exp(b_last - b_i)).astype(BF16)
                v_t = v[rs].astype(F32).T.astype(BF16)
                st = st * jnp.exp(b_last) + _dot(v_t, k_dec)
            state_ref[i * GLA_HEADS + h] = st
        o = _rms(jnp.concatenate(outs, axis=0), on_ref[...])
        r = stack(r_ref, vs).astype(F32)
        y = (o * (r * jax.nn.sigmoid(r))).astype(o_ref.dtype)
        blk = GLA_SUB * CHUNK
        for i in range(nb):
            o_ref[i, :, vs] = y[i * blk:(i + 1) * blk]


def _gla(proj, misc, w_a2, b_a, out_norm, batch, seq):
    blk = GLA_SUB * CHUNK
    assert seq % blk == 0 and PAD_FRONT % blk == 0
    proj3 = proj.reshape(batch, seq, proj.shape[1])
    y = pl.pallas_call(
        functools.partial(_gla_kernel, batch),
        grid=(seq // blk,),
        in_specs=[pl.BlockSpec((batch, blk, 1024), lambda c: (0, c, 0)),
                  pl.BlockSpec((batch, blk, 1024), lambda c: (0, c, 1)),
                  pl.BlockSpec((batch, blk, 2048), lambda c: (0, c, 1)),
                  pl.BlockSpec((batch, blk, 2048), lambda c: (0, c, 2)),
                  pl.BlockSpec((batch, blk, 128), lambda c: (0, c + PAD_FRONT // blk, 0)),
                  pl.BlockSpec((GLA_RANK, GLA_HEADS * GLA_DK), lambda c: (0, 0)),
                  pl.BlockSpec((1, GLA_HEADS * GLA_DK), lambda c: (0, 0)),
                  pl.BlockSpec((1, GLA_DV), lambda c: (0, 0))],
        out_specs=pl.BlockSpec((batch, blk, GLA_HEADS * GLA_DV), lambda c: (0, c, 0)),
        out_shape=jax.ShapeDtypeStruct((batch, seq, GLA_HEADS * GLA_DV), BF16),
        scratch_shapes=[pltpu.VMEM((batch * GLA_HEADS, GLA_DV, GLA_DK), F32)],
        compiler_params=_cparams(("arbitrary",)),
        name="gla",
    )(proj3, proj3, proj3, proj3, misc, w_a2, b_a, out_norm)
    return y.reshape(batch * seq, GLA_HEADS * GLA_DV)


def _t5_bucket(rel):
    half = REL_BUCKETS // 2
    max_exact = half // 2
    ret = jnp.where(rel > 0, half, 0)
    n = jnp.abs(rel)
    nf = jnp.maximum(n, 1).astype(jnp.float32)
    large = max_exact + (jnp.log(nf / max_exact) / math.log(REL_MAX_DIST / max_exact)
                         * (half - max_exact)).astype(jnp.int32)
    large = jnp.minimum(large, half - 1)
    return ret + jnp.where(n < max_exact, n, large)


def _bias_band_kernel(bucket_ref, rb_ref, o_ref):
    far = REL_BUCKETS // 2 - 1
    bucket = bucket_ref[...]
    for h in range(DSA_HEADS):
        acc = jnp.zeros(bucket.shape, F32)
        for b in range(REL_BUCKETS):
            acc = jnp.where(bucket == b, rb_ref[b, h], acc)
        o_ref[h] = (acc - rb_ref[far, h]) * LOG2E


def _bias_band(rel_bias):
    t = jnp.arange(QBLK, dtype=jnp.int32)[:, None]
    j = jnp.arange(2 * QBLK, dtype=jnp.int32)[None, :]
    bucket = _t5_bucket(j - QBLK - t).astype(jnp.int32)
    return pl.pallas_call(
        _bias_band_kernel,
        in_specs=[pl.BlockSpec(memory_space=pltpu.VMEM), pl.BlockSpec(memory_space=pltpu.SMEM)],
        out_specs=pl.BlockSpec(memory_space=pltpu.VMEM),
        out_shape=jax.ShapeDtypeStruct((DSA_HEADS, QBLK, 2 * QBLK), F32),
        name="bias_band",
    )(bucket, rel_bias)


_SWAP_MASK = {16: 0x0000FFFF, 8: 0x00FF00FF, 4: 0x0F0F0F0F, 2: 0x33333333, 1: 0x55555555}


def _transpose_stages(words, stages):
    a = list(words)
    for j in stages:
        for k in range(len(a)):
            if k & j == 0:
                t = (a[k] ^ lax.shift_right_logical(a[k + j], jnp.int32(j))) & jnp.int32(_SWAP_MASK[j])
                a[k], a[k + j] = a[k] ^ t, a[k + j] ^ (t << j)
    return a


def _sortable(x):
    i = pltpu.bitcast(x, I32)
    return jnp.where(i < 0, i ^ jnp.int32(0x7FFFFFFF), i)


def _dsa_kernel(top_k, dq_ref, iq_ref, misc_ref, kidx_ref, clat_ref, wuk_ref, wuv_ref, band_ref,
                o_ref, keys_ref, planes_ref, eq_ref, iqs_ref, wb_ref, qlat_ref, madd_ref, s_ref, s2_ref, p_ref, alpha_ref,
                m_ref, l_ref, acc_ref):
    qb = pl.program_id(1)
    start = qb * QBLK
    hrows = lambda h: slice(h * QBLK, (h + 1) * QBLK)

    w_scale = IDX_HEADS ** -0.5 * IDX_DIM ** -0.5
    wq = misc_ref[0][:, GLA_RANK:GLA_RANK + IDX_HEADS] * w_scale
    for h in range(IDX_HEADS):
        wb_ref[hrows(h), :] = jnp.broadcast_to(wq[:, h:h + 1], (QBLK, 128))
        iqs_ref[hrows(h), :] = iq_ref[:, h * IDX_DIM:(h + 1) * IDX_DIM]

    row = lax.broadcasted_iota(I32, (QBLK, KCH), 0)
    lane = lax.broadcasted_iota(I32, (QBLK, KCH), 1)
    p_lim = start + (row // CHUNK + 1) * CHUNK + PAD_FRONT

    n_chunks = (start + PAD_FRONT + QBLK + KCH - 1) // KCH
    tiles_per_chunk = KCH // 128
    chunks_per_group = GROUP_KEYS // KCH

    def stage_planes(c, key):
        tiles = [key[:, i * 128:(i + 1) * 128] for i in range(tiles_per_chunk)]
        tiles = _transpose_stages(tiles, (2, 1))
        g = c // chunks_per_group
        w0 = (c % chunks_per_group) * tiles_per_chunk
        for i in range(tiles_per_chunk):
            planes_ref[w0 + i, :, pl.ds(pl.multiple_of(g * 128, 128), 128)] = tiles[i]

    no_key = jnp.full((QBLK, KCH), INT_MIN, I32)
    keys_ref[:, 0:KCH] = no_key
    stage_planes(jnp.int32(0), no_key)

    def index_chunk(c, dots_ref):
        off = pl.multiple_of(c * KCH, KCH)
        kc = kidx_ref[0, pl.ds(off, KCH), :]
        dots_ref[...] = _dot_nt(iqs_ref[...], kc)
        acc = jnp.zeros((QBLK, KCH), F32)
        for h in range(IDX_HEADS):
            wbh = wb_ref[hrows(h), :]
            acc = acc + jnp.concatenate([wbh] * (KCH // 128), axis=1) * jnp.maximum(dots_ref[hrows(h), :], 0.0)
        key = jnp.where(lane + off < p_lim, _sortable(acc), INT_MIN)
        keys_ref[:, pl.ds(off, KCH)] = key
        stage_planes(c, key)

    n_odd = (n_chunks - 1) % 2

    @pl.when(n_odd == 1)
    def _():
        index_chunk(1, s_ref)

    def idx_body(i, carry):
        c = 1 + n_odd + 2 * i
        index_chunk(c, s_ref)
        index_chunk(c + 1, s2_ref)
        return carry

    lax.fori_loop(0, (n_chunks - 1) // 2, idx_body, 0)

    n_groups = (n_chunks + chunks_per_group - 1) // chunks_per_group
    ngrp_max = eq_ref.shape[1] // 128

    def pad_body(c, carry):
        stage_planes(c, no_key)
        return carry

    lax.fori_loop(n_chunks, n_groups * chunks_per_group, pad_body, 0)

    def plane_body(idx, carry):
        g = idx // (QBLK // 16)
        gl = pl.ds(pl.multiple_of(g * 128, 128), 128)
        for half in range(2):
            r0 = pl.multiple_of((idx % (QBLK // 16)) * 16 + half * 8, 8)
            words = [planes_ref[j, pl.ds(r0, 8), gl] for j in range(32)]
            words = _transpose_stages(words, (16, 8, 4))
            words[0] = ~words[0]
            for i in range(32):
                planes_ref[31 - i, pl.ds(r0, 8), gl] = words[i]
        return carry

    lax.fori_loop(0, n_groups * (QBLK // 16), plane_body, 0)

    for g in range(ngrp_max):
        eq_ref[:, g * 128:(g + 1) * 128] = jnp.broadcast_to(jnp.where(g < n_groups, -1, 0), (QBLK, 128))

    def row_count(t):
        pc = lax.population_count(t)
        tot = pc[:, 0:128]
        for g in range(1, ngrp_max):
            tot = tot + pc[:, g * 128:(g + 1) * 128]
        return jnp.broadcast_to(jnp.sum(tot.astype(F32), axis=1, keepdims=True), (QBLK, 128))

    def pair_body(i, carry):
        prefix, above = carry
        b0 = 30 - 2 * i
        p1 = planes_ref[b0 + 1]
        p0 = planes_ref[b0]
        eq = eq_ref[...]
        e1 = eq & p1
        e0 = eq & ~p1
        t11 = e1 & p0
        t10 = e1 & ~p0
        t01 = e0 & p0
        t00 = e0 & ~p0
        s3 = above + row_count(t11)
        s2 = s3 + row_count(t10)
        s1 = s2 + row_count(t01)
        is3 = s3 >= top_k
        is2 = s2 >= top_k
        is1 = s1 >= top_k
        for g in range(ngrp_max):
            gs = slice(g * 128, (g + 1) * 128)
            eq_ref[:, gs] = jnp.where(is3, t11[:, gs], jnp.where(is2, t10[:, gs], jnp.where(is1, t01[:, gs], t00[:, gs])))
        above = jnp.where(is3, above, jnp.where(is2, s3, jnp.where(is1, s2, s1)))
        digit = jnp.where(is3, 3, jnp.where(is2, 2, jnp.where(is1, 1, 0)))
        return prefix | (digit << b0), above

    end = start + PAD_FRONT + QBLK
    n_att = (start + QBLK + KCH - 1) // KCH

    def chunk_off(j):
        return pl.multiple_of(end - KCH * (j + 1), 128)

    def absorbed_query(h):
        ql = _dot(dq_ref[:, h * DSA_DH:(h + 1) * DSA_DH], wuk_ref[h]) * (LOG2E * DSA_DH ** -0.5)
        qlat_ref[hrows(h), :] = ql.astype(BF16)

    def first_scores(h):
        s_ref[hrows(h), :] = _dot_nt(qlat_ref[hrows(h), :], clat_ref[0, pl.ds(chunk_off(0), KCH), :])

    carry = (jnp.zeros((QBLK, 128), I32), jnp.zeros((QBLK, 128), F32))
    for i in range(16):
        carry = pair_body(i, carry)
        for h in range(i * DSA_HEADS // 16, (i + 1) * DSA_HEADS // 16):
            absorbed_query(h)
            if h > 0:
                first_scores(h - 1)
    first_scores(DSA_HEADS - 1)
    prefix, above = carry
    thr = prefix ^ INT_MIN
    thr = jnp.maximum(thr, INT_MIN + 1)

    surplus = jnp.where(prefix != 0, above + row_count(eq_ref[...]) - top_k, 0.0)

    def reset_softmax():
        m_ref[...] = jnp.full(m_ref.shape, NEG, F32)
        l_ref[...] = jnp.zeros(l_ref.shape, F32)
        acc_ref[...] = jnp.zeros(acc_ref.shape, F32)

    def demote_surplus_ties():
        keep = top_k - above
        lane1 = lax.broadcasted_iota(I32, (QBLK, 128), 1)

        def tied_before(q):
            def body(c, cnt):
                off = pl.multiple_of(c * KCH, KCH)
                kk = keys_ref[:, pl.ds(off, KCH)]
                for s in range(KCH // 128):
                    hit = jnp.logical_and(kk[:, s * 128:(s + 1) * 128] == thr, lane1 + (off + s * 128) < q)
                    cnt = cnt + jnp.where(hit, 1, 0)
                return cnt
            cnt = lax.fori_loop(1, n_chunks, body, jnp.zeros((QBLK, 128), I32))
            return jnp.broadcast_to(jnp.sum(cnt.astype(F32), axis=1, keepdims=True), (QBLK, 128))

        def pos_body(i, q):
            cand = q | (jnp.int32(1) << (POS_BITS - 1 - i))
            return jnp.where(tied_before(cand) < keep, cand, q)

        last = lax.fori_loop(0, POS_BITS, pos_body, jnp.zeros((QBLK, 128), I32))
        last = jnp.where(surplus > 0.0, last, jnp.int32(2 ** POS_BITS))

        def demote_body(c, carry):
            off = pl.multiple_of(c * KCH, KCH)
            for s in range(KCH // 128):
                cs = pl.ds(off + s * 128, 128)
                kk = keys_ref[:, cs]
                drop = jnp.logical_and(kk == thr, lane1 + (off + s * 128) > last)
                keys_ref[:, cs] = jnp.where(drop, INT_MIN, kk)
            return carry

        lax.fori_loop(1, n_chunks, demote_body, 0)

    reset_softmax()

    def scores(j, dst_ref):
        cc = clat_ref[0, pl.ds(chunk_off(j), KCH), :]
        dst_ref[...] = _dot_nt(qlat_ref[...], cc)

    def softmax_update(j, src_ref, near):
        off = chunk_off(j)
        kk = keys_ref[:, pl.ds(off, KCH)]
        madd_ref[...] = jnp.where(kk >= jnp.concatenate([thr] * (KCH // 128), axis=1), 0.0, NEG)
        cc = clat_ref[0, pl.ds(off, KCH), :]
        slab = QBLK // 2
        for h, r in [(h, r) for h in range(DSA_HEADS) for r in range(2)]:
            qs = slice(r * slab, (r + 1) * slab)
            rs = slice(h * QBLK + r * slab, h * QBLK + (r + 1) * slab)
            s = src_ref[rs, :] + madd_ref[qs, :]
            if near:
                s = jnp.concatenate([s[:, :KCH - 2 * QBLK], s[:, KCH - 2 * QBLK:] + band_ref[h, qs, :]], axis=1)
            m_old = m_ref[rs, :]
            m_new = jnp.maximum(m_old, jnp.broadcast_to(jnp.max(s, axis=1, keepdims=True), (slab, 128)))
            alpha = jnp.exp2(m_old - m_new)
            p = jnp.exp2(s - jnp.concatenate([m_new] * (KCH // 128), axis=1))
            l_ref[rs, :] = (alpha * l_ref[rs, :]
                            + jnp.broadcast_to(jnp.sum(p, axis=1, keepdims=True), (slab, 128)))
            m_ref[rs, :] = m_new
            alpha_ref[rs, :] = alpha
            p_ref[rs, :] = p.astype(BF16)
        al = alpha_ref[...]
        acc_ref[...] = (jnp.concatenate([al] * (DSA_LATENT // 128), axis=1) * acc_ref[...]
                        + _dot(p_ref[...], cc))

    def step(j, cur_ref, nxt_ref, near):
        scores(jnp.minimum(j + 1, n_att - 1), nxt_ref)
        softmax_update(j, cur_ref, near)

    step(0, s_ref, s2_ref, True)

    @pl.when(jnp.max(surplus) > 0.0)
    def _():
        demote_surplus_ties()
        reset_softmax()
        softmax_update(0, s_ref, True)

    def att_body(i, carry):
        j = 2 * i + 1
        step(j, s2_ref, s_ref, False)
        step(j + 1, s_ref, s2_ref, False)
        return carry

    lax.fori_loop(0, (n_att - 1) // 2, att_body, 0)

    @pl.when((n_att - 1) % 2 == 1)
    def _():
        softmax_update(n_att - 1, s2_ref, False)

    for h in range(DSA_HEADS):
        inv = 1.0 / l_ref[hrows(h), :]
        o_lat = acc_ref[hrows(h), :] * jnp.concatenate([inv] * (DSA_LATENT // 128), axis=1)
        o_ref[:, h * DSA_DV:(h + 1) * DSA_DV] = _dot(o_lat.astype(BF16), wuv_ref[h]).astype(o_ref.dtype)


def _dsa(proj, misc, kidx, clat, w_uk, w_uv, band, batch, seq):
    nqb = seq // QBLK
    t = batch * seq
    npad = clat.shape[1]
    top_k = min(INDEX_TOPK, seq // 4)
    assert npad < 2 ** POS_BITS
    hq = DSA_HEADS * QBLK
    tokb = lambda b, i: b * nqb + i
    const3 = lambda b, i: (0, 0, 0)
    ngrp = (npad + GROUP_KEYS - 1) // GROUP_KEYS
    once = pl.Buffered(1)
    return pl.pallas_call(
        functools.partial(_dsa_kernel, top_k),
        grid=(batch, nqb),
        in_specs=[pl.BlockSpec((QBLK, 2048), lambda b, i: (tokb(b, i), 3)),
                  pl.BlockSpec((QBLK, 2048), lambda b, i: (tokb(b, i), 4)),
                  pl.BlockSpec((1, QBLK, 128), lambda b, i: (b, i + PAD_FRONT // QBLK, 0)),
                  pl.BlockSpec((1, npad, IDX_DIM), lambda b, i: (b, 0, 0), pipeline_mode=once),
                  pl.BlockSpec((1, npad, DSA_LATENT), lambda b, i: (b, 0, 0), pipeline_mode=once),
                  pl.BlockSpec((DSA_HEADS, DSA_DH, DSA_LATENT), const3, pipeline_mode=once),
                  pl.BlockSpec((DSA_HEADS, DSA_LATENT, DSA_DV), const3, pipeline_mode=once),
                  pl.BlockSpec((DSA_HEADS, QBLK, 2 * QBLK), const3, pipeline_mode=once)],
        out_specs=pl.BlockSpec((QBLK, DSA_HEADS * DSA_DV), lambda b, i: (tokb(b, i), 0)),
        out_shape=jax.ShapeDtypeStruct((t, DSA_HEADS * DSA_DV), BF16),
        scratch_shapes=[pltpu.VMEM((QBLK, npad), I32),
                        pltpu.VMEM((32, QBLK, ngrp * 128), I32),
                        pltpu.VMEM((QBLK, ngrp * 128), I32),
                        pltpu.VMEM((hq, IDX_DIM), BF16),
                        pltpu.VMEM((hq, 128), F32),
                        pltpu.VMEM((hq, DSA_LATENT), BF16),
                        pltpu.VMEM((QBLK, KCH), F32),
                        pltpu.VMEM((hq, KCH), F32),
                        pltpu.VMEM((hq, KCH), F32),
                        pltpu.VMEM((hq, KCH), BF16),
                        pltpu.VMEM((hq, 128), F32),
                        pltpu.VMEM((hq, 128), F32),
                        pltpu.VMEM((hq, 128), F32),
                        pltpu.VMEM((hq, DSA_LATENT), F32)],
        compiler_params=_cparams(("parallel", "arbitrary")),
        name="dsa",
    )(proj, proj, misc, kidx, clat, w_uk, w_uv, band)


def _mix_kernel(yg_ref, yd_ref, gg_ref, gd_ref, wg_ref, wd_ref, wm_ref, g_ref, h_ref, o_ref):
    a = _dot(yg_ref[...], wg_ref[...])
    b = _dot(yd_ref[...], wd_ref[...])
    gg = jax.nn.sigmoid(gg_ref[...].astype(F32))
    gd = jax.nn.sigmoid(gd_ref[...].astype(F32))
    merged = (gg * a + gd * b).astype(BF16)
    o_ref[...] = h_ref[...] + _rms(_dot(merged, wm_ref[...]), g_ref[...])


def _mix(y_gla, y_dsa, proj, w_g, w_d, w_m, gain, h, tm):
    t, d = y_gla.shape
    once = pl.Buffered(1)
    row = lambda i: (i, 0)
    fixed = lambda i: (0, 0)
    return pl.pallas_call(
        _mix_kernel,
        grid=(t // tm,),
        in_specs=[pl.BlockSpec((tm, d), row),
                  pl.BlockSpec((tm, d), row),
                  pl.BlockSpec((tm, d), lambda i: (i, 5)),
                  pl.BlockSpec((tm, d), lambda i: (i, 6)),
                  pl.BlockSpec((d, d), fixed, pipeline_mode=once),
                  pl.BlockSpec((d, d), fixed, pipeline_mode=once),
                  pl.BlockSpec((d, d), fixed, pipeline_mode=once),
                  pl.BlockSpec((1, d), fixed),
                  pl.BlockSpec((tm, d), row)],
        out_specs=pl.BlockSpec((tm, d), row),
        out_shape=jax.ShapeDtypeStruct((t, d), F32),
        compiler_params=_cparams(("parallel",)),
        name="mix",
    )(y_gla, y_dsa, proj, proj, w_g, w_d, w_m, gain, h)


def _xa_kernel(h_ref, g_ref, wq_ref, k_ref, v_ref, wo_ref, pg_ref, o_ref):
    h = h_ref[...]
    q = _dot(_rms(h, g_ref[...]).astype(BF16), wq_ref[...]).astype(BF16)
    outs = []
    for hd in range(XA_HEADS):
        hs = slice(hd * XA_DH, (hd + 1) * XA_DH)
        s = _dot_nt(q[:, hs], k_ref[0][:, hs]) * (XA_DH ** -0.5)
        m = jnp.max(s, axis=-1, keepdims=True)
        p = jnp.exp(s - m)
        p = p / jnp.sum(p, axis=-1, keepdims=True)
        outs.append(_dot(p.astype(BF16), v_ref[0][:, hs]).astype(BF16))
    y = _dot(jnp.concatenate(outs, axis=1), wo_ref[...])
    o_ref[...] = h + _rms(y, pg_ref[...])


def _xa(h, pre_gain, w_q, kv, w_o, post_gain, batch, seq, tm):
    t, d = h.shape
    n_mem = kv.shape[1]
    nb = seq // tm
    once = pl.Buffered(1)
    row = lambda b, i: (b * nb + i, 0)
    fixed = lambda b, i: (0, 0)
    return pl.pallas_call(
        _xa_kernel,
        grid=(batch, nb),
        in_specs=[pl.BlockSpec((tm, d), row),
                  pl.BlockSpec((1, d), fixed),
                  pl.BlockSpec((d, d), fixed, pipeline_mode=once),
                  pl.BlockSpec((1, n_mem, d), lambda b, i: (b, 0, 0)),
                  pl.BlockSpec((1, n_mem, d), lambda b, i: (b, 0, 1)),
                  pl.BlockSpec((d, d), fixed, pipeline_mode=once),
                  pl.BlockSpec((1, d), fixed)],
        out_specs=pl.BlockSpec((tm, d), row),
        out_shape=jax.ShapeDtypeStruct((t, d), F32),
        compiler_params=_cparams(("parallel", "parallel")),
        name="xa",
    )(h, pre_gain, w_q, kv, kv, w_o, post_gain)


def _ffn_kernel(h_ref, g_ref, wg_ref, wu_ref, wd_ref, pg_ref, o_ref, u_ref):
    f = pl.program_id(1)

    @pl.when(f == 0)
    def _():
        u_ref[...] = _rms(h_ref[...], g_ref[...]).astype(BF16)
        o_ref[...] = jnp.zeros_like(o_ref)

    u = u_ref[...]
    n_sub = max(wg_ref.shape[1] // 256, 1)
    half = wg_ref.shape[1] // n_sub
    down = None
    for s in range(n_sub):
        cs = slice(s * half, (s + 1) * half)
        a = _dot(u, wg_ref[:, cs])
        b = _dot(u, wu_ref[:, cs])
        act = (a * jax.nn.sigmoid(a) * b).astype(BF16)
        d = _dot(act, wd_ref[cs, :])
        down = d if down is None else down + d
    o_ref[...] += down

    @pl.when(f == pl.num_programs(1) - 1)
    def _():
        o_ref[...] = h_ref[...] + _rms(o_ref[...], pg_ref[...])


def _ffn(h, pre_gain, w_gate, w_up, w_down, post_gain, tm, tf):
    t, d = h.shape
    ff = w_gate.shape[1]
    return pl.pallas_call(
        _ffn_kernel,
        grid=(t // tm, ff // tf),
        in_specs=[pl.BlockSpec((tm, d), lambda i, f: (i, 0)),
                  pl.BlockSpec((1, d), lambda i, f: (0, 0)),
                  pl.BlockSpec((d, tf), lambda i, f: (0, f)),
                  pl.BlockSpec((d, tf), lambda i, f: (0, f)),
                  pl.BlockSpec((tf, d), lambda i, f: (f, 0)),
                  pl.BlockSpec((1, d), lambda i, f: (0, 0))],
        out_specs=pl.BlockSpec((tm, d), lambda i, f: (i, 0)),
        out_shape=jax.ShapeDtypeStruct((t, d), F32),
        scratch_shapes=[pltpu.VMEM((tm, d), BF16)],
        compiler_params=_cparams(("parallel", "arbitrary")),
        name="ffn",
    )(h, pre_gain, w_gate, w_up, w_down, post_gain)


def _row(v):
    return v.reshape(1, -1).astype(F32)


def _layer(h, mem, w_in, gla_w_a2, gla_b_a, gla_out_norm, dsa_w_uk, dsa_w_uv, dsa_latent_norm,
           idx_k_norm_w, idx_k_norm_b, band, w_gla_branch, w_dsa_branch, w_mix_out,
           mix_pre_norm, mix_post_norm, xa_pre_norm, xa_post_norm, xa_mem_norm,
           w_xa_q, w_xa_kv, w_xa_o, ffn_pre_norm, ffn_post_norm, w_ffn_gate, w_ffn_up, w_ffn_down,
           batch, seq):
    d = D_MODEL
    t = batch * seq
    cols = lambda a, b: w_in[:, a:b]
    w_main = jnp.concatenate(
        [cols(_O_GQ, _O_GA), cols(_O_DQ, _O_DC), cols(_O_IQ, _O_IK), cols(_O_GG, _O_END)], axis=1).astype(BF16)
    w_small = jnp.concatenate(
        [cols(_O_DC, _O_IQ), cols(_O_IK, _O_IW), cols(_O_GA, _O_DQ), cols(_O_IW, _O_GG),
         jnp.zeros((d, SMALL_COLS - 416), w_in.dtype)], axis=1).astype(BF16)

    pre = _row(mix_pre_norm)
    proj = _norm_matmul(h, pre, w_main, BF16, 1024, 2048)
    clat, kidx, misc = _small_proj(h, pre, w_small, _row(dsa_latent_norm), _row(idx_k_norm_w),
                                   _row(idx_k_norm_b), batch, seq)

    y_gla = _gla(proj, misc, gla_w_a2.astype(BF16), _row(gla_b_a), _row(gla_out_norm), batch, seq)
    y_dsa = _dsa(proj, misc, kidx, clat, dsa_w_uk.astype(BF16), dsa_w_uv.astype(BF16), band, batch, seq)

    h = _mix(y_gla, y_dsa, proj, w_gla_branch.astype(BF16), w_dsa_branch.astype(BF16),
             w_mix_out.astype(BF16), _row(mix_post_norm), h, 256)

    n_mem = mem.shape[1]
    kv = _norm_matmul(mem.reshape(batch * n_mem, d), _row(xa_mem_norm), w_xa_kv.astype(BF16), BF16,
                      batch * n_mem, 512).reshape(batch, n_mem, 2 * d)
    h = _xa(h, _row(xa_pre_norm), w_xa_q.astype(BF16), kv, w_xa_o.astype(BF16), _row(xa_post_norm),
            batch, seq, 512)

    h = _ffn(h, _row(ffn_pre_norm), w_ffn_gate.astype(BF16), w_ffn_up.astype(BF16),
             w_ffn_down.astype(BF16), _row(ffn_post_norm), 1024, 256)
    return h


def kernel(x, mem, w_in, gla_w_a2, gla_b_a, gla_out_norm, dsa_w_uk, dsa_w_uv, dsa_latent_norm,
           idx_k_norm_w, idx_k_norm_b, rel_bias, w_gla_branch, w_dsa_branch, w_mix_out,
           mix_pre_norm, mix_post_norm, xa_pre_norm, xa_post_norm, xa_mem_norm,
           w_xa_q, w_xa_kv, w_xa_o, ffn_pre_norm, ffn_post_norm, w_ffn_gate, w_ffn_up, w_ffn_down):
    batch, seq, d = x.shape
    depth = w_in.shape[0]
    band = _bias_band(rel_bias.astype(F32))
    h = x.reshape(batch * seq, d)
    for l in range(depth):
        h = _layer(h, mem, w_in[l], gla_w_a2[l], gla_b_a[l], gla_out_norm[l], dsa_w_uk[l], dsa_w_uv[l],
                   dsa_latent_norm[l], idx_k_norm_w[l], idx_k_norm_b[l], band, w_gla_branch[l],
                   w_dsa_branch[l], w_mix_out[l], mix_pre_norm[l], mix_post_norm[l], xa_pre_norm[l],
                   xa_post_norm[l], xa_mem_norm[l], w_xa_q[l], w_xa_kv[l], w_xa_o[l], ffn_pre_norm[l],
                   ffn_post_norm[l], w_ffn_gate[l], w_ffn_up[l], w_ffn_down[l], batch, seq)
    return h.reshape(batch, seq, d)
```
